```python
import jax, jax.numpy as jnp
from jax import lax
import numpy as np


D_MODEL = 1024
BATCH = 8
SEQ = 4096
DEPTH = 4

GRID_W = 64
CTX_LEN = 256
EPS = 1e-6

FOURIER_HEADS = 4
FOURIER_HEAD_DIM = D_MODEL // 8
FOURIER_WIDTH = FOURIER_HEADS * FOURIER_HEAD_DIM
HEAD_DIM = 64
N_Q_HEADS = (D_MODEL // 2) // HEAD_DIM
N_KV_HEADS = 2
GQA_GROUP = N_Q_HEADS // N_KV_HEADS
ATTN_WIDTH = N_Q_HEADS * HEAD_DIM
KV_WIDTH = N_KV_HEADS * HEAD_DIM
Q_END = FOURIER_WIDTH + ATTN_WIDTH
IN_WIDTH = Q_END + 2 * KV_WIDTH
MIX_WIDTH = FOURIER_WIDTH + ATTN_WIDTH
WINDOW = 128
BLOCK = 128
ROPE_THETA = 10000.0
POOL_WINDOWS = (2, 4, 8, 16)
POOL_GROUP = D_MODEL // len(POOL_WINDOWS)
FFN_HIDDEN = ((-(-8 * D_MODEL // 3)) + 255) // 256 * 256
N_EVEN = (DEPTH + 1) // 2
N_ODD = DEPTH // 2

kernel_name = "hybrid_fourier_window_pool_dit"


def rms_norm(x, g):
    xf = x.astype(jnp.float32)
    y = xf * lax.rsqrt(jnp.mean(xf * xf, axis=-1, keepdims=True) + EPS)
    return (y * g.astype(jnp.float32)).astype(x.dtype)


def ada_mod(cond, w, b):
    m = jax.nn.silu(cond) @ w + b
    return jnp.split(m[..., None, :], 6, axis=-1)


def modulate(h, shift, scale):
    return h * (1 + scale) + shift


def axial_rope_tables(n_tokens):
    rows = n_tokens // GRID_W
    row = jnp.repeat(jnp.arange(rows, dtype=jnp.float32), GRID_W)
    col = jnp.tile(jnp.arange(GRID_W, dtype=jnp.float32), rows)
    n_freq = HEAD_DIM // 4
    inv = ROPE_THETA ** (-jnp.arange(n_freq, dtype=jnp.float32) / n_freq)
    ang = jnp.concatenate([row[:, None] * inv[None], col[:, None] * inv[None]], axis=-1)
    return jnp.cos(ang), jnp.sin(ang)


def apply_rope(x, cos, sin):
    xf = x.astype(jnp.float32).reshape(*x.shape[:-1], HEAD_DIM // 2, 2)
    x1, x2 = xf[..., 0], xf[..., 1]
    c, s = cos[:, None, :], sin[:, None, :]
    out = jnp.stack([x1 * c - x2 * s, x1 * s + x2 * c], axis=-1).reshape(x.shape)
    return out.astype(x.dtype)


def fourier_mix(u):
    B, N, _ = u.shape
    uh = u.astype(jnp.float32).reshape(B, N, FOURIER_HEADS, FOURIER_HEAD_DIM)
    y = jnp.fft.fftn(uh, axes=(1, 3), norm='ortho').real
    return y.reshape(B, N, FOURIER_WIDTH).astype(u.dtype)


def sink_logits(sink, lead_shape):
    s = sink.astype(jnp.float32).reshape(1, N_KV_HEADS, GQA_GROUP, 1, 1)
    return jnp.broadcast_to(s, lead_shape + (1,))


def window_attention(q, k, v, k_ctx, v_ctx, sink):
    B, S = q.shape[:2]
    L = k_ctx.shape[1]
    nb = S // BLOCK
    scale = HEAD_DIM ** -0.5
    pad = ((0, 0), (BLOCK, BLOCK), (0, 0), (0, 0))
    kp, vp = jnp.pad(k, pad), jnp.pad(v, pad)
    qb = q.reshape(B, nb, BLOCK, N_KV_HEADS, GQA_GROUP, HEAD_DIM).transpose(1, 0, 2, 3, 4, 5)

    def one_block(args):
        i, q_i = args
        start = i * BLOCK
        k_i = lax.dynamic_slice_in_dim(kp, start, 3 * BLOCK, axis=1)
        v_i = lax.dynamic_slice_in_dim(vp, start, 3 * BLOCK, axis=1)
        qpos = start + jnp.arange(BLOCK)
        kpos = start - BLOCK + jnp.arange(3 * BLOCK)
        valid = (jnp.abs(kpos[None, :] - qpos[:, None]) <= WINDOW) & (kpos[None, :] >= 0) & (kpos[None, :] < S)
        s_win = jnp.einsum('bqkgd,bjkd->bkgqj', q_i, k_i).astype(jnp.float32) * scale
        s_win = jnp.where(valid[None, None, None], s_win, -jnp.inf)
        s_ctx = jnp.einsum('bqkgd,bckd->bkgqc', q_i, k_ctx).astype(jnp.float32) * scale
        logits = jnp.concatenate([s_win, s_ctx, sink_logits(sink, s_win.shape[:-1])], axis=-1)
        p = jax.nn.softmax(logits, axis=-1)
        p_win = p[..., :3 * BLOCK].astype(v.dtype)
        p_ctx = p[..., 3 * BLOCK:3 * BLOCK + L].astype(v.dtype)
        return (jnp.einsum('bkgqj,bjkd->bqkgd', p_win, v_i)
                + jnp.einsum('bkgqc,bckd->bqkgd', p_ctx, v_ctx))

    o = lax.map(one_block, (jnp.arange(nb), qb))
    return o.transpose(1, 0, 2, 3, 4, 5).reshape(B, S, ATTN_WIDTH)


def context_attention(q_c, k_c, v_c, sink):
    B, L = q_c.shape[:2]
    qg = q_c.reshape(B, L, N_KV_HEADS, GQA_GROUP, HEAD_DIM)
    s = jnp.einsum('bqkgd,bckd->bkgqc', qg, k_c).astype(jnp.float32) * HEAD_DIM ** -0.5
    p = jax.nn.softmax(jnp.concatenate([s, sink_logits(sink, s.shape[:-1])], axis=-1), axis=-1)
    p = p[..., :L].astype(v_c.dtype)
    return jnp.einsum('bkgqc,bckd->bqkgd', p, v_c).reshape(B, L, ATTN_WIDTH)


def even_mixer(xn, xcn, w_in, w_out, sink, cos, sin, with_ctx_out):
    B, S, _ = xn.shape
    L = xcn.shape[1]
    proj = xn @ w_in
    f_in = proj[..., :FOURIER_WIDTH]
    q = proj[..., FOURIER_WIDTH:Q_END].reshape(B, S, N_Q_HEADS, HEAD_DIM)
    k = proj[..., Q_END:Q_END + KV_WIDTH].reshape(B, S, N_KV_HEADS, HEAD_DIM)
    v = proj[..., Q_END + KV_WIDTH:].reshape(B, S, N_KV_HEADS, HEAD_DIM)
    q = apply_rope(q, cos, sin)
    k = apply_rope(k, cos, sin)
    kv_c = xcn @ w_in[:, Q_END:]
    k_c = kv_c[..., :KV_WIDTH].reshape(B, L, N_KV_HEADS, HEAD_DIM)
    v_c = kv_c[..., KV_WIDTH:].reshape(B, L, N_KV_HEADS, HEAD_DIM)
    attn = window_attention(q, k, v, k_c, v_c, sink)
    y = jnp.concatenate([fourier_mix(f_in), attn], axis=-1) @ w_out
    if not with_ctx_out:
        return y, None
    fq_c = xcn @ w_in[:, :Q_END]
    f_c = fq_c[..., :FOURIER_WIDTH]
    q_c = fq_c[..., FOURIER_WIDTH:].reshape(B, L, N_Q_HEADS, HEAD_DIM)
    attn_c = context_attention(q_c, k_c, v_c, sink)
    y_c = jnp.concatenate([fourier_mix(f_c), attn_c], axis=-1) @ w_out
    return y, y_c


def pool_mix(h, w_pool, scale):
    B, N, _ = h.shape
    hf = h.astype(jnp.float32)
    csum = jnp.pad(jnp.cumsum(hf, axis=1), ((0, 0), (1, 0), (0, 0)))
    t = np.arange(N)
    outs = []
    for g, w in enumerate(POOL_WINDOWS):
        lo = np.clip(t - w // 2, 0, N)
        hi = np.clip(t + w // 2, 0, N)
        sl = slice(g * POOL_GROUP, (g + 1) * POOL_GROUP)
        cg = csum[..., sl]
        count = jnp.asarray(hi - lo, dtype=jnp.float32)[None, :, None]
        outs.append((cg[:, hi] - cg[:, lo]) / count - hf[..., sl])
    y = jnp.stack(outs, axis=2).astype(h.dtype)
    y = jnp.einsum('bngc,gcd->bngd', y, w_pool).reshape(B, N, D_MODEL)
    return y * scale


def swiglu(h, w1, w3, w2):
    return (jax.nn.silu(h @ w1) * (h @ w3)) @ w2


def setup_inputs(seed: int = 0) -> dict:
    key = jax.random.key(seed)
    ks = jax.random.split(key, 18)
    f32 = jnp.float32

    def nrm(k, shape, s=1.0):
        return s * jax.random.normal(k, shape, f32)

    return {
        'x': nrm(ks[0], (BATCH, SEQ, D_MODEL)),
        'c': nrm(ks[1], (BATCH, D_MODEL)),
        'ctx': nrm(ks[2], (BATCH, CTX_LEN, D_MODEL)),
        'c_ctx': nrm(ks[3], (D_MODEL,)),
        'ada_w': nrm(ks[4], (DEPTH, D_MODEL, 6 * D_MODEL), 0.5 * D_MODEL ** -0.5),
        'ada_b': nrm(ks[5], (DEPTH, 6 * D_MODEL), 0.02),
        'norm_mix_g': 1.0 + nrm(ks[6], (DEPTH, D_MODEL), 0.05),
        'norm_ffn_g': 1.0 + nrm(ks[7], (DEPTH, D_MODEL), 0.05),
        'mix_in_w': nrm(ks[8], (N_EVEN, D_MODEL, IN_WIDTH), D_MODEL ** -0.5),
        'mix_out_w': nrm(ks[9], (N_EVEN, MIX_WIDTH, D_MODEL), MIX_WIDTH ** -0.5),
        'attn_sink': nrm(ks[10], (N_EVEN, N_Q_HEADS), 0.5),
        'pool_w': nrm(ks[11], (N_ODD, len(POOL_WINDOWS), POOL_GROUP, POOL_GROUP), POOL_GROUP ** -0.5),
        'pool_scale': 1.0 + nrm(ks[12], (N_ODD, D_MODEL), 0.1),
        'ffn_w1': nrm(ks[13], (DEPTH, D_MODEL, FFN_HIDDEN), D_MODEL ** -0.5),
        'ffn_w3': nrm(ks[14], (DEPTH, D_MODEL, FFN_HIDDEN), D_MODEL ** -0.5),
        'ffn_w2': nrm(ks[15], (DEPTH, FFN_HIDDEN, D_MODEL), FFN_HIDDEN ** -0.5),
        'final_g': 1.0 + nrm(ks[16], (D_MODEL,), 0.05),
    }


def reference(x, c, ctx, c_ctx, ada_w, ada_b, norm_mix_g, norm_ffn_g, mix_in_w, mix_out_w,
              attn_sink, pool_w, pool_scale, ffn_w1, ffn_w3, ffn_w2, final_g):
    n_tokens = x.shape[1]
    cos, sin = axial_rope_tables(n_tokens)
    last_ctx_reader = max(range(0, DEPTH, 2))
    h, hc = x, ctx
    for layer in range(DEPTH):
        sh1, sc1, g1, sh2, sc2, g2 = ada_mod(c, ada_w[layer], ada_b[layer])
        update_ctx = layer < last_ctx_reader
        need_ctx_in = update_ctx or (layer % 2 == 0 and layer <= last_ctx_reader)
        if need_ctx_in:
            csh1, csc1, cg1, csh2, csc2, cg2 = ada_mod(c_ctx, ada_w[layer], ada_b[layer])
            xcn = modulate(rms_norm(hc, norm_mix_g[layer]), csh1, csc1)
        xn = modulate(rms_norm(h, norm_mix_g[layer]), sh1, sc1)
        j = layer // 2
        if layer % 2 == 0:
            y, y_c = even_mixer(xn, xcn, mix_in_w[j], mix_out_w[j], attn_sink[j], cos, sin, update_ctx)
        else:
            y = pool_mix(xn, pool_w[j], pool_scale[j])
            y_c = pool_mix(xcn, pool_w[j], pool_scale[j]) if update_ctx else None
        h = h + g1 * y
        hn = modulate(rms_norm(h, norm_ffn_g[layer]), sh2, sc2)
        h = h + g2 * swiglu(hn, ffn_w1[layer], ffn_w3[layer], ffn_w2[layer])
        if update_ctx:
            hc = hc + cg1 * y_c
            hcn = modulate(rms_norm(hc, norm_ffn_g[layer]), csh2, csc2)
            hc = hc + cg2 * swiglu(hcn, ffn_w1[layer], ffn_w3[layer], ffn_w2[layer])
    return rms_norm(h, final_g)
```

```python
import functools

import numpy as np
import jax
import jax.numpy as jnp
from jax import lax
from jax.experimental import pallas as pl
from jax.experimental.pallas import tpu as pltpu

F32 = jnp.float32
BF16 = jnp.bfloat16

D_MODEL = 1024
DEPTH = 4
GRID_W = 64
EPS = 1e-6
FOURIER_HEADS = 4
FOURIER_HEAD_DIM = 128
FOURIER_WIDTH = 512
HEAD_DIM = 64
N_Q_HEADS = 8
N_KV_HEADS = 2
GQA_GROUP = 4
ATTN_WIDTH = 512
KV_WIDTH = 128
Q_END = 1024
IN_WIDTH = 1280
BLOCK = 128
ROPE_THETA = 10000.0
POOL_WINDOWS = (2, 4, 8, 16)
POOL_GROUP = 256
POOL_HALO = 8
FFN_HIDDEN = 2816
FFN_CHUNK = 256
N_FFN_CHUNKS = FFN_HIDDEN // FFN_CHUNK
FFT_RADIX = 64
LANES = 128
ADA_ROWS = 16
VMEM_LIMIT = 56 * 1024 * 1024


def _params(*sem):
    return pltpu.CompilerParams(dimension_semantics=sem, vmem_limit_bytes=VMEM_LIMIT)


def _rms(x, g):
    ms = jnp.mean(x * x, axis=-1, keepdims=True)
    return x * lax.rsqrt(ms + EPS) * g


def _rms_mod(x, g, shift, scale):
    return _rms(x, g) * (1.0 + scale) + shift


def _dot(a, b):
    return jnp.dot(a, b, preferred_element_type=F32)


def _ada_kernel(c_ref, w_ref, b_ref, o_ref):
    a = jax.nn.silu(c_ref[...])
    o_ref[0] = jnp.dot(a, w_ref[0], precision=lax.Precision.HIGHEST,
                       preferred_element_type=F32) + b_ref[0]


def _ada(cond, ada_w, ada_b):
    tn = 1536
    n_out = ada_w.shape[-1]
    return pl.pallas_call(
        _ada_kernel,
        grid=(DEPTH, n_out // tn),
        in_specs=[
            pl.BlockSpec((ADA_ROWS, D_MODEL), lambda l, n: (0, 0)),
            pl.BlockSpec((1, D_MODEL, tn), lambda l, n: (l, 0, n)),
            pl.BlockSpec((1, 1, tn), lambda l, n: (l, 0, n)),
        ],
        out_specs=pl.BlockSpec((1, ADA_ROWS, tn), lambda l, n: (l, 0, n)),
        out_shape=jax.ShapeDtypeStruct((DEPTH, ADA_ROWS, n_out), F32),
        compiler_params=_params("arbitrary", "arbitrary"),
        name="ada_mod",
    )(cond, ada_w, ada_b.reshape(DEPTH, 1, n_out))


def _in_kernel(h_ref, g_ref, sh_ref, sc_ref, w_ref, cos_ref, sin_ref, wc_ref,
               zr_ref, zi_ref, q_ref, k_ref, v_ref, *, tm):
    xn = _rms_mod(h_ref[0], g_ref[...], sh_ref[0], sc_ref[0])
    proj = _dot(xn.astype(BF16), w_ref[...])
    cosf = cos_ref[...]
    sinf = sin_ref[...]
    lane = lax.broadcasted_iota(jnp.int32, (tm, LANES), 1)
    even = (lane & 1) == 0
    low = lane < HEAD_DIM

    def rope(t):
        partner = jnp.where(even, pltpu.roll(t, LANES - 1, 1), pltpu.roll(t, 1, 1))
        return t * cosf + partner * sinf

    wc = wc_ref[...]
    for hh in range(FOURIER_HEADS):
        sl = slice(hh * LANES, (hh + 1) * LANES)
        z = _dot(proj[:, sl].astype(BF16), wc)
        zr_ref[0, :, sl] = z[:, :LANES].astype(BF16)
        zi_ref[0, :, sl] = z[:, LANES:].astype(BF16)
    for p in range(ATTN_WIDTH // LANES):
        sl = slice(p * LANES, (p + 1) * LANES)
        qq = proj[:, FOURIER_WIDTH + p * LANES:FOURIER_WIDTH + (p + 1) * LANES]
        q_ref[0, :, sl] = (rope(qq) * (HEAD_DIM ** -0.5)).astype(BF16)
    kk = rope(proj[:, Q_END:Q_END + KV_WIDTH])
    vv = proj[:, Q_END + KV_WIDTH:]
    k_sw = pltpu.roll(kk, HEAD_DIM, 1)
    v_sw = pltpu.roll(vv, HEAD_DIM, 1)
    k_ref[0, 0] = jnp.where(low, kk, k_sw).astype(BF16)
    k_ref[0, 1] = jnp.where(low, k_sw, kk).astype(BF16)
    v_ref[0, 0] = jnp.where(low, vv, v_sw).astype(BF16)
    v_ref[0, 1] = jnp.where(low, v_sw, vv).astype(BF16)


def _in_proj(h, g, shift, scale, w_in, cosf, sinf, wc, tm):
    B, S, _ = h.shape
    vec = pl.BlockSpec((1, 1, D_MODEL), lambda b, i: (b, 0, 0))
    tok = lambda w: pl.BlockSpec((1, tm, w), lambda b, i: (b, i, 0))
    kv = pl.BlockSpec((1, N_KV_HEADS, tm, LANES), lambda b, i: (b, 0, i, 0))
    return pl.pallas_call(
        functools.partial(_in_kernel, tm=tm),
        grid=(B, S // tm),
        in_specs=[
            tok(D_MODEL),
            pl.BlockSpec((1, D_MODEL), lambda b, i: (0, 0)),
            vec, vec,
            pl.BlockSpec((D_MODEL, IN_WIDTH), lambda b, i: (0, 0)),
            pl.BlockSpec((tm, LANES), lambda b, i: (i, 0)),
            pl.BlockSpec((tm, LANES), lambda b, i: (i, 0)),
            pl.BlockSpec((FOURIER_HEAD_DIM, 2 * FOURIER_HEAD_DIM), lambda b, i: (0, 0)),
        ],
        out_specs=[tok(FOURIER_WIDTH), tok(FOURIER_WIDTH), tok(ATTN_WIDTH), kv, kv],
        out_shape=[
            jax.ShapeDtypeStruct((B, S, FOURIER_WIDTH), BF16),
            jax.ShapeDtypeStruct((B, S, FOURIER_WIDTH), BF16),
            jax.ShapeDtypeStruct((B, S, ATTN_WIDTH), BF16),
            jax.ShapeDtypeStruct((B, N_KV_HEADS, S, LANES), BF16),
            jax.ShapeDtypeStruct((B, N_KV_HEADS, S, LANES), BF16),
        ],
        compiler_params=_params("arbitrary", "arbitrary"),
        name="in_proj",
    )(h, g, shift, scale, w_in, cosf, sinf, wc)


def _fft1_kernel(zr_ref, zi_ref, w_ref, twc_ref, tws_ref, y_ref, *, t):
    x = jnp.concatenate([zr_ref[0], zi_ref[0]], axis=0)
    y = _dot(w_ref[...], x)
    r = FFT_RADIX
    for j in range(t):
        tc = twc_ref[j]
        ts = tws_ref[j]
        for hh in range(FOURIER_HEADS):
            sl = slice(j * FOURIER_WIDTH + hh * LANES, j * FOURIER_WIDTH + (hh + 1) * LANES)
            yr = y[:r, sl]
            yi = y[r:, sl]
            y_ref[0, 0, :, sl] = (yr * tc + yi * ts).astype(BF16)
            y_ref[0, 1, :, sl] = (yi * tc - yr * ts).astype(BF16)


def _fft2_kernel(y_ref, w_ref, o_ref, *, t):
    w = w_ref[...]
    for j in range(t):
        x = jnp.concatenate([y_ref[0, 0, j], y_ref[0, 1, j]], axis=0)
        o_ref[0, :, j * FOURIER_WIDTH:(j + 1) * FOURIER_WIDTH] = _dot(w, x).astype(BF16)


def _fourier_4096(zr, zi, w1, w2, twc, tws):
    B, S, W = zr.shape
    r = FFT_RADIX
    t = 8
    zr2 = zr.reshape(B, r, r * W)
    zi2 = zi.reshape(B, r, r * W)
    y = pl.pallas_call(
        functools.partial(_fft1_kernel, t=t),
        grid=(B, r // t),
        in_specs=[
            pl.BlockSpec((1, r, t * W), lambda b, n: (b, 0, n)),
            pl.BlockSpec((1, r, t * W), lambda b, n: (b, 0, n)),
            pl.BlockSpec((2 * r, 2 * r), lambda b, n: (0, 0)),
            pl.BlockSpec((t, r, LANES), lambda b, n: (n, 0, 0)),
            pl.BlockSpec((t, r, LANES), lambda b, n: (n, 0, 0)),
        ],
        out_specs=pl.BlockSpec((1, 2, r, t * W), lambda b, n: (b, 0, 0, n)),
        out_shape=jax.ShapeDtypeStruct((B, 2, r, r * W), BF16),
        compiler_params=_params("arbitrary", "arbitrary"),
        name="fft_stage1",
    )(zr2, zi2, w1, twc, tws)
    y5 = y.reshape(B, 2, r, r, W)
    out = pl.pallas_call(
        functools.partial(_fft2_kernel, t=t),
        grid=(B, r // t),
        in_specs=[
            pl.BlockSpec((1, 2, t, r, W), lambda b, k: (b, 0, k, 0, 0)),
            pl.BlockSpec((r, 2 * r), lambda b, k: (0, 0)),
        ],
        out_specs=pl.BlockSpec((1, r, t * W), lambda b, k: (b, 0, k)),
        out_shape=jax.ShapeDtypeStruct((B, r, r * W), BF16),
        compiler_params=_params("arbitrary", "arbitrary"),
        name="fft_stage2",
    )(y5, w2)
    return out.reshape(B, S, W)


def _dft_small_kernel(zr_ref, zi_ref, w_ref, o_ref, *, scale):
    x = jnp.concatenate([zr_ref[0], zi_ref[0]], axis=0)
    o_ref[0] = (_dot(w_ref[...], x) * scale).astype(BF16)


def _fourier_small(zr, zi, w):
    B, L, W = zr.shape
    scale = float((L * FOURIER_HEAD_DIM) ** -0.5)
    blk = pl.BlockSpec((1, L, W), lambda b: (b, 0, 0))
    return pl.pallas_call(
        functools.partial(_dft_small_kernel, scale=scale),
        grid=(B,),
        in_specs=[blk, blk, pl.BlockSpec((L, 2 * L), lambda b: (0, 0))],
        out_specs=blk,
        out_shape=jax.ShapeDtypeStruct((B, L, W), BF16),
        compiler_params=_params("arbitrary"),
        name="dft_ctx",
    )(zr, zi, w)


def _att_kernel(sink_ref, q_ref, *refs, nb, window):
    if window:
        kp_ref, kc_ref, kn_ref, vp_ref, vc_ref, vn_ref, kx_ref, vx_ref, o_ref = refs
    else:
        kx_ref, vx_ref, o_ref = refs
    i = pl.program_id(1)
    q = q_ref[0]
    lane = lax.broadcasted_iota(jnp.int32, (BLOCK, LANES), 1)
    low = lane < HEAD_DIM
    if window:
        row = lax.broadcasted_iota(jnp.int32, (BLOCK, BLOCK), 0)
        col = lax.broadcasted_iota(jnp.int32, (BLOCK, BLOCK), 1)
        ninf = jnp.float32(-jnp.inf)
        b_prev = jnp.where((col >= row) & (i > 0), 0.0, ninf)
        b_next = jnp.where((col <= row) & (i < nb - 1), 0.0, ninf)
        n_ctx = kx_ref.shape[2]
        bias = jnp.concatenate(
            [b_prev, jnp.zeros((BLOCK, BLOCK), F32), b_next, jnp.zeros((BLOCK, n_ctx), F32)], axis=1)
    for kh in range(N_KV_HEADS):
        if window:
            k = jnp.concatenate([kp_ref[0, kh], kc_ref[0, kh], kn_ref[0, kh], kx_ref[0, kh]], axis=0)
            v = jnp.concatenate([vp_ref[0, kh], vc_ref[0, kh], vn_ref[0, kh], vx_ref[0, kh]], axis=0)
        else:
            k = kx_ref[0, kh]
            v = vx_ref[0, kh]
        qs = []
        for g in range(GQA_GROUP):
            h = kh * GQA_GROUP + g
            qp = q[:, (h // 2) * LANES:(h // 2 + 1) * LANES]
            keep = low if h % 2 == 0 else jnp.logical_not(low)
            qs.append(jnp.where(keep, qp, jnp.zeros_like(qp)))
        qst = jnp.concatenate(qs, axis=0)
        s = lax.dot_general(qst, k, (((1,), (1,)), ((), ())), preferred_element_type=F32)
        ps, ls = [], []
        for g in range(GQA_GROUP):
            sk = sink_ref[kh * GQA_GROUP + g]
            sg = s[g * BLOCK:(g + 1) * BLOCK]
            if window:
                sg = sg + bias
            m = jnp.maximum(jnp.max(sg, axis=-1, keepdims=True), sk)
            p = jnp.exp(sg - m)
            ls.append(jnp.sum(p, axis=-1, keepdims=True) + jnp.exp(sk - m))
            ps.append(p.astype(BF16))
        o = _dot(jnp.concatenate(ps, axis=0), v)
        og = [o[g * BLOCK:(g + 1) * BLOCK] / ls[g] for g in range(GQA_GROUP)]
        for pr in range(GQA_GROUP // 2):
            c0 = kh * GQA_GROUP * HEAD_DIM + pr * LANES
            o_ref[0, :, c0:c0 + LANES] = jnp.where(low, og[2 * pr], og[2 * pr + 1]).astype(BF16)


def _attention(q, k, v, kx, vx, sink, window):
    B, S, _ = q.shape
    nb = S // BLOCK
    L = kx.shape[2]
    qspec = pl.BlockSpec((1, BLOCK, ATTN_WIDTH), lambda b, i: (b, i, 0))
    xspec = pl.BlockSpec((1, N_KV_HEADS, L, LANES), lambda b, i: (b, 0, 0, 0))
    smem = pl.BlockSpec(memory_space=pltpu.SMEM)
    if window:
        blk = (1, N_KV_HEADS, BLOCK, LANES)
        prev = pl.BlockSpec(blk, lambda b, i: (b, 0, jnp.maximum(i - 1, 0), 0))
        cur = pl.BlockSpec(blk, lambda b, i: (b, 0, i, 0))
        nxt = pl.BlockSpec(blk, lambda b, i: (b, 0, jnp.minimum(i + 1, nb - 1), 0))
        in_specs = [smem, qspec, prev, cur, nxt, prev, cur, nxt, xspec, xspec]
        args = (sink, q, k, k, k, v, v, v, kx, vx)
    else:
        in_specs = [smem, qspec, xspec, xspec]
        args = (sink, q, kx, vx)
    return pl.pallas_call(
        functools.partial(_att_kernel, nb=nb, window=window),
        grid=(B, nb),
        in_specs=in_specs,
        out_specs=qspec,
        out_shape=jax.ShapeDtypeStruct((B, S, ATTN_WIDTH), BF16),
        compiler_params=_params("arbitrary", "arbitrary"),
        name="window_attn" if window else "ctx_attn",
    )(*args)


def _pool_kernel(hm_ref, hp_ref, hn_ref, g_ref, sh_ref, sc_ref, g1_ref, pw_ref, ps_ref,
                 o_ref, xs_ref, *, tm, seq):
    i = pl.program_id(1)
    nt = seq // tm
    g = g_ref[...]
    sh = sh_ref[0]
    sc = sc_ref[0]
    xm = hm_ref[0]
    hal = POOL_HALO
    xs_ref[0:hal] = jnp.where(i > 0, _rms_mod(hp_ref[0], g, sh, sc), 0.0)
    xs_ref[hal:hal + tm] = _rms_mod(xm, g, sh, sc)
    xs_ref[hal + tm:2 * hal + tm] = jnp.where(i < nt - 1, _rms_mod(hn_ref[0], g, sh, sc), 0.0)
    pos = i * tm + lax.broadcasted_iota(jnp.int32, (tm, 1), 0)
    ys = []
    for gi, w in enumerate(POOL_WINDOWS):
        half = w // 2
        cs = slice(gi * POOL_GROUP, (gi + 1) * POOL_GROUP)
        acc = xs_ref[hal - half:hal - half + tm, cs]
        for d in range(-half + 1, half):
            acc = acc + xs_ref[hal + d:hal + d + tm, cs]
        cnt = (jnp.minimum(pos + half, seq) - jnp.maximum(pos - half, 0)).astype(F32)
        yg = acc / cnt - xs_ref[hal:hal + tm, cs]
        ys.append(_dot(yg.astype(BF16), pw_ref[gi]))
    y = jnp.concatenate(ys, axis=1) * ps_ref[...]
    o_ref[0] = xm + g1_ref[0] * y


def _pool_mixer(h, g, shift, scale, gate, pool_w, pool_scale, tm):
    B, S, _ = h.shape
    hb = tm // POOL_HALO
    vec = pl.BlockSpec((1, 1, D_MODEL), lambda b, i: (b, 0, 0))
    row = pl.BlockSpec((1, D_MODEL), lambda b, i: (0, 0))
    tok = pl.BlockSpec((1, tm, D_MODEL), lambda b, i: (b, i, 0))
    halo = (1, POOL_HALO, D_MODEL)
    return pl.pallas_call(
        functools.partial(_pool_kernel, tm=tm, seq=S),
        grid=(B, S // tm),
        in_specs=[
            tok,
            pl.BlockSpec(halo, lambda b, i: (b, jnp.maximum(i * hb - 1, 0), 0)),
            pl.BlockSpec(halo, lambda b, i: (b, jnp.minimum((i + 1) * hb, S // POOL_HALO - 1), 0)),
            row, vec, vec, vec,
            pl.BlockSpec((len(POOL_WINDOWS), POOL_GROUP, POOL_GROUP), lambda b, i: (0, 0, 0)),
            row,
        ],
        out_specs=tok,
        out_shape=jax.ShapeDtypeStruct((B, S, D_MODEL), F32),
        scratch_shapes=[pltpu.VMEM((tm + 2 * POOL_HALO, D_MODEL), F32)],
        compiler_params=_params("arbitrary", "arbitrary"),
        name="pool_mixer",
    )(h, h, h, g, shift, scale, gate, pool_w, pool_scale)


def _ffn_kernel(*refs, tm, proj, final):
    refs = list(refs)
    h_ref = refs.pop(0)
    if proj:
        four_ref, attn_ref, wo_ref, g1_ref = refs[:4]
        refs = refs[4:]
    gn_ref, sh_ref, sc_ref, g2_ref, w1_ref, w3_ref, w2_ref = refs[:7]
    refs = refs[7:]
    if final:
        fg_ref = refs.pop(0)
    o_ref, acc_ref = refs
    h = h_ref[0]
    if proj:
        y = (_dot(four_ref[0], wo_ref[:FOURIER_WIDTH]) + _dot(attn_ref[0], wo_ref[FOURIER_WIDTH:]))
        h = h + g1_ref[0] * y
    hn = _rms_mod(h, gn_ref[...], sh_ref[0], sc_ref[0]).astype(BF16)
    acc_ref[...] = jnp.zeros_like(acc_ref)

    def body(c, carry):
        a = _dot(hn, w1_ref[c])
        b = _dot(hn, w3_ref[c])
        t = (jax.nn.silu(a) * b).astype(BF16)
        acc_ref[...] += _dot(t, w2_ref[c])
        return carry

    lax.fori_loop(0, N_FFN_CHUNKS, body, 0)
    out = h + g2_ref[0] * acc_ref[...]
    if final:
        out = _rms(out, fg_ref[...])
    o_ref[0] = out


def _ffn(h, mix, gn, shift, scale, gate, w1, w3, w2, final_g, tm):
    B, S, _ = h.shape
    vec = pl.BlockSpec((1, 1, D_MODEL), lambda b, i: (b, 0, 0))
    row = pl.BlockSpec((1, D_MODEL), lambda b, i: (0, 0))
    tok = lambda w: pl.BlockSpec((1, tm, w), lambda b, i: (b, i, 0))
    whole = pl.BlockSpec(memory_space=pltpu.VMEM)
    in_specs = [tok(D_MODEL)]
    args = [h]
    if mix is not None:
        four, attn, w_out, g1 = mix
        in_specs += [tok(FOURIER_WIDTH), tok(ATTN_WIDTH), whole, vec]
        args += [four, attn, w_out, g1]
    in_specs += [row, vec, vec, vec, whole, whole, whole]
    args += [gn, shift, scale, gate, w1, w3, w2]
    if final_g is not None:
        in_specs.append(row)
        args.append(final_g)
    return pl.pallas_call(
        functools.partial(_ffn_kernel, tm=tm, proj=mix is not None, final=final_g is not None),
        grid=(B, S // tm),
        in_specs=in_specs,
        out_specs=tok(D_MODEL),
        out_shape=jax.ShapeDtypeStruct((B, S, D_MODEL), F32),
        scratch_shapes=[pltpu.VMEM((tm, D_MODEL), F32)],
        compiler_params=_params("arbitrary", "arbitrary"),
        name="ffn",
    )(*args)


def _dft_cos_sin(n):
    idx = np.arange(n, dtype=np.int64)
    ang = 2.0 * np.pi * ((idx[:, None] * idx[None, :]) % n) / n
    return np.cos(ang), np.sin(ang)


def _rope_tables(n_tokens):
    rows = n_tokens // GRID_W
    row = jnp.repeat(jnp.arange(rows, dtype=F32), GRID_W)
    col = jnp.tile(jnp.arange(GRID_W, dtype=F32), rows)
    n_freq = HEAD_DIM // 4
    inv = ROPE_THETA ** (-jnp.arange(n_freq, dtype=F32) / n_freq)
    ang = jnp.concatenate([row[:, None] * inv[None], col[:, None] * inv[None]], axis=-1)
    cos = jnp.repeat(jnp.cos(ang), 2, axis=-1)
    sin = jnp.repeat(jnp.sin(ang), 2, axis=-1)
    sign = jnp.tile(jnp.asarray([-1.0, 1.0], F32), HEAD_DIM // 2)
    return jnp.tile(cos, (1, 2)), jnp.tile(sin * sign, (1, 2))


def _fourier_tables():
    r = FFT_RADIX
    c, s = _dft_cos_sin(r)
    w1 = np.block([[c, s], [-s, c]])
    w2 = np.concatenate([c, s], axis=1)
    n_pos = r * r
    idx = np.arange(r, dtype=np.int64)
    ang = 2.0 * np.pi * (idx[:, None] * idx[None, :]) / n_pos
    scale = (n_pos * FOURIER_HEAD_DIM) ** -0.5
    twc = np.broadcast_to((np.cos(ang) * scale)[:, :, None], (r, r, LANES))
    tws = np.broadcast_to((np.sin(ang) * scale)[:, :, None], (r, r, LANES))
    cc, sc = _dft_cos_sin(FOURIER_HEAD_DIM)
    wc = np.concatenate([cc, -sc], axis=1)
    f32 = lambda a: jnp.asarray(np.ascontiguousarray(a), F32)
    return f32(w1).astype(BF16), f32(w2).astype(BF16), f32(twc), f32(tws), f32(wc).astype(BF16)


def kernel(x, c, ctx, c_ctx, ada_w, ada_b, norm_mix_g, norm_ffn_g, mix_in_w, mix_out_w, attn_sink,
           pool_w, pool_scale, ffn_w1, ffn_w3, ffn_w2, final_g):
    B, S, _ = x.shape
    L = ctx.shape[1]
    tm = 512
    tm_ctx = L

    cond = jnp.zeros((ADA_ROWS, D_MODEL), F32).at[:B].set(c).at[B].set(c_ctx)
    mods = _ada(cond, ada_w, ada_b)

    cosf, sinf = _rope_tables(S)
    cos_id = jnp.ones((L, LANES), F32)
    sin_id = jnp.zeros((L, LANES), F32)
    w1_dft, w2_dft, twc, tws, wc = _fourier_tables()
    cl, sl_ = _dft_cos_sin(L)
    w_ctx_dft = jnp.asarray(np.concatenate([cl, sl_], axis=1), F32).astype(BF16)

    last_ctx_reader = max(range(0, DEPTH, 2))
    h, hc = x, ctx
    for layer in range(DEPTH):
        m = mods[layer]
        lat = [m[:B, k * D_MODEL:(k + 1) * D_MODEL].reshape(B, 1, D_MODEL) for k in range(6)]
        cm = [jnp.broadcast_to(m[B, k * D_MODEL:(k + 1) * D_MODEL].reshape(1, 1, D_MODEL), (B, 1, D_MODEL))
              for k in range(6)]
        update_ctx = layer < last_ctx_reader
        need_ctx_in = update_ctx or (layer % 2 == 0 and layer <= last_ctx_reader)
        g_mix = norm_mix_g[layer].reshape(1, D_MODEL)
        g_ffn = norm_ffn_g[layer].reshape(1, D_MODEL)
        nc = N_FFN_CHUNKS
        w1 = ffn_w1[layer].astype(BF16).reshape(D_MODEL, nc, FFN_CHUNK).transpose(1, 0, 2)
        w3 = ffn_w3[layer].astype(BF16).reshape(D_MODEL, nc, FFN_CHUNK).transpose(1, 0, 2)
        w2 = ffn_w2[layer].astype(BF16).reshape(nc, FFN_CHUNK, D_MODEL)
        fin = final_g.reshape(1, D_MODEL) if layer == DEPTH - 1 else None
        j = layer // 2
        if layer % 2 == 0:
            w_in = mix_in_w[j].astype(BF16)
            w_out = mix_out_w[j].astype(BF16)
            sink = attn_sink[j]
            zr_c, zi_c, q_c, k_c, v_c = _in_proj(hc, g_mix, cm[0], cm[1], w_in, cos_id, sin_id, wc, tm_ctx)
            zr, zi, q, k, v = _in_proj(h, g_mix, lat[0], lat[1], w_in, cosf, sinf, wc, tm)
            four = _fourier_4096(zr, zi, w1_dft, w2_dft, twc, tws)
            attn = _attention(q, k, v, k_c, v_c, sink, True)
            h = _ffn(h, (four, attn, w_out, lat[2]), g_ffn, lat[3], lat[4], lat[5], w1, w3, w2, fin, tm)
            if update_ctx:
                four_c = _fourier_small(zr_c, zi_c, w_ctx_dft)
                attn_c = _attention(q_c, None, None, k_c, v_c, sink, False)
                hc = _ffn(hc, (four_c, attn_c, w_out, cm[2]), g_ffn, cm[3], cm[4], cm[5], w1, w3, w2,
                          None, tm_ctx)
        else:
            pw = pool_w[j].astype(BF16)
            psc = pool_scale[j].reshape(1, D_MODEL)
            h = _pool_mixer(h, g_mix, lat[0], lat[1], lat[2], pw, psc, tm)
            h = _ffn(h, None, g_ffn, lat[3], lat[4], lat[5], w1, w3, w2, fin, tm)
            if update_ctx:
                hc = _pool_mixer(hc, g_mix, cm[0], cm[1], cm[2], pw, psc, tm_ctx)
                hc = _ffn(hc, None, g_ffn, cm[3], cm[4], cm[5], w1, w3, w2, None, tm_ctx)
    return h
```

```python
import functools

import numpy as np
import jax
import jax.numpy as jnp
from jax import lax
from jax.experimental import pallas as pl
from jax.experimental.pallas import tpu as pltpu

F32 = jnp.float32
BF16 = jnp.bfloat16

D_MODEL = 1024
DEPTH = 4
GRID_W = 64
EPS = 1e-6
FOURIER_HEADS = 4
FOURIER_HEAD_DIM = 128
FOURIER_WIDTH = 512
HEAD_DIM = 64
N_Q_HEADS = 8
N_KV_HEADS = 2
GQA_GROUP = 4
ATTN_WIDTH = 512
KV_WIDTH = 128
Q_END = 1024
IN_WIDTH = 1280
BLOCK = 128
ATT_QBLOCKS = 4
ROPE_THETA = 10000.0
POOL_WINDOWS = (2, 4, 8, 16)
POOL_GROUP = 256
POOL_HALO = 8
POOL_PAD = 2 * POOL_HALO
FFN_HIDDEN = 2816
FFN_CHUNKS = ((0, 1024), (1024, 1024), (2048, 768))
FFT_RADIX = 64
LANES = 128
LOG2E = 1.4426950408889634
ADA_ROWS = 16
VMEM_LIMIT = 56 * 1024 * 1024


def _params(*sem):
    return pltpu.CompilerParams(dimension_semantics=sem, vmem_limit_bytes=VMEM_LIMIT)


def _rms(x, g):
    ms = jnp.mean(x * x, axis=-1, keepdims=True)
    return x * lax.rsqrt(ms + EPS) * g


def _rms_mod(x, g, shift, scale):
    return _rms(x, g) * (1.0 + scale) + shift


def _dot(a, b):
    return jnp.dot(a, b, preferred_element_type=F32)


def _ada_kernel(c_ref, w_ref, b_ref, o_ref):
    a = jax.nn.silu(c_ref[...])
    o_ref[0] = jnp.dot(a, w_ref[0], precision=lax.Precision.HIGHEST,
                       preferred_element_type=F32) + b_ref[0]


def _ada(cond, ada_w, ada_b):
    tn = 1536
    n_out = ada_w.shape[-1]
    return pl.pallas_call(
        _ada_kernel,
        grid=(DEPTH, n_out // tn),
        in_specs=[
            pl.BlockSpec((ADA_ROWS, D_MODEL), lambda l, n: (0, 0)),
            pl.BlockSpec((1, D_MODEL, tn), lambda l, n: (l, 0, n)),
            pl.BlockSpec((1, 1, tn), lambda l, n: (l, 0, n)),
        ],
        out_specs=pl.BlockSpec((1, ADA_ROWS, tn), lambda l, n: (l, 0, n)),
        out_shape=jax.ShapeDtypeStruct((DEPTH, ADA_ROWS, n_out), F32),
        compiler_params=_params("arbitrary", "arbitrary"),
        name="ada_mod",
    )(cond, ada_w, ada_b.reshape(DEPTH, 1, n_out))


def _in_kernel(h_ref, g_ref, sh_ref, sc_ref, w_ref, cos_ref, sin_ref, wc_ref,
               zr_ref, zi_ref, q_ref, k_ref, v_ref, *, tm):
    xn = _rms_mod(h_ref[0], g_ref[...], sh_ref[0], sc_ref[0])
    proj = _dot(xn.astype(BF16), w_ref[...])
    cosf = cos_ref[...]
    sinf = sin_ref[...]
    lane = lax.broadcasted_iota(jnp.int32, (tm, LANES), 1)
    even = (lane & 1) == 0
    low = lane < HEAD_DIM

    def rope(t):
        partner = jnp.where(even, pltpu.roll(t, LANES - 1, 1), pltpu.roll(t, 1, 1))
        return t * cosf + partner * sinf

    wc = wc_ref[...]
    for hh in range(FOURIER_HEADS):
        sl = slice(hh * LANES, (hh + 1) * LANES)
        z = _dot(proj[:, sl].astype(BF16), wc)
        zr_ref[0, :, sl] = z[:, :LANES].astype(BF16)
        zi_ref[0, :, sl] = z[:, LANES:].astype(BF16)
    for p in range(ATTN_WIDTH // LANES):
        sl = slice(p * LANES, (p + 1) * LANES)
        qq = proj[:, FOURIER_WIDTH + p * LANES:FOURIER_WIDTH + (p + 1) * LANES]
        q_ref[0, :, sl] = (rope(qq) * (HEAD_DIM ** -0.5 * LOG2E)).astype(BF16)
    kk = rope(proj[:, Q_END:Q_END + KV_WIDTH])
    vv = proj[:, Q_END + KV_WIDTH:]
    k_sw = pltpu.roll(kk, HEAD_DIM, 1)
    v_sw = pltpu.roll(vv, HEAD_DIM, 1)
    k_ref[0, 0] = jnp.where(low, kk, k_sw).astype(BF16)
    k_ref[0, 1] = jnp.where(low, k_sw, kk).astype(BF16)
    v_ref[0, 0] = jnp.where(low, vv, v_sw).astype(BF16)
    v_ref[0, 1] = jnp.where(low, v_sw, vv).astype(BF16)


def _in_proj(h, g, shift, scale, w_in, cosf, sinf, wc, tm):
    B, S, _ = h.shape
    vec = pl.BlockSpec((1, 1, D_MODEL), lambda b, i: (b, 0, 0))
    tok = lambda w: pl.BlockSpec((1, tm, w), lambda b, i: (b, i, 0))
    kv = pl.BlockSpec((1, N_KV_HEADS, tm, LANES), lambda b, i: (b, 0, i, 0))
    return pl.pallas_call(
        functools.partial(_in_kernel, tm=tm),
        grid=(B, S // tm),
        in_specs=[
            tok(D_MODEL),
            pl.BlockSpec((1, D_MODEL), lambda b, i: (0, 0)),
            vec, vec,
            pl.BlockSpec((D_MODEL, IN_WIDTH), lambda b, i: (0, 0)),
            pl.BlockSpec((tm, LANES), lambda b, i: (i, 0)),
            pl.BlockSpec((tm, LANES), lambda b, i: (i, 0)),
            pl.BlockSpec((FOURIER_HEAD_DIM, 2 * FOURIER_HEAD_DIM), lambda b, i: (0, 0)),
        ],
        out_specs=[tok(FOURIER_WIDTH), tok(FOURIER_WIDTH), tok(ATTN_WIDTH), kv, kv],
        out_shape=[
            jax.ShapeDtypeStruct((B, S, FOURIER_WIDTH), BF16),
            jax.ShapeDtypeStruct((B, S, FOURIER_WIDTH), BF16),
            jax.ShapeDtypeStruct((B, S, ATTN_WIDTH), BF16),
            jax.ShapeDtypeStruct((B, N_KV_HEADS, S, LANES), BF16),
            jax.ShapeDtypeStruct((B, N_KV_HEADS, S, LANES), BF16),
        ],
        compiler_params=_params("arbitrary", "arbitrary"),
        name="in_proj",
    )(h, g, shift, scale, w_in, cosf, sinf, wc)


def _fft1_kernel(zr_ref, zi_ref, w_ref, twc_ref, tws_ref, y_ref, *, t):
    x = jnp.concatenate([zr_ref[0], zi_ref[0]], axis=0)
    y = _dot(w_ref[...], x)
    r = FFT_RADIX
    for j in range(t):
        tc = twc_ref[j]
        ts = tws_ref[j]
        for hh in range(FOURIER_HEADS):
            sl = slice(j * FOURIER_WIDTH + hh * LANES, j * FOURIER_WIDTH + (hh + 1) * LANES)
            yr = y[:r, sl]
            yi = y[r:, sl]
            y_ref[0, 0, :, sl] = (yr * tc + yi * ts).astype(BF16)
            y_ref[0, 1, :, sl] = (yi * tc - yr * ts).astype(BF16)


def _fft2_kernel(y_ref, w_ref, o_ref, *, t):
    w = w_ref[...]
    for j in range(t):
        x = jnp.concatenate([y_ref[0, 0, j], y_ref[0, 1, j]], axis=0)
        o_ref[0, :, j * FOURIER_WIDTH:(j + 1) * FOURIER_WIDTH] = _dot(w, x).astype(BF16)


def _fourier_4096(zr, zi, w1, w2, twc, tws):
    B, S, W = zr.shape
    r = FFT_RADIX
    t = 8
    zr2 = zr.reshape(B, r, r * W)
    zi2 = zi.reshape(B, r, r * W)
    y = pl.pallas_call(
        functools.partial(_fft1_kernel, t=t),
        grid=(B, r // t),
        in_specs=[
            pl.BlockSpec((1, r, t * W), lambda b, n: (b, 0, n)),
            pl.BlockSpec((1, r, t * W), lambda b, n: (b, 0, n)),
            pl.BlockSpec((2 * r, 2 * r), lambda b, n: (0, 0)),
            pl.BlockSpec((t, r, LANES), lambda b, n: (n, 0, 0)),
            pl.BlockSpec((t, r, LANES), lambda b, n: (n, 0, 0)),
        ],
        out_specs=pl.BlockSpec((1, 2, r, t * W), lambda b, n: (b, 0, 0, n)),
        out_shape=jax.ShapeDtypeStruct((B, 2, r, r * W), BF16),
        compiler_params=_params("arbitrary", "arbitrary"),
        name="fft_stage1",
    )(zr2, zi2, w1, twc, tws)
    y5 = y.reshape(B, 2, r, r, W)
    out = pl.pallas_call(
        functools.partial(_fft2_kernel, t=t),
        grid=(B, r // t),
        in_specs=[
            pl.BlockSpec((1, 2, t, r, W), lambda b, k: (b, 0, k, 0, 0)),
            pl.BlockSpec((r, 2 * r), lambda b, k: (0, 0)),
        ],
        out_specs=pl.BlockSpec((1, r, t * W), lambda b, k: (b, 0, k)),
        out_shape=jax.ShapeDtypeStruct((B, r, r * W), BF16),
        compiler_params=_params("arbitrary", "arbitrary"),
        name="fft_stage2",
    )(y5, w2)
    return out.reshape(B, S, W)


def _dft_small_kernel(zr_ref, zi_ref, w_ref, o_ref, *, scale):
    x = jnp.concatenate([zr_ref[0], zi_ref[0]], axis=0)
    o_ref[0] = (_dot(w_ref[...], x) * scale).astype(BF16)


def _fourier_small(zr, zi, w):
    B, L, W = zr.shape
    scale = float((L * FOURIER_HEAD_DIM) ** -0.5)
    blk = pl.BlockSpec((1, L, W), lambda b: (b, 0, 0))
    return pl.pallas_call(
        functools.partial(_dft_small_kernel, scale=scale),
        grid=(B,),
        in_specs=[blk, blk, pl.BlockSpec((L, 2 * L), lambda b: (0, 0))],
        out_specs=blk,
        out_shape=jax.ShapeDtypeStruct((B, L, W), BF16),
        compiler_params=_params("arbitrary"),
        name="dft_ctx",
    )(zr, zi, w)


def _att_kernel(sink_ref, q_ref, *refs, n_steps, qb, window):
    if window:
        kp_ref, kc_ref, kn_ref, vp_ref, vc_ref, vn_ref, kx_ref, vx_ref, o_ref = refs
    else:
        kx_ref, vx_ref, o_ref = refs
    i = pl.program_id(1)
    lane = lax.broadcasted_iota(jnp.int32, (BLOCK, LANES), 1)
    low = lane < HEAD_DIM
    if window:
        row = lax.broadcasted_iota(jnp.int32, (BLOCK, BLOCK), 0)
        col = lax.broadcasted_iota(jnp.int32, (BLOCK, BLOCK), 1)
        ninf = jnp.float32(-jnp.inf)
        before = col >= row
        after = col <= row

    def window_tiles(p_ref, c_ref, n_ref, kh, a):
        cur = lambda j: c_ref[0, kh, j * BLOCK:(j + 1) * BLOCK]
        first = p_ref[0, kh] if a == 0 else cur(a - 1)
        last = n_ref[0, kh] if a == qb - 1 else cur(a + 1)
        return [first, cur(a), last]

    def scores(a, kh):
        if window:
            k = jnp.concatenate(window_tiles(kp_ref, kc_ref, kn_ref, kh, a) + [kx_ref[0, kh]], axis=0)
        else:
            k = kx_ref[0, kh]
        qs = []
        for g in range(GQA_GROUP):
            h = kh * GQA_GROUP + g
            qp = q_ref[0, a * BLOCK:(a + 1) * BLOCK, (h // 2) * LANES:(h // 2 + 1) * LANES]
            keep = low if h % 2 == 0 else jnp.logical_not(low)
            qs.append(jnp.where(keep, qp, jnp.zeros_like(qp)))
        qst = jnp.concatenate(qs, axis=0)
        return lax.dot_general(qst, k, (((1,), (1,)), ((), ())), preferred_element_type=F32)

    def finish(a, kh, s):
        if window:
            v = jnp.concatenate(window_tiles(vp_ref, vc_ref, vn_ref, kh, a) + [vx_ref[0, kh]], axis=0)
            ok_first = before & (i > 0) if a == 0 else before
            ok_last = after & (i < n_steps - 1) if a == qb - 1 else after
            b_first = jnp.where(ok_first, 0.0, ninf)
            b_last = jnp.where(ok_last, 0.0, ninf)
        else:
            v = vx_ref[0, kh]
        ps, ls = [], []
        for g in range(GQA_GROUP):
            sk = sink_ref[kh * GQA_GROUP + g] * LOG2E
            sg = s[g * BLOCK:(g + 1) * BLOCK]
            if window:
                sg = jnp.concatenate(
                    [sg[:, :BLOCK] + b_first, sg[:, BLOCK:2 * BLOCK],
                     sg[:, 2 * BLOCK:3 * BLOCK] + b_last, sg[:, 3 * BLOCK:]], axis=1)
            m = jnp.maximum(jnp.max(sg, axis=-1, keepdims=True), sk)
            p = jnp.exp2(sg - m)
            ls.append(jnp.sum(p, axis=-1, keepdims=True) + jnp.exp2(sk - m))
            ps.append(p.astype(BF16))
        o = _dot(jnp.concatenate(ps, axis=0), v)
        og = [o[g * BLOCK:(g + 1) * BLOCK] / ls[g] for g in range(GQA_GROUP)]
        for pr in range(GQA_GROUP // 2):
            c0 = kh * GQA_GROUP * HEAD_DIM + pr * LANES
            o_ref[0, a * BLOCK:(a + 1) * BLOCK, c0:c0 + LANES] = jnp.where(
                low, og[2 * pr], og[2 * pr + 1]).astype(BF16)

    units = [(a, kh) for a in range(qb) for kh in range(N_KV_HEADS)]
    ahead = 2
    pending = [scores(*u) for u in units[:ahead]]
    for n, u in enumerate(units):
        s = pending.pop(0)
        if n + ahead < len(units):
            pending.append(scores(*units[n + ahead]))
        finish(*u, s)


def _attention(q, k, v, kx, vx, sink, window):
    B, S, _ = q.shape
    nb = S // BLOCK
    qb = min(ATT_QBLOCKS, nb)
    rows = qb * BLOCK
    n_steps = S // rows
    L = kx.shape[2]
    qspec = pl.BlockSpec((1, rows, ATTN_WIDTH), lambda b, i: (b, i, 0))
    xspec = pl.BlockSpec((1, N_KV_HEADS, L, LANES), lambda b, i: (b, 0, 0, 0))
    smem = pl.BlockSpec(memory_space=pltpu.SMEM)
    if window:
        blk = (1, N_KV_HEADS, BLOCK, LANES)
        prev = pl.BlockSpec(blk, lambda b, i: (b, 0, jnp.maximum(i * qb - 1, 0), 0))
        cur = pl.BlockSpec((1, N_KV_HEADS, rows, LANES), lambda b, i: (b, 0, i, 0))
        nxt = pl.BlockSpec(blk, lambda b, i: (b, 0, jnp.minimum((i + 1) * qb, nb - 1), 0))
        in_specs = [smem, qspec, prev, cur, nxt, prev, cur, nxt, xspec, xspec]
        args = (sink, q, k, k, k, v, v, v, kx, vx)
    else:
        in_specs = [smem, qspec, xspec, xspec]
        args = (sink, q, kx, vx)
    return pl.pallas_call(
        functools.partial(_att_kernel, n_steps=n_steps, qb=qb, window=window),
        grid=(B, n_steps),
        in_specs=in_specs,
        out_specs=qspec,
        out_shape=jax.ShapeDtypeStruct((B, S, ATTN_WIDTH), BF16),
        compiler_params=_params("arbitrary", "arbitrary"),
        name="window_attn" if window else "ctx_attn",
    )(*args)


def _pool_kernel(hm_ref, hp_ref, hn_ref, g_ref, sh_ref, sc_ref, g1_ref, pw_ref, ps_ref, ic_ref,
                 o_ref, xs_ref, p_ref, *, tm, seq):
    i = pl.program_id(1)
    nt = seq // tm
    g = g_ref[...]
    sh = sh_ref[0]
    sc = sc_ref[0]
    xm = hm_ref[0]
    hal = POOL_HALO
    xs_ref[0:hal] = jnp.where(i > 0, _rms_mod(hp_ref[0], g, sh, sc), 0.0)
    xs_ref[hal:hal + tm] = _rms_mod(xm, g, sh, sc)
    xs_ref[hal + tm:2 * hal + tm] = jnp.where(i < nt - 1, _rms_mod(hn_ref[0], g, sh, sc), 0.0)
    xs_ref[2 * hal + tm:] = jnp.zeros((POOL_PAD, D_MODEL), F32)
    ys = []
    for gi, w in enumerate(POOL_WINDOWS):
        half = w // 2
        cs = slice(gi * POOL_GROUP, (gi + 1) * POOL_GROUP)
        if w == 2:
            acc = xs_ref[hal - 1:hal - 1 + tm, cs] + xs_ref[hal:hal + tm, cs]
        else:
            n2, n4, n8 = tm + 3 * hal, tm + 2 * hal, tm + hal
            p_ref[0:n2] = xs_ref[0:n2, cs] + xs_ref[1:n2 + 1, cs]
            if w == 4:
                acc = p_ref[hal - 2:hal - 2 + tm] + p_ref[hal:hal + tm]
            else:
                p_ref[0:n4] = p_ref[0:n4] + p_ref[2:n4 + 2]
                if w == 8:
                    acc = p_ref[hal - 4:hal - 4 + tm] + p_ref[hal:hal + tm]
                else:
                    p_ref[0:n8] = p_ref[0:n8] + p_ref[4:n8 + 4]
                    acc = p_ref[0:tm] + p_ref[hal:hal + tm]
        yg = acc * ic_ref[:, gi:gi + 1] - xs_ref[hal:hal + tm, cs]
        ys.append(_dot(yg.astype(BF16), pw_ref[gi]))
    y = jnp.concatenate(ys, axis=1) * ps_ref[...]
    o_ref[0] = xm + g1_ref[0] * y


def _pool_mixer(h, g, shift, scale, gate, pool_w, pool_scale, tm):
    B, S, _ = h.shape
    hb = tm // POOL_HALO
    vec = pl.BlockSpec((1, 1, D_MODEL), lambda b, i: (b, 0, 0))
    row = pl.BlockSpec((1, D_MODEL), lambda b, i: (0, 0))
    tok = pl.BlockSpec((1, tm, D_MODEL), lambda b, i: (b, i, 0))
    halo = (1, POOL_HALO, D_MODEL)
    t = np.arange(S)
    inv_count = jnp.asarray(np.stack(
        [1.0 / (np.minimum(t + w // 2, S) - np.maximum(t - w // 2, 0)) for w in POOL_WINDOWS], axis=1), F32)
    return pl.pallas_call(
        functools.partial(_pool_kernel, tm=tm, seq=S),
        grid=(B, S // tm),
        in_specs=[
            tok,
            pl.BlockSpec(halo, lambda b, i: (b, jnp.maximum(i * hb - 1, 0), 0)),
            pl.BlockSpec(halo, lambda b, i: (b, jnp.minimum((i + 1) * hb, S // POOL_HALO - 1), 0)),
            row, vec, vec, vec,
            pl.BlockSpec((len(POOL_WINDOWS), POOL_GROUP, POOL_GROUP), lambda b, i: (0, 0, 0)),
            row,
            pl.BlockSpec((tm, len(POOL_WINDOWS)), lambda b, i: (i, 0)),
        ],
        out_specs=tok,
        out_shape=jax.ShapeDtypeStruct((B, S, D_MODEL), F32),
        scratch_shapes=[pltpu.VMEM((tm + 2 * POOL_HALO + POOL_PAD, D_MODEL), F32),
                        pltpu.VMEM((tm + 3 * POOL_HALO, POOL_GROUP), F32)],
        compiler_params=_params("arbitrary", "arbitrary"),
        name="pool_mixer",
    )(h, h, h, g, shift, scale, gate, pool_w, pool_scale, inv_count)


def _ffn_kernel(*refs, tm, proj, final):
    refs = list(refs)
    h_ref = refs.pop(0)
    if proj:
        four_ref, attn_ref, wo_ref, g1_ref = refs[:4]
        refs = refs[4:]
    gn_ref, sh_ref, sc_ref, g2_ref, w1_ref, w3_ref, w2_ref = refs[:7]
    refs = refs[7:]
    if final:
        fg_ref = refs.pop(0)
    (o_ref,) = refs
    h = h_ref[0]
    if proj:
        y = (_dot(four_ref[0], wo_ref[:FOURIER_WIDTH]) + _dot(attn_ref[0], wo_ref[FOURIER_WIDTH:]))
        h = h + g1_ref[0] * y
    hn = _rms_mod(h, gn_ref[...], sh_ref[0], sc_ref[0]).astype(BF16)
    acc = None
    for c0, cn in FFN_CHUNKS:
        a = _dot(hn, w1_ref[:, c0:c0 + cn])
        b = _dot(hn, w3_ref[:, c0:c0 + cn])
        t = (jax.nn.silu(a) * b).astype(BF16)
        d = _dot(t, w2_ref[c0:c0 + cn, :])
        acc = d if acc is None else acc + d
    out = h + g2_ref[0] * acc
    if final:
        out = _rms(out, fg_ref[...])
    o_ref[0] = out


def _ffn(h, mix, gn, shift, scale, gate, w1, w3, w2, final_g, tm):
    B, S, _ = h.shape
    vec = pl.BlockSpec((1, 1, D_MODEL), lambda b, i: (b, 0, 0))
    row = pl.BlockSpec((1, D_MODEL), lambda b, i: (0, 0))
    tok = lambda w: pl.BlockSpec((1, tm, w), lambda b, i: (b, i, 0))
    whole = pl.BlockSpec(memory_space=pltpu.VMEM)
    in_specs = [tok(D_MODEL)]
    args = [h]
    if mix is not None:
        four, attn, w_out, g1 = mix
        in_specs += [tok(FOURIER_WIDTH), tok(ATTN_WIDTH), whole, vec]
        args += [four, attn, w_out, g1]
    in_specs += [row, vec, vec, vec, whole, whole, whole]
    args += [gn, shift, scale, gate, w1, w3, w2]
    if final_g is not None:
        in_specs.append(row)
        args.append(final_g)
    return pl.pallas_call(
        functools.partial(_ffn_kernel, tm=tm, proj=mix is not None, final=final_g is not None),
        grid=(B, S // tm),
        in_specs=in_specs,
        out_specs=tok(D_MODEL),
        out_shape=jax.ShapeDtypeStruct((B, S, D_MODEL), F32),
        compiler_params=_params("arbitrary", "arbitrary"),
        name="ffn",
    )(*args)


def _dft_cos_sin(n):
    idx = np.arange(n, dtype=np.int64)
    ang = 2.0 * np.pi * ((idx[:, None] * idx[None, :]) % n) / n
    return np.cos(ang), np.sin(ang)


def _rope_tables(n_tokens):
    rows = n_tokens // GRID_W
    row = jnp.repeat(jnp.arange(rows, dtype=F32), GRID_W)
    col = jnp.tile(jnp.arange(GRID_W, dtype=F32), rows)
    n_freq = HEAD_DIM // 4
    inv = ROPE_THETA ** (-jnp.arange(n_freq, dtype=F32) / n_freq)
    ang = jnp.concatenate([row[:, None] * inv[None], col[:, None] * inv[None]], axis=-1)
    cos = jnp.repeat(jnp.cos(ang), 2, axis=-1)
    sin = jnp.repeat(jnp.sin(ang), 2, axis=-1)
    sign = jnp.tile(jnp.asarray([-1.0, 1.0], F32), HEAD_DIM // 2)
    return jnp.tile(cos, (1, 2)), jnp.tile(sin * sign, (1, 2))


def _fourier_tables():
    r = FFT_RADIX
    c, s = _dft_cos_sin(r)
    w1 = np.block([[c, s], [-s, c]])
    w2 = np.concatenate([c, s], axis=1)
    n_pos = r * r
    idx = np.arange(r, dtype=np.int64)
    ang = 2.0 * np.pi * (idx[:, None] * idx[None, :]) / n_pos
    scale = (n_pos * FOURIER_HEAD_DIM) ** -0.5
    twc = np.broadcast_to((np.cos(ang) * scale)[:, :, None], (r, r, LANES))
    tws = np.broadcast_to((np.sin(ang) * scale)[:, :, None], (r, r, LANES))
    cc, sc = _dft_cos_sin(FOURIER_HEAD_DIM)
    wc = np.concatenate([cc, -sc], axis=1)
    f32 = lambda a: jnp.asarray(np.ascontiguousarray(a), F32)
    return f32(w1).astype(BF16), f32(w2).astype(BF16), f32(twc), f32(tws), f32(wc).astype(BF16)


def kernel(x, c, ctx, c_ctx, ada_w, ada_b, norm_mix_g, norm_ffn_g, mix_in_w, mix_out_w, attn_sink,
           pool_w, pool_scale, ffn_w1, ffn_w3, ffn_w2, final_g):
    B, S, _ = x.shape
    L = ctx.shape[1]
    tm = 512
    tm_ctx = L

    cond = jnp.zeros((ADA_ROWS, D_MODEL), F32).at[:B].set(c).at[B].set(c_ctx)
    mods = _ada(cond, ada_w, ada_b)

    cosf, sinf = _rope_tables(S)
    cos_id = jnp.ones((L, LANES), F32)
    sin_id = jnp.zeros((L, LANES), F32)
    w1_dft, w2_dft, twc, tws, wc = _fourier_tables()
    cl, sl_ = _dft_cos_sin(L)
    w_ctx_dft = jnp.asarray(np.concatenate([cl, sl_], axis=1), F32).astype(BF16)

    last_ctx_reader = max(range(0, DEPTH, 2))
    h, hc = x, ctx
    for layer in range(DEPTH):
        m = mods[layer]
        lat = [m[:B, k * D_MODEL:(k + 1) * D_MODEL].reshape(B, 1, D_MODEL) for k in range(6)]
        cm = [jnp.broadcast_to(m[B, k * D_MODEL:(k + 1) * D_MODEL].reshape(1, 1, D_MODEL), (B, 1, D_MODEL))
              for k in range(6)]
        update_ctx = layer < last_ctx_reader
        g_mix = norm_mix_g[layer].reshape(1, D_MODEL)
        g_ffn = norm_ffn_g[layer].reshape(1, D_MODEL)
        w1 = ffn_w1[layer].astype(BF16)
        w3 = ffn_w3[layer].astype(BF16)
        w2 = ffn_w2[layer].astype(BF16)
        fin = final_g.reshape(1, D_MODEL) if layer == DEPTH - 1 else None
        j = layer // 2
        if layer % 2 == 0:
            w_in = mix_in_w[j].astype(BF16)
            w_out = mix_out_w[j].astype(BF16)
            sink = attn_sink[j]
            zr_c, zi_c, q_c, k_c, v_c = _in_proj(hc, g_mix, cm[0], cm[1], w_in, cos_id, sin_id, wc, tm_ctx)
            zr, zi, q, k, v = _in_proj(h, g_mix, lat[0], lat[1], w_in, cosf, sinf, wc, tm)
            four = _fourier_4096(zr, zi, w1_dft, w2_dft, twc, tws)
            attn = _attention(q, k, v, k_c, v_c, sink, True)
            h = _ffn(h, (four, attn, w_out, lat[2]), g_ffn, lat[3], lat[4], lat[5], w1, w3, w2, fin, tm)
            if update_ctx:
                four_c = _fourier_small(zr_c, zi_c, w_ctx_dft)
                attn_c = _attention(q_c, None, None, k_c, v_c, sink, False)
                hc = _ffn(hc, (four_c, attn_c, w_out, cm[2]), g_ffn, cm[3], cm[4], cm[5], w1, w3, w2,
                          None, tm_ctx)
        else:
            pw = pool_w[j].astype(BF16)
            psc = pool_scale[j].reshape(1, D_MODEL)
            h = _pool_mixer(h, g_mix, lat[0], lat[1], lat[2], pw, psc, tm)
            h = _ffn(h, None, g_ffn, lat[3], lat[4], lat[5], w1, w3, w2, fin, tm)
            if update_ctx:
                hc = _pool_mixer(hc, g_mix, cm[0], cm[1], cm[2], pw, psc, tm_ctx)
                hc = _ffn(hc, None, g_ffn, cm[3], cm[4], cm[5], w1, w3, w2, None, tm_ctx)
    return h
```

```python
import functools

import numpy as np
import jax
import jax.numpy as jnp
from jax import lax
from jax.experimental import pallas as pl
from jax.experimental.pallas import tpu as pltpu

F32 = jnp.float32
BF16 = jnp.bfloat16

D_MODEL = 1024
DEPTH = 4
GRID_W = 64
EPS = 1e-6
FOURIER_HEADS = 4
FOURIER_HEAD_DIM = 128
FOURIER_WIDTH = 512
HEAD_DIM = 64
N_Q_HEADS = 8
N_KV_HEADS = 2
GQA_GROUP = 4
ATTN_WIDTH = 512
KV_WIDTH = 128
Q_END = 1024
IN_WIDTH = 1280
BLOCK = 128
ATT_QBLOCKS = 4
ROPE_THETA = 10000.0
POOL_WINDOWS = (2, 4, 8, 16)
POOL_GROUP = 256
POOL_HALO = 8
POOL_PAD = 2 * POOL_HALO
FFN_HIDDEN = 2816
FFN_CHUNKS = ((0, 1024), (1024, 1024), (2048, 768))
FFN_PART_ROWS = 512
IN_PART_ROWS = 512
FFT_RADIX = 64
FFT_ROWS = 16
LANES = 128
LOG2E = 1.4426950408889634
ADA_ROWS = 16
VMEM_LIMIT = 56 * 1024 * 1024


def _params(*sem):
    return pltpu.CompilerParams(dimension_semantics=sem, vmem_limit_bytes=VMEM_LIMIT)


def _rms(x, g):
    ms = jnp.mean(x * x, axis=-1, keepdims=True)
    return x * lax.rsqrt(ms + EPS) * g


def _rms_mod(x, g, shift, scale):
    return _rms(x, g) * (1.0 + scale) + shift


def _dot(a, b):
    return jnp.dot(a, b, preferred_element_type=F32)


def _ada_kernel(c_ref, w_ref, b_ref, o_ref):
    a = jax.nn.silu(c_ref[...])
    o_ref[0] = jnp.dot(a, w_ref[0], precision=lax.Precision.HIGHEST,
                       preferred_element_type=F32) + b_ref[0]


def _ada(cond, ada_w, ada_b):
    tn = 1536
    n_out = ada_w.shape[-1]
    return pl.pallas_call(
        _ada_kernel,
        grid=(DEPTH, n_out // tn),
        in_specs=[
            pl.BlockSpec((ADA_ROWS, D_MODEL), lambda l, n: (0, 0)),
            pl.BlockSpec((1, D_MODEL, tn), lambda l, n: (l, 0, n)),
            pl.BlockSpec((1, 1, tn), lambda l, n: (l, 0, n)),
        ],
        out_specs=pl.BlockSpec((1, ADA_ROWS, tn), lambda l, n: (l, 0, n)),
        out_shape=jax.ShapeDtypeStruct((DEPTH, ADA_ROWS, n_out), F32),
        compiler_params=_params("arbitrary", "arbitrary"),
        name="ada_mod",
    )(cond, ada_w, ada_b.reshape(DEPTH, 1, n_out))


def _in_kernel(h_ref, g_ref, sh_ref, sc_ref, w_ref, cos_ref, sin_ref, wc_ref,
               zr_ref, zi_ref, q_ref, k_ref, v_ref, *, tm):
    n_parts = max(1, tm // IN_PART_ROWS)
    pr = tm // n_parts
    rows = [slice(p * pr, (p + 1) * pr) for p in range(n_parts)]
    projs = [_dot(_rms_mod(h_ref[0, r], g_ref[...], sh_ref[0], sc_ref[0]).astype(BF16), w_ref[...])
             for r in rows]
    wc = wc_ref[...]
    zs = [[_dot(proj[:, hh * LANES:(hh + 1) * LANES].astype(BF16), wc) for hh in range(FOURIER_HEADS)]
          for proj in projs]
    lane = lax.broadcasted_iota(jnp.int32, (pr, LANES), 1)
    even = (lane & 1) == 0
    low = lane < HEAD_DIM
    for r, proj, z4 in zip(rows, projs, zs):
        cosf = cos_ref[r]
        sinf = sin_ref[r]

        def rope(t):
            partner = jnp.where(even, pltpu.roll(t, LANES - 1, 1), pltpu.roll(t, 1, 1))
            return t * cosf + partner * sinf

        for hh, z in enumerate(z4):
            sl = slice(hh * LANES, (hh + 1) * LANES)
            zr_ref[0, r, sl] = z[:, :LANES].astype(BF16)
            zi_ref[0, r, sl] = z[:, LANES:].astype(BF16)
        for p in range(ATTN_WIDTH // LANES):
            sl = slice(p * LANES, (p + 1) * LANES)
            qq = proj[:, FOURIER_WIDTH + p * LANES:FOURIER_WIDTH + (p + 1) * LANES]
            q_ref[0, r, sl] = (rope(qq) * (HEAD_DIM ** -0.5 * LOG2E)).astype(BF16)
        kk = rope(proj[:, Q_END:Q_END + KV_WIDTH])
        vv = proj[:, Q_END + KV_WIDTH:]
        k_sw = pltpu.roll(kk, HEAD_DIM, 1)
        v_sw = pltpu.roll(vv, HEAD_DIM, 1)
        k_ref[0, 0, r] = jnp.where(low, kk, k_sw).astype(BF16)
        k_ref[0, 1, r] = jnp.where(low, k_sw, kk).astype(BF16)
        v_ref[0, 0, r] = jnp.where(low, vv, v_sw).astype(BF16)
        v_ref[0, 1, r] = jnp.where(low, v_sw, vv).astype(BF16)


def _in_proj(h, g, shift, scale, w_in, cosf, sinf, wc, tm):
    B, S, _ = h.shape
    vec = pl.BlockSpec((1, 1, D_MODEL), lambda b, i: (b, 0, 0))
    tok = lambda w: pl.BlockSpec((1, tm, w), lambda b, i: (b, i, 0))
    kv = pl.BlockSpec((1, N_KV_HEADS, tm, LANES), lambda b, i: (b, 0, i, 0))
    return pl.pallas_call(
        functools.partial(_in_kernel, tm=tm),
        grid=(B, S // tm),
        in_specs=[
            tok(D_MODEL),
            pl.BlockSpec((1, D_MODEL), lambda b, i: (0, 0)),
            vec, vec,
            pl.BlockSpec((D_MODEL, IN_WIDTH), lambda b, i: (0, 0)),
            pl.BlockSpec((tm, LANES), lambda b, i: (i, 0)),
            pl.BlockSpec((tm, LANES), lambda b, i: (i, 0)),
            pl.BlockSpec((FOURIER_HEAD_DIM, 2 * FOURIER_HEAD_DIM), lambda b, i: (0, 0)),
        ],
        out_specs=[tok(FOURIER_WIDTH), tok(FOURIER_WIDTH), tok(ATTN_WIDTH), kv, kv],
        out_shape=[
            jax.ShapeDtypeStruct((B, S, FOURIER_WIDTH), BF16),
            jax.ShapeDtypeStruct((B, S, FOURIER_WIDTH), BF16),
            jax.ShapeDtypeStruct((B, S, ATTN_WIDTH), BF16),
            jax.ShapeDtypeStruct((B, N_KV_HEADS, S, LANES), BF16),
            jax.ShapeDtypeStruct((B, N_KV_HEADS, S, LANES), BF16),
        ],
        compiler_params=_params("arbitrary", "arbitrary"),
        name="in_proj",
    )(h, g, shift, scale, w_in, cosf, sinf, wc)


def _fft1_kernel(zr_ref, zi_ref, w_ref, twc_ref, tws_ref, y_ref, *, t):
    xr = jnp.swapaxes(zr_ref[0], 0, 1)
    xi = jnp.swapaxes(zi_ref[0], 0, 1)
    w = w_ref[...]
    r = FFT_RADIX
    outs_r, outs_i = [], []
    for j in range(t):
        y = _dot(w, jnp.concatenate([xr[j], xi[j]], axis=0))
        tc = jnp.concatenate([twc_ref[j]] * FOURIER_HEADS, axis=1)
        ts = jnp.concatenate([tws_ref[j]] * FOURIER_HEADS, axis=1)
        yr = y[:r]
        yi = y[r:]
        outs_r.append((yr * tc + yi * ts).astype(BF16))
        outs_i.append((yi * tc - yr * ts).astype(BF16))
    y_ref[0, 0] = jnp.swapaxes(jnp.stack(outs_r, axis=0), 0, 1)
    y_ref[0, 1] = jnp.swapaxes(jnp.stack(outs_i, axis=0), 0, 1)


def _fft2_kernel(y_ref, w_ref, o_ref, *, t):
    w = w_ref[...]
    outs = []
    for j in range(t):
        x = jnp.concatenate([y_ref[0, 0, j], y_ref[0, 1, j]], axis=0)
        outs.append(_dot(w, x).astype(BF16))
    o_ref[0] = jnp.swapaxes(jnp.stack(outs, axis=0), 0, 1)


def _fourier_4096(zr, zi, w1, w2, twc, tws):
    B, S, W = zr.shape
    r = FFT_RADIX
    t = FFT_ROWS
    zr4 = zr.reshape(B, r, r, W)
    zi4 = zi.reshape(B, r, r, W)
    zspec = pl.BlockSpec((1, r, t, W), lambda b, n: (b, 0, n, 0))
    tspec = pl.BlockSpec((t, r, LANES), lambda b, n: (n, 0, 0))
    y = pl.pallas_call(
        functools.partial(_fft1_kernel, t=t),
        grid=(B, r // t),
        in_specs=[zspec, zspec, pl.BlockSpec((2 * r, 2 * r), lambda b, n: (0, 0)), tspec, tspec],
        out_specs=pl.BlockSpec((1, 2, r, t, W), lambda b, n: (b, 0, 0, n, 0)),
        out_shape=jax.ShapeDtypeStruct((B, 2, r, r, W), BF16),
        compiler_params=_params("arbitrary", "arbitrary"),
        name="fft_stage1",
    )(zr4, zi4, w1, twc, tws)
    out = pl.pallas_call(
        functools.partial(_fft2_kernel, t=t),
        grid=(B, r // t),
        in_specs=[
            pl.BlockSpec((1, 2, t, r, W), lambda b, k: (b, 0, k, 0, 0)),
            pl.BlockSpec((r, 2 * r), lambda b, k: (0, 0)),
        ],
        out_specs=pl.BlockSpec((1, r, t, W), lambda b, k: (b, 0, k, 0)),
        out_shape=jax.ShapeDtypeStruct((B, r, r, W), BF16),
        compiler_params=_params("arbitrary", "arbitrary"),
        name="fft_stage2",
    )(y, w2)
    return out.reshape(B, S, W)


def _dft_small_kernel(zr_ref, zi_ref, w_ref, o_ref, *, scale):
    x = jnp.concatenate([zr_ref[0], zi_ref[0]], axis=0)
    o_ref[0] = (_dot(w_ref[...], x) * scale).astype(BF16)


def _fourier_small(zr, zi, w):
    B, L, W = zr.shape
    scale = float((L * FOURIER_HEAD_DIM) ** -0.5)
    blk = pl.BlockSpec((1, L, W), lambda b: (b, 0, 0))
    return pl.pallas_call(
        functools.partial(_dft_small_kernel, scale=scale),
        grid=(B,),
        in_specs=[blk, blk, pl.BlockSpec((L, 2 * L), lambda b: (0, 0))],
        out_specs=blk,
        out_shape=jax.ShapeDtypeStruct((B, L, W), BF16),
        compiler_params=_params("arbitrary"),
        name="dft_ctx",
    )(zr, zi, w)


def _att_kernel(sink_ref, q_ref, *refs, n_steps, qb, window):
    if window:
        kp_ref, kc_ref, kn_ref, vp_ref, vc_ref, vn_ref, kx_ref, vx_ref, o_ref = refs
    else:
        kx_ref, vx_ref, o_ref = refs
    i = pl.program_id(1)
    lane = lax.broadcasted_iota(jnp.int32, (BLOCK, LANES), 1)
    low = lane < HEAD_DIM
    if window:
        row = lax.broadcasted_iota(jnp.int32, (BLOCK, BLOCK), 0)
        col = lax.broadcasted_iota(jnp.int32, (BLOCK, BLOCK), 1)
        ninf = jnp.float32(-jnp.inf)
        before = col >= row
        after = col <= row

    def window_tiles(p_ref, c_ref, n_ref, kh, a):
        cur = lambda j: c_ref[0, kh, j * BLOCK:(j + 1) * BLOCK]
        first = p_ref[0, kh] if a == 0 else cur(a - 1)
        last = n_ref[0, kh] if a == qb - 1 else cur(a + 1)
        return [first, cur(a), last]

    def scores(a, kh):
        if window:
            k = jnp.concatenate(window_tiles(kp_ref, kc_ref, kn_ref, kh, a) + [kx_ref[0, kh]], axis=0)
        else:
            k = kx_ref[0, kh]
        qs = []
        for g in range(GQA_GROUP):
            h = kh * GQA_GROUP + g
            qp = q_ref[0, a * BLOCK:(a + 1) * BLOCK, (h // 2) * LANES:(h // 2 + 1) * LANES]
            keep = low if h % 2 == 0 else jnp.logical_not(low)
            qs.append(jnp.where(keep, qp, jnp.zeros_like(qp)))
        qst = jnp.concatenate(qs, axis=0)
        return lax.dot_general(qst, k, (((1,), (1,)), ((), ())), preferred_element_type=F32)

    def finish(a, kh, s):
        if window:
            v = jnp.concatenate(window_tiles(vp_ref, vc_ref, vn_ref, kh, a) + [vx_ref[0, kh]], axis=0)
            ok_first = before & (i > 0) if a == 0 else before
            ok_last = after & (i < n_steps - 1) if a == qb - 1 else after
            b_first = jnp.where(ok_first, 0.0, ninf)
            b_last = jnp.where(ok_last, 0.0, ninf)
        else:
            v = vx_ref[0, kh]
        ps, ls = [], []
        for g in range(GQA_GROUP):
            sk = sink_ref[kh * GQA_GROUP + g] * LOG2E
            sg = s[g * BLOCK:(g + 1) * BLOCK]
            if window:
                sg = jnp.concatenate(
                    [sg[:, :BLOCK] + b_first, sg[:, BLOCK:2 * BLOCK],
                     sg[:, 2 * BLOCK:3 * BLOCK] + b_last, sg[:, 3 * BLOCK:]], axis=1)
            m = jnp.maximum(jnp.max(sg, axis=-1, keepdims=True), sk)
            p = jnp.exp2(sg - m)
            ls.append(jnp.sum(p, axis=-1, keepdims=True) + jnp.exp2(sk - m))
            ps.append(p.astype(BF16))
        o = _dot(jnp.concatenate(ps, axis=0), v)
        og = [o[g * BLOCK:(g + 1) * BLOCK] / ls[g] for g in range(GQA_GROUP)]
        for pr in range(GQA_GROUP // 2):
            c0 = kh * GQA_GROUP * HEAD_DIM + pr * LANES
            o_ref[0, a * BLOCK:(a + 1) * BLOCK, c0:c0 + LANES] = jnp.where(
                low, og[2 * pr], og[2 * pr + 1]).astype(BF16)

    units = [(a, kh) for a in range(qb) for kh in range(N_KV_HEADS)]
    ahead = 2
    pending = [scores(*u) for u in units[:ahead]]
    for n, u in enumerate(units):
        s = pending.pop(0)
        if n + ahead < len(units):
            pending.append(scores(*units[n + ahead]))
        finish(*u, s)


def _attention(q, k, v, kx, vx, sink, window):
    B, S, _ = q.shape
    nb = S // BLOCK
    qb = min(ATT_QBLOCKS, nb)
    rows = qb * BLOCK
    n_steps = S // rows
    L = kx.shape[2]
    qspec = pl.BlockSpec((1, rows, ATTN_WIDTH), lambda b, i: (b, i, 0))
    xspec = pl.BlockSpec((1, N_KV_HEADS, L, LANES), lambda b, i: (b, 0, 0, 0))
    smem = pl.BlockSpec(memory_space=pltpu.SMEM)
    if window:
        blk = (1, N_KV_HEADS, BLOCK, LANES)
        prev = pl.BlockSpec(blk, lambda b, i: (b, 0, jnp.maximum(i * qb - 1, 0), 0))
        cur = pl.BlockSpec((1, N_KV_HEADS, rows, LANES), lambda b, i: (b, 0, i, 0))
        nxt = pl.BlockSpec(blk, lambda b, i: (b, 0, jnp.minimum((i + 1) * qb, nb - 1), 0))
        in_specs = [smem, qspec, prev, cur, nxt, prev, cur, nxt, xspec, xspec]
        args = (sink, q, k, k, k, v, v, v, kx, vx)
    else:
        in_specs = [smem, qspec, xspec, xspec]
        args = (sink, q, kx, vx)
    return pl.pallas_call(
        functools.partial(_att_kernel, n_steps=n_steps, qb=qb, window=window),
        grid=(B, n_steps),
        in_specs=in_specs,
        out_specs=qspec,
        out_shape=jax.ShapeDtypeStruct((B, S, ATTN_WIDTH), BF16),
        compiler_params=_params("arbitrary", "arbitrary"),
        name="window_attn" if window else "ctx_attn",
    )(*args)


def _pool_kernel(hm_ref, hp_ref, hn_ref, g_ref, sh_ref, sc_ref, g1_ref, pw_ref, ps_ref, ic_ref,
                 o_ref, xs_ref, p_ref, *, tm, seq):
    i = pl.program_id(1)
    nt = seq // tm
    g = g_ref[...]
    sh = sh_ref[0]
    sc = sc_ref[0]
    xm = hm_ref[0]
    hal = POOL_HALO
    xs_ref[0:hal] = jnp.where(i > 0, _rms_mod(hp_ref[0], g, sh, sc), 0.0)
    xs_ref[hal:hal + tm] = _rms_mod(xm, g, sh, sc)
    xs_ref[hal + tm:2 * hal + tm] = jnp.where(i < nt - 1, _rms_mod(hn_ref[0], g, sh, sc), 0.0)
    xs_ref[2 * hal + tm:] = jnp.zeros((POOL_PAD, D_MODEL), F32)
    ys = []
    for gi, w in enumerate(POOL_WINDOWS):
        half = w // 2
        cs = slice(gi * POOL_GROUP, (gi + 1) * POOL_GROUP)
        if w == 2:
            acc = xs_ref[hal - 1:hal - 1 + tm, cs] + xs_ref[hal:hal + tm, cs]
        else:
            n2, n4, n8 = tm + 3 * hal, tm + 2 * hal, tm + hal
            p_ref[0:n2] = xs_ref[0:n2, cs] + xs_ref[1:n2 + 1, cs]
            if w == 4:
                acc = p_ref[hal - 2:hal - 2 + tm] + p_ref[hal:hal + tm]
            else:
                p_ref[0:n4] = p_ref[0:n4] + p_ref[2:n4 + 2]
                if w == 8:
                    acc = p_ref[hal - 4:hal - 4 + tm] + p_ref[hal:hal + tm]
                else:
                    p_ref[0:n8] = p_ref[0:n8] + p_ref[4:n8 + 4]
                    acc = p_ref[0:tm] + p_ref[hal:hal + tm]
        yg = acc * ic_ref[:, gi:gi + 1] - xs_ref[hal:hal + tm, cs]
        ys.append(_dot(yg.astype(BF16), pw_ref[gi]))
    y = jnp.concatenate(ys, axis=1) * ps_ref[...]
    o_ref[0] = xm + g1_ref[0] * y


def _pool_mixer(h, g, shift, scale, gate, pool_w, pool_scale, tm):
    B, S, _ = h.shape
    hb = tm // POOL_HALO
    vec = pl.BlockSpec((1, 1, D_MODEL), lambda b, i: (b, 0, 0))
    row = pl.BlockSpec((1, D_MODEL), lambda b, i: (0, 0))
    tok = pl.BlockSpec((1, tm, D_MODEL), lambda b, i: (b, i, 0))
    halo = (1, POOL_HALO, D_MODEL)
    t = np.arange(S)
    inv_count = jnp.asarray(np.stack(
        [1.0 / (np.minimum(t + w // 2, S) - np.maximum(t - w // 2, 0)) for w in POOL_WINDOWS], axis=1), F32)
    return pl.pallas_call(
        functools.partial(_pool_kernel, tm=tm, seq=S),
        grid=(B, S // tm),
        in_specs=[
            tok,
            pl.BlockSpec(halo, lambda b, i: (b, jnp.maximum(i * hb - 1, 0), 0)),
            pl.BlockSpec(halo, lambda b, i: (b, jnp.minimum((i + 1) * hb, S // POOL_HALO - 1), 0)),
            row, vec, vec, vec,
            pl.BlockSpec((len(POOL_WINDOWS), POOL_GROUP, POOL_GROUP), lambda b, i: (0, 0, 0)),
            row,
            pl.BlockSpec((tm, len(POOL_WINDOWS)), lambda b, i: (i, 0)),
        ],
        out_specs=tok,
        out_shape=jax.ShapeDtypeStruct((B, S, D_MODEL), F32),
        scratch_shapes=[pltpu.VMEM((tm + 2 * POOL_HALO + POOL_PAD, D_MODEL), F32),
                        pltpu.VMEM((tm + 3 * POOL_HALO, POOL_GROUP), F32)],
        compiler_params=_params("arbitrary", "arbitrary"),
        name="pool_mixer",
    )(h, h, h, g, shift, scale, gate, pool_w, pool_scale, inv_count)


def _ffn_kernel(*refs, tm, proj, final):
    refs = list(refs)
    h_ref = refs.pop(0)
    if proj:
        four_ref, attn_ref, wo_ref, g1_ref = refs[:4]
        refs = refs[4:]
    gn_ref, sh_ref, sc_ref, g2_ref, w1_ref, w3_ref, w2_ref = refs[:7]
    refs = refs[7:]
    if final:
        fg_ref = refs.pop(0)
    (o_ref,) = refs
    n_parts = max(1, tm // FFN_PART_ROWS)
    rows = [slice(p * (tm // n_parts), (p + 1) * (tm // n_parts)) for p in range(n_parts)]
    hs = [h_ref[0, r] for r in rows]
    if proj:
        ys = [_dot(four_ref[0, r], wo_ref[:FOURIER_WIDTH]) + _dot(attn_ref[0, r], wo_ref[FOURIER_WIDTH:])
              for r in rows]
        hs = [h + g1_ref[0] * y for h, y in zip(hs, ys)]
    hns = [_rms_mod(h, gn_ref[...], sh_ref[0], sc_ref[0]).astype(BF16) for h in hs]
    accs = [None] * n_parts
    for c0, cn in FFN_CHUNKS:
        ab = [(_dot(hn, w1_ref[:, c0:c0 + cn]), _dot(hn, w3_ref[:, c0:c0 + cn])) for hn in hns]
        for p, (a, b) in enumerate(ab):
            t = (jax.nn.silu(a) * b).astype(BF16)
            d = _dot(t, w2_ref[c0:c0 + cn, :])
            accs[p] = d if accs[p] is None else accs[p] + d
    for p, r in enumerate(rows):
        out = hs[p] + g2_ref[0] * accs[p]
        if final:
            out = _rms(out, fg_ref[...])
        o_ref[0, r] = out


def _ffn(h, mix, gn, shift, scale, gate, w1, w3, w2, final_g, tm):
    B, S, _ = h.shape
    vec = pl.BlockSpec((1, 1, D_MODEL), lambda b, i: (b, 0, 0))
    row = pl.BlockSpec((1, D_MODEL), lambda b, i: (0, 0))
    tok = lambda w: pl.BlockSpec((1, tm, w), lambda b, i: (b, i, 0))
    whole = pl.BlockSpec(memory_space=pltpu.VMEM)
    in_specs = [tok(D_MODEL)]
    args = [h]
    if mix is not None:
        four, attn, w_out, g1 = mix
        in_specs += [tok(FOURIER_WIDTH), tok(ATTN_WIDTH), whole, vec]
        args += [four, attn, w_out, g1]
    in_specs += [row, vec, vec, vec, whole, whole, whole]
    args += [gn, shift, scale, gate, w1, w3, w2]
    if final_g is not None:
        in_specs.append(row)
        args.append(final_g)
    return pl.pallas_call(
        functools.partial(_ffn_kernel, tm=tm, proj=mix is not None, final=final_g is not None),
        grid=(B, S // tm),
        in_specs=in_specs,
        out_specs=tok(D_MODEL),
        out_shape=jax.ShapeDtypeStruct((B, S, D_MODEL), F32),
        compiler_params=_params("arbitrary", "arbitrary"),
        name="ffn",
    )(*args)


def _dft_cos_sin(n):
    idx = np.arange(n, dtype=np.int64)
    ang = 2.0 * np.pi * ((idx[:, None] * idx[None, :]) % n) / n
    return np.cos(ang), np.sin(ang)


def _rope_tables(n_tokens):
    rows = n_tokens // GRID_W
    row = jnp.repeat(jnp.arange(rows, dtype=F32), GRID_W)
    col = jnp.tile(jnp.arange(GRID_W, dtype=F32), rows)
    n_freq = HEAD_DIM // 4
    inv = ROPE_THETA ** (-jnp.arange(n_freq, dtype=F32) / n_freq)
    ang = jnp.concatenate([row[:, None] * inv[None], col[:, None] * inv[None]], axis=-1)
    cos = jnp.repeat(jnp.cos(ang), 2, axis=-1)
    sin = jnp.repeat(jnp.sin(ang), 2, axis=-1)
    sign = jnp.tile(jnp.asarray([-1.0, 1.0], F32), HEAD_DIM // 2)
    return jnp.tile(cos, (1, 2)), jnp.tile(sin * sign, (1, 2))


def _fourier_tables():
    r = FFT_RADIX
    c, s = _dft_cos_sin(r)
    w1 = np.block([[c, s], [-s, c]])
    w2 = np.concatenate([c, s], axis=1)
    n_pos = r * r
    idx = np.arange(r, dtype=np.int64)
    ang = 2.0 * np.pi * (idx[:, None] * idx[None, :]) / n_pos
    scale = (n_pos * FOURIER_HEAD_DIM) ** -0.5
    twc = np.broadcast_to((np.cos(ang) * scale)[:, :, None], (r, r, LANES))
    tws = np.broadcast_to((np.sin(ang) * scale)[:, :, None], (r, r, LANES))
    cc, sc = _dft_cos_sin(FOURIER_HEAD_DIM)
    wc = np.concatenate([cc, -sc], axis=1)
    f32 = lambda a: jnp.asarray(np.ascontiguousarray(a), F32)
    return f32(w1).astype(BF16), f32(w2).astype(BF16), f32(twc), f32(tws), f32(wc).astype(BF16)


def kernel(x, c, ctx, c_ctx, ada_w, ada_b, norm_mix_g, norm_ffn_g, mix_in_w, mix_out_w, attn_sink,
           pool_w, pool_scale, ffn_w1, ffn_w3, ffn_w2, final_g):
    B, S, _ = x.shape
    L = ctx.shape[1]
    tm = 512
    tm_ffn = 2 * FFN_PART_ROWS
    tm_ctx = L

    cond = jnp.zeros((ADA_ROWS, D_MODEL), F32).at[:B].set(c).at[B].set(c_ctx)
    mods = _ada(cond, ada_w, ada_b)

    cosf, sinf = _rope_tables(S)
    cos_id = jnp.ones((L, LANES), F32)
    sin_id = jnp.zeros((L, LANES), F32)
    w1_dft, w2_dft, twc, tws, wc = _fourier_tables()
    cl, sl_ = _dft_cos_sin(L)
    w_ctx_dft = jnp.asarray(np.concatenate([cl, sl_], axis=1), F32).astype(BF16)

    last_ctx_reader = max(range(0, DEPTH, 2))
    h, hc = x, ctx
    for layer in range(DEPTH):
        m = mods[layer]
        lat = [m[:B, k * D_MODEL:(k + 1) * D_MODEL].reshape(B, 1, D_MODEL) for k in range(6)]
        cm = [jnp.broadcast_to(m[B, k * D_MODEL:(k + 1) * D_MODEL].reshape(1, 1, D_MODEL), (B, 1, D_MODEL))
              for k in range(6)]
        update_ctx = layer < last_ctx_reader
        g_mix = norm_mix_g[layer].reshape(1, D_MODEL)
        g_ffn = norm_ffn_g[layer].reshape(1, D_MODEL)
        w1 = ffn_w1[layer].astype(BF16)
        w3 = ffn_w3[layer].astype(BF16)
        w2 = ffn_w2[layer].astype(BF16)
        fin = final_g.reshape(1, D_MODEL) if layer == DEPTH - 1 else None
        j = layer // 2
        if layer % 2 == 0:
            w_in = mix_in_w[j].astype(BF16)
            w_out = mix_out_w[j].astype(BF16)
            sink = attn_sink[j]
            zr_c, zi_c, q_c, k_c, v_c = _in_proj(hc, g_mix, cm[0], cm[1], w_in, cos_id, sin_id, wc, tm_ctx)
            zr, zi, q, k, v = _in_proj(h, g_mix, lat[0], lat[1], w_in, cosf, sinf, wc, 4 * IN_PART_ROWS)
            four = _fourier_4096(zr, zi, w1_dft, w2_dft, twc, tws)
            attn = _attention(q, k, v, k_c, v_c, sink, True)
            h = _ffn(h, (four, attn, w_out, lat[2]), g_ffn, lat[3], lat[4], lat[5], w1, w3, w2, fin, tm_ffn)
            if update_ctx:
                four_c = _fourier_small(zr_c, zi_c, w_ctx_dft)
                attn_c = _attention(q_c, None, None, k_c, v_c, sink, False)
                hc = _ffn(hc, (four_c, attn_c, w_out, cm[2]), g_ffn, cm[3], cm[4], cm[5], w1, w3, w2,
                          None, tm_ctx)
        else:
            pw = pool_w[j].astype(BF16)
            psc = pool_scale[j].reshape(1, D_MODEL)
            h = _pool_mixer(h, g_mix, lat[0], lat[1], lat[2], pw, psc, tm)
            h = _ffn(h, None, g_ffn, lat[3], lat[4], lat[5], w1, w3, w2, fin, tm_ffn)
            if update_ctx:
                hc = _pool_mixer(hc, g_mix, cm[0], cm[1], cm[2], pw, psc, tm_ctx)
                hc = _ffn(hc, None, g_ffn, cm[3], cm[4], cm[5], w1, w3, w2, None, tm_ctx)
    return h
```

```python
import functools

import numpy as np
import jax
import jax.numpy as jnp
from jax import lax
from jax.experimental import pallas as pl
from jax.experimental.pallas import tpu as pltpu

F32 = jnp.float32
BF16 = jnp.bfloat16

D_MODEL = 1024
DEPTH = 4
GRID_W = 64
EPS = 1e-6
FOURIER_HEADS = 4
FOURIER_HEAD_DIM = 128
FOURIER_WIDTH = 512
HEAD_DIM = 64
N_Q_HEADS = 8
N_KV_HEADS = 2
GQA_GROUP = 4
ATTN_WIDTH = 512
KV_WIDTH = 128
Q_END = 1024
IN_WIDTH = 1280
BLOCK = 128
ATT_QBLOCKS = 4
ROPE_THETA = 10000.0
POOL_WINDOWS = (2, 4, 8, 16)
POOL_GROUP = 256
POOL_HALO = 8
POOL_PAD = 2 * POOL_HALO
FFN_HIDDEN = 2816
FFN_CHUNKS = ((0, 1024), (1024, 1024), (2048, 768))
FFN_PART_ROWS = 512
IN_PART_ROWS = 512
FFT_RADIX = 64
FFT_ROWS = 16
LANES = 128
LOG2E = 1.4426950408889634
ADA_ROWS = 16
VMEM_LIMIT = 56 * 1024 * 1024


def _params(*sem):
    return pltpu.CompilerParams(dimension_semantics=sem, vmem_limit_bytes=VMEM_LIMIT)


def _rms(x, g):
    ms = jnp.mean(x * x, axis=-1, keepdims=True)
    return x * lax.rsqrt(ms + EPS) * g


def _rms_mod(x, g, shift, scale):
    return _rms(x, g) * (1.0 + scale) + shift


def _dot(a, b):
    return jnp.dot(a, b, preferred_element_type=F32)


def _ada_kernel(c_ref, w_ref, b_ref, o_ref):
    a = jax.nn.silu(c_ref[...])
    o_ref[0] = jnp.dot(a, w_ref[0], precision=lax.Precision.HIGHEST,
                       preferred_element_type=F32) + b_ref[0]


def _ada(cond, ada_w, ada_b):
    tn = 3072
    n_out = ada_w.shape[-1]
    return pl.pallas_call(
        _ada_kernel,
        grid=(DEPTH, n_out // tn),
        in_specs=[
            pl.BlockSpec((ADA_ROWS, D_MODEL), lambda l, n: (0, 0)),
            pl.BlockSpec((1, D_MODEL, tn), lambda l, n: (l, 0, n)),
            pl.BlockSpec((1, 1, tn), lambda l, n: (l, 0, n)),
        ],
        out_specs=pl.BlockSpec((1, ADA_ROWS, tn), lambda l, n: (l, 0, n)),
        out_shape=jax.ShapeDtypeStruct((DEPTH, ADA_ROWS, n_out), F32),
        compiler_params=_params("arbitrary", "arbitrary"),
        name="ada_mod",
    )(cond, ada_w, ada_b.reshape(DEPTH, 1, n_out))


def _in_kernel(h_ref, g_ref, sh_ref, sc_ref, w_ref, cos_ref, sin_ref, wc_ref,
               zr_ref, zi_ref, q_ref, k_ref, v_ref, *, tm):
    n_parts = max(1, tm // IN_PART_ROWS)
    pr = tm // n_parts
    rows = [slice(p * pr, (p + 1) * pr) for p in range(n_parts)]
    projs = [_dot(_rms_mod(h_ref[0, r], g_ref[...], sh_ref[0], sc_ref[0]).astype(BF16), w_ref[...])
             for r in rows]
    wc = wc_ref[...]
    zs = [[_dot(proj[:, hh * LANES:(hh + 1) * LANES].astype(BF16), wc) for hh in range(FOURIER_HEADS)]
          for proj in projs]
    lane = lax.broadcasted_iota(jnp.int32, (pr, LANES), 1)
    even = (lane & 1) == 0
    low = lane < HEAD_DIM
    for r, proj, z4 in zip(rows, projs, zs):
        cosf = cos_ref[r]
        sinf = sin_ref[r]

        def rope(t):
            partner = jnp.where(even, pltpu.roll(t, LANES - 1, 1), pltpu.roll(t, 1, 1))
            return t * cosf + partner * sinf

        for hh, z in enumerate(z4):
            sl = slice(hh * LANES, (hh + 1) * LANES)
            zr_ref[0, r, sl] = z[:, :LANES].astype(BF16)
            zi_ref[0, r, sl] = z[:, LANES:].astype(BF16)
        for p in range(ATTN_WIDTH // LANES):
            sl = slice(p * LANES, (p + 1) * LANES)
            qq = proj[:, FOURIER_WIDTH + p * LANES:FOURIER_WIDTH + (p + 1) * LANES]
            q_ref[0, r, sl] = (rope(qq) * (HEAD_DIM ** -0.5 * LOG2E)).astype(BF16)
        kk = rope(proj[:, Q_END:Q_END + KV_WIDTH])
        vv = proj[:, Q_END + KV_WIDTH:]
        k_sw = pltpu.roll(kk, HEAD_DIM, 1)
        v_sw = pltpu.roll(vv, HEAD_DIM, 1)
        k_ref[0, 0, r] = jnp.where(low, kk, k_sw).astype(BF16)
        k_ref[0, 1, r] = jnp.where(low, k_sw, kk).astype(BF16)
        v_ref[0, 0, r] = jnp.where(low, vv, v_sw).astype(BF16)
        v_ref[0, 1, r] = jnp.where(low, v_sw, vv).astype(BF16)


def _in_proj(h, g, shift, scale, w_in, cosf, sinf, wc, tm):
    B, S, _ = h.shape
    vec = pl.BlockSpec((1, 1, D_MODEL), lambda b, i: (b, 0, 0))
    tok = lambda w: pl.BlockSpec((1, tm, w), lambda b, i: (b, i, 0))
    kv = pl.BlockSpec((1, N_KV_HEADS, tm, LANES), lambda b, i: (b, 0, i, 0))
    return pl.pallas_call(
        functools.partial(_in_kernel, tm=tm),
        grid=(B, S // tm),
        in_specs=[
            tok(D_MODEL),
            pl.BlockSpec((1, D_MODEL), lambda b, i: (0, 0)),
            vec, vec,
            pl.BlockSpec((D_MODEL, IN_WIDTH), lambda b, i: (0, 0)),
            pl.BlockSpec((tm, LANES), lambda b, i: (i, 0)),
            pl.BlockSpec((tm, LANES), lambda b, i: (i, 0)),
            pl.BlockSpec((FOURIER_HEAD_DIM, 2 * FOURIER_HEAD_DIM), lambda b, i: (0, 0)),
        ],
        out_specs=[tok(FOURIER_WIDTH), tok(FOURIER_WIDTH), tok(ATTN_WIDTH), kv, kv],
        out_shape=[
            jax.ShapeDtypeStruct((B, S, FOURIER_WIDTH), BF16),
            jax.ShapeDtypeStruct((B, S, FOURIER_WIDTH), BF16),
            jax.ShapeDtypeStruct((B, S, ATTN_WIDTH), BF16),
            jax.ShapeDtypeStruct((B, N_KV_HEADS, S, LANES), BF16),
            jax.ShapeDtypeStruct((B, N_KV_HEADS, S, LANES), BF16),
        ],
        compiler_params=_params("arbitrary", "arbitrary"),
        name="in_proj",
    )(h, g, shift, scale, w_in, cosf, sinf, wc)


def _fft1_kernel(zr_ref, zi_ref, w_ref, y_ref, *, t):
    xr = jnp.swapaxes(zr_ref[0], 0, 1)
    xi = jnp.swapaxes(zi_ref[0], 0, 1)
    w = w_ref[...]
    r = FFT_RADIX
    outs_r, outs_i = [], []
    for j in range(t):
        y = _dot(w, jnp.concatenate([xr[j], xi[j]], axis=0))
        outs_r.append(y[:r].astype(BF16))
        outs_i.append(y[r:].astype(BF16))
    y_ref[0, 0] = jnp.swapaxes(jnp.stack(outs_r, axis=0), 0, 1)
    y_ref[0, 1] = jnp.swapaxes(jnp.stack(outs_i, axis=0), 0, 1)


def _fft2_kernel(y_ref, w_ref, o_ref, *, t):
    outs = []
    for j in range(t):
        x = jnp.concatenate([y_ref[0, 0, j], y_ref[0, 1, j]], axis=0)
        outs.append(_dot(w_ref[j], x).astype(BF16))
    o_ref[0] = jnp.swapaxes(jnp.stack(outs, axis=0), 0, 1)


def _fourier_4096(zr, zi, w1, w2):
    B, S, W = zr.shape
    r = FFT_RADIX
    t = FFT_ROWS
    zr4 = zr.reshape(B, r, r, W)
    zi4 = zi.reshape(B, r, r, W)
    zspec = pl.BlockSpec((1, r, t, W), lambda b, n: (b, 0, n, 0))
    y = pl.pallas_call(
        functools.partial(_fft1_kernel, t=t),
        grid=(B, r // t),
        in_specs=[zspec, zspec, pl.BlockSpec((2 * r, 2 * r), lambda b, n: (0, 0))],
        out_specs=pl.BlockSpec((1, 2, r, t, W), lambda b, n: (b, 0, 0, n, 0)),
        out_shape=jax.ShapeDtypeStruct((B, 2, r, r, W), BF16),
        compiler_params=_params("arbitrary", "arbitrary"),
        name="fft_stage1",
    )(zr4, zi4, w1)
    out = pl.pallas_call(
        functools.partial(_fft2_kernel, t=t),
        grid=(B, r // t),
        in_specs=[
            pl.BlockSpec((1, 2, t, r, W), lambda b, k: (b, 0, k, 0, 0)),
            pl.BlockSpec((t, r, 2 * r), lambda b, k: (k, 0, 0)),
        ],
        out_specs=pl.BlockSpec((1, r, t, W), lambda b, k: (b, 0, k, 0)),
        out_shape=jax.ShapeDtypeStruct((B, r, r, W), BF16),
        compiler_params=_params("arbitrary", "arbitrary"),
        name="fft_stage2",
    )(y, w2)
    return out.reshape(B, S, W)


def _dft_small_kernel(zr_ref, zi_ref, w_ref, o_ref, *, scale):
    x = jnp.concatenate([zr_ref[0], zi_ref[0]], axis=0)
    o_ref[0] = (_dot(w_ref[...], x) * scale).astype(BF16)


def _fourier_small(zr, zi, w):
    B, L, W = zr.shape
    scale = float((L * FOURIER_HEAD_DIM) ** -0.5)
    blk = pl.BlockSpec((1, L, W), lambda b: (b, 0, 0))
    return pl.pallas_call(
        functools.partial(_dft_small_kernel, scale=scale),
        grid=(B,),
        in_specs=[blk, blk, pl.BlockSpec((L, 2 * L), lambda b: (0, 0))],
        out_specs=blk,
        out_shape=jax.ShapeDtypeStruct((B, L, W), BF16),
        compiler_params=_params("arbitrary"),
        name="dft_ctx",
    )(zr, zi, w)


def _att_kernel(sink_ref, q_ref, *refs, n_steps, qb, window):
    if window:
        kp_ref, kc_ref, kn_ref, vp_ref, vc_ref, vn_ref, kx_ref, vx_ref, o_ref = refs
    else:
        kx_ref, vx_ref, o_ref = refs
    i = pl.program_id(1)
    lane = lax.broadcasted_iota(jnp.int32, (BLOCK, LANES), 1)
    low = lane < HEAD_DIM
    if window:
        row = lax.broadcasted_iota(jnp.int32, (BLOCK, BLOCK), 0)
        col = lax.broadcasted_iota(jnp.int32, (BLOCK, BLOCK), 1)
        ninf = jnp.float32(-jnp.inf)
        before = col >= row
        after = col <= row

    def window_tiles(p_ref, c_ref, n_ref, kh, a):
        cur = lambda j: c_ref[0, kh, j * BLOCK:(j + 1) * BLOCK]
        first = p_ref[0, kh] if a == 0 else cur(a - 1)
        last = n_ref[0, kh] if a == qb - 1 else cur(a + 1)
        return [first, cur(a), last]

    def scores(a, kh):
        if window:
            k = jnp.concatenate(window_tiles(kp_ref, kc_ref, kn_ref, kh, a) + [kx_ref[0, kh]], axis=0)
        else:
            k = kx_ref[0, kh]
        qs = []
        for g in range(GQA_GROUP):
            h = kh * GQA_GROUP + g
            qp = q_ref[0, a * BLOCK:(a + 1) * BLOCK, (h // 2) * LANES:(h // 2 + 1) * LANES]
            keep = low if h % 2 == 0 else jnp.logical_not(low)
            qs.append(jnp.where(keep, qp, jnp.zeros_like(qp)))
        qst = jnp.concatenate(qs, axis=0)
        return lax.dot_general(qst, k, (((1,), (1,)), ((), ())), preferred_element_type=F32)

    def finish(a, kh, s):
        if window:
            v = jnp.concatenate(window_tiles(vp_ref, vc_ref, vn_ref, kh, a) + [vx_ref[0, kh]], axis=0)
            ok_first = before & (i > 0) if a == 0 else before
            ok_last = after & (i < n_steps - 1) if a == qb - 1 else after
            b_first = jnp.where(ok_first, 0.0, ninf)
            b_last = jnp.where(ok_last, 0.0, ninf)
        else:
            v = vx_ref[0, kh]
        ps, ls = [], []
        for g in range(GQA_GROUP):
            sk = sink_ref[kh * GQA_GROUP + g] * LOG2E
            sg = s[g * BLOCK:(g + 1) * BLOCK]
            if window:
                sg = jnp.concatenate(
                    [sg[:, :BLOCK] + b_first, sg[:, BLOCK:2 * BLOCK],
                     sg[:, 2 * BLOCK:3 * BLOCK] + b_last, sg[:, 3 * BLOCK:]], axis=1)
            m = jnp.maximum(jnp.max(sg, axis=-1, keepdims=True), sk)
            p = jnp.exp2(sg - m)
            ls.append(jnp.sum(p, axis=-1, keepdims=True) + jnp.exp2(sk - m))
            ps.append(p.astype(BF16))
        o = _dot(jnp.concatenate(ps, axis=0), v)
        og = [o[g * BLOCK:(g + 1) * BLOCK] / ls[g] for g in range(GQA_GROUP)]
        for pr in range(GQA_GROUP // 2):
            c0 = kh * GQA_GROUP * HEAD_DIM + pr * LANES
            o_ref[0, a * BLOCK:(a + 1) * BLOCK, c0:c0 + LANES] = jnp.where(
                low, og[2 * pr], og[2 * pr + 1]).astype(BF16)

    units = [(a, kh) for a in range(qb) for kh in range(N_KV_HEADS)]
    ahead = 2
    pending = [scores(*u) for u in units[:ahead]]
    for n, u in enumerate(units):
        s = pending.pop(0)
        if n + ahead < len(units):
            pending.append(scores(*units[n + ahead]))
        finish(*u, s)


def _attention(q, k, v, kx, vx, sink, window):
    B, S, _ = q.shape
    nb = S // BLOCK
    qb = min(ATT_QBLOCKS, nb)
    rows = qb * BLOCK
    n_steps = S // rows
    L = kx.shape[2]
    qspec = pl.BlockSpec((1, rows, ATTN_WIDTH), lambda b, i: (b, i, 0))
    xspec = pl.BlockSpec((1, N_KV_HEADS, L, LANES), lambda b, i: (b, 0, 0, 0))
    smem = pl.BlockSpec(memory_space=pltpu.SMEM)
    if window:
        blk = (1, N_KV_HEADS, BLOCK, LANES)
        prev = pl.BlockSpec(blk, lambda b, i: (b, 0, jnp.maximum(i * qb - 1, 0), 0))
        cur = pl.BlockSpec((1, N_KV_HEADS, rows, LANES), lambda b, i: (b, 0, i, 0))
        nxt = pl.BlockSpec(blk, lambda b, i: (b, 0, jnp.minimum((i + 1) * qb, nb - 1), 0))
        in_specs = [smem, qspec, prev, cur, nxt, prev, cur, nxt, xspec, xspec]
        args = (sink, q, k, k, k, v, v, v, kx, vx)
    else:
        in_specs = [smem, qspec, xspec, xspec]
        args = (sink, q, kx, vx)
    return pl.pallas_call(
        functools.partial(_att_kernel, n_steps=n_steps, qb=qb, window=window),
        grid=(B, n_steps),
        in_specs=in_specs,
        out_specs=qspec,
        out_shape=jax.ShapeDtypeStruct((B, S, ATTN_WIDTH), BF16),
        compiler_params=_params("arbitrary", "arbitrary"),
        name="window_attn" if window else "ctx_attn",
    )(*args)


def _pool_fill(xs_ref, h_ref, hp_ref, hx_ref, g, sh, sc, lo, hi, tm, first_tile, last_tile):
    hal = POOL_HALO
    if lo == 0:
        xs_ref[0:hal] = jnp.where(first_tile, 0.0, _rms_mod(hp_ref[0], g, sh, sc))
    xs_ref[hal + lo:hal + hi] = _rms_mod(h_ref[0, lo:hi], g, sh, sc)
    if hi == tm:
        xs_ref[hal + tm:2 * hal + tm] = jnp.where(last_tile, 0.0, _rms_mod(hx_ref[0], g, sh, sc))
        xs_ref[2 * hal + tm:] = jnp.zeros((POOL_PAD, D_MODEL), F32)


def _pool_part(xs_ref, p_ref, ic_ref, pw_ref, r0, pr):
    hal = POOL_HALO
    ys = []
    for gi, w in enumerate(POOL_WINDOWS):
        cs = slice(gi * POOL_GROUP, (gi + 1) * POOL_GROUP)
        x = lambda off, n: xs_ref[r0 + off:r0 + off + n, cs]
        if w == 2:
            acc = x(hal - 1, pr) + x(hal, pr)
        else:
            n2, n4, n8 = pr + 3 * hal, pr + 2 * hal, pr + hal
            p_ref[0:n2] = x(0, n2) + x(1, n2)
            if w == 4:
                acc = p_ref[hal - 2:hal - 2 + pr] + p_ref[hal:hal + pr]
            else:
                p_ref[0:n4] = p_ref[0:n4] + p_ref[2:n4 + 2]
                if w == 8:
                    acc = p_ref[hal - 4:hal - 4 + pr] + p_ref[hal:hal + pr]
                else:
                    p_ref[0:n8] = p_ref[0:n8] + p_ref[4:n8 + 4]
                    acc = p_ref[0:pr] + p_ref[hal:hal + pr]
        yg = acc * ic_ref[r0:r0 + pr, gi:gi + 1] - x(hal, pr)
        ys.append(_dot(yg.astype(BF16), pw_ref[gi]))
    return jnp.concatenate(ys, axis=1)


def _ffn_kernel(*refs, tm, seq, mode, final):
    refs = list(refs)
    take = lambda n: [refs.pop(0) for _ in range(n)]
    (h_ref,) = take(1)
    if mode == "proj":
        four_ref, attn_ref, wo_ref, g1_ref = take(4)
    elif mode == "pool":
        hp_ref, hx_ref, gm_ref, sh1_ref, sc1_ref, g1_ref, pw_ref, ps_ref, ic_ref = take(9)
    gn_ref, sh_ref, sc_ref, g2_ref, w1_ref, w3_ref, w2_ref = take(7)
    if final:
        (fg_ref,) = take(1)
    (o_ref,) = take(1)
    if mode == "pool":
        xs_ref, p_ref = take(2)
        first_tile = pl.program_id(1) == 0
        last_tile = pl.program_id(1) == seq // tm - 1
    n_parts = max(1, tm // FFN_PART_ROWS)
    pr = tm // n_parts
    rows = [slice(p * pr, (p + 1) * pr) for p in range(n_parts)]

    def mixed(p):
        h = h_ref[0, rows[p]]
        if mode == "proj":
            y = (_dot(four_ref[0, rows[p]], wo_ref[:FOURIER_WIDTH])
                 + _dot(attn_ref[0, rows[p]], wo_ref[FOURIER_WIDTH:]))
            return h + g1_ref[0] * y
        if mode == "pool":
            lo = 0 if p == 0 else p * pr + POOL_PAD
            hi = tm if p == n_parts - 1 else (p + 1) * pr + POOL_PAD
            _pool_fill(xs_ref, h_ref, hp_ref, hx_ref, gm_ref[...], sh1_ref[0], sc1_ref[0], lo, hi, tm,
                       first_tile, last_tile)
            y = _pool_part(xs_ref, p_ref, ic_ref, pw_ref, p * pr, pr) * ps_ref[...]
            return h + g1_ref[0] * y
        return h

    def normed(h):
        return _rms_mod(h, gn_ref[...], sh_ref[0], sc_ref[0]).astype(BF16)

    def up(hn, c0, cn):
        return _dot(hn, w1_ref[0, :, c0:c0 + cn]), _dot(hn, w3_ref[0, :, c0:c0 + cn])

    c0, cn = FFN_CHUNKS[0]
    if mode == "pool":
        hs, hns, ab = [], [], []
        for p in range(n_parts):
            hs.append(mixed(p))
            hns.append(normed(hs[p]))
            ab.append(up(hns[p], c0, cn))
    else:
        hs = [mixed(p) for p in range(n_parts)]
        hns = [normed(h) for h in hs]
        ab = [up(hn, c0, cn) for hn in hns]
    accs = [None] * n_parts
    for ci, (c0, cn) in enumerate(FFN_CHUNKS):
        if ci > 0:
            ab = [up(hn, c0, cn) for hn in hns]
        for p, (a, b) in enumerate(ab):
            t = (jax.nn.silu(a) * b).astype(BF16)
            d = _dot(t, w2_ref[0, c0:c0 + cn, :])
            accs[p] = d if accs[p] is None else accs[p] + d
    for p, r in enumerate(rows):
        out = hs[p] + g2_ref[0] * accs[p]
        if final:
            out = _rms(out, fg_ref[...])
        o_ref[0, r] = out


def _ffn(h, mix, gn, shift, scale, gate, ffn_w, layer, final_g, tm):
    B, S, _ = h.shape
    mode = "none" if mix is None else mix[0]
    vec = pl.BlockSpec((1, 1, D_MODEL), lambda b, i: (b, 0, 0))
    row = pl.BlockSpec((1, D_MODEL), lambda b, i: (0, 0))
    tok = lambda w: pl.BlockSpec((1, tm, w), lambda b, i: (b, i, 0))
    whole = pl.BlockSpec(memory_space=pltpu.VMEM)
    in_specs = [tok(D_MODEL)]
    args = [h]
    scratch = []
    if mode == "proj":
        _, four, attn, w_out, g1 = mix
        in_specs += [tok(FOURIER_WIDTH), tok(ATTN_WIDTH), whole, vec]
        args += [four, attn, w_out, g1]
    elif mode == "pool":
        _, g_mix, shift1, scale1, g1, pool_w, pool_scale = mix
        hb = tm // POOL_HALO
        halo = (1, POOL_HALO, D_MODEL)
        t = np.arange(S)
        inv_count = jnp.asarray(np.stack(
            [1.0 / (np.minimum(t + w // 2, S) - np.maximum(t - w // 2, 0)) for w in POOL_WINDOWS], axis=1), F32)
        in_specs += [
            pl.BlockSpec(halo, lambda b, i: (b, jnp.maximum(i * hb - 1, 0), 0)),
            pl.BlockSpec(halo, lambda b, i: (b, jnp.minimum((i + 1) * hb, S // POOL_HALO - 1), 0)),
            row, vec, vec, vec,
            pl.BlockSpec((len(POOL_WINDOWS), POOL_GROUP, POOL_GROUP), lambda b, i: (0, 0, 0)),
            row,
            pl.BlockSpec((tm, len(POOL_WINDOWS)), lambda b, i: (i, 0)),
        ]
        args += [h, h, g_mix, shift1, scale1, g1, pool_w, pool_scale, inv_count]
        pr = tm // max(1, tm // FFN_PART_ROWS)
        scratch = [pltpu.VMEM((tm + 2 * POOL_HALO + POOL_PAD, D_MODEL), F32),
                   pltpu.VMEM((pr + 3 * POOL_HALO, POOL_GROUP), F32)]
    in_specs += [row, vec, vec, vec] + [
        pl.BlockSpec((1,) + w.shape[1:], lambda b, i: (layer, 0, 0), pipeline_mode=pl.Buffered(1)) for w in ffn_w]
    args += [gn, shift, scale, gate, *ffn_w]
    if final_g is not None:
        in_specs.append(row)
        args.append(final_g)
    return pl.pallas_call(
        functools.partial(_ffn_kernel, tm=tm, seq=S, mode=mode, final=final_g is not None),
        grid=(B, S // tm),
        in_specs=in_specs,
        out_specs=tok(D_MODEL),
        out_shape=jax.ShapeDtypeStruct((B, S, D_MODEL), F32),
        scratch_shapes=scratch,
        compiler_params=_params("arbitrary", "arbitrary"),
        name="ffn_" + mode,
    )(*args)


def _dft_cos_sin(n):
    idx = np.arange(n, dtype=np.int64)
    ang = 2.0 * np.pi * ((idx[:, None] * idx[None, :]) % n) / n
    return np.cos(ang), np.sin(ang)


def _rope_tables(n_tokens):
    rows = n_tokens // GRID_W
    row = jnp.repeat(jnp.arange(rows, dtype=F32), GRID_W)
    col = jnp.tile(jnp.arange(GRID_W, dtype=F32), rows)
    n_freq = HEAD_DIM // 4
    inv = ROPE_THETA ** (-jnp.arange(n_freq, dtype=F32) / n_freq)
    ang = jnp.concatenate([row[:, None] * inv[None], col[:, None] * inv[None]], axis=-1)
    cos = jnp.repeat(jnp.cos(ang), 2, axis=-1)
    sin = jnp.repeat(jnp.sin(ang), 2, axis=-1)
    sign = jnp.tile(jnp.asarray([-1.0, 1.0], F32), HEAD_DIM // 2)
    return jnp.tile(cos, (1, 2)), jnp.tile(sin * sign, (1, 2))


def _fourier_tables():
    r = FFT_RADIX
    c, s = _dft_cos_sin(r)
    w1 = np.block([[c, s], [-s, c]])
    n_pos = r * r
    idx = np.arange(r, dtype=np.int64)
    k = idx[:, None, None] + r * idx[None, :, None]
    ang = 2.0 * np.pi * ((k * idx[None, None, :]) % n_pos) / n_pos
    scale = (n_pos * FOURIER_HEAD_DIM) ** -0.5
    w2 = np.concatenate([np.cos(ang), np.sin(ang)], axis=2) * scale
    cc, sc = _dft_cos_sin(FOURIER_HEAD_DIM)
    wc = np.concatenate([cc, -sc], axis=1)
    f32 = lambda a: jnp.asarray(np.ascontiguousarray(a), F32)
    return f32(w1).astype(BF16), f32(w2).astype(BF16), f32(wc).astype(BF16)


def kernel(x, c, ctx, c_ctx, ada_w, ada_b, norm_mix_g, norm_ffn_g, mix_in_w, mix_out_w, attn_sink,
           pool_w, pool_scale, ffn_w1, ffn_w3, ffn_w2, final_g):
    B, S, _ = x.shape
    L = ctx.shape[1]
    tm_ffn = 2 * FFN_PART_ROWS
    tm_ctx = L

    cond = jnp.zeros((ADA_ROWS, D_MODEL), F32).at[:B].set(c).at[B].set(c_ctx)
    mods = _ada(cond, ada_w, ada_b)

    cosf, sinf = _rope_tables(S)
    cos_id = jnp.ones((L, LANES), F32)
    sin_id = jnp.zeros((L, LANES), F32)
    w1_dft, w2_dft, wc = _fourier_tables()
    cl, sl_ = _dft_cos_sin(L)
    w_ctx_dft = jnp.asarray(np.concatenate([cl, sl_], axis=1), F32).astype(BF16)

    ffn_w = (ffn_w1.astype(BF16), ffn_w3.astype(BF16), ffn_w2.astype(BF16))
    last_ctx_reader = max(range(0, DEPTH, 2))
    h, hc = x, ctx
    for layer in range(DEPTH):
        m = mods[layer]
        lat = [m[:B, k * D_MODEL:(k + 1) * D_MODEL].reshape(B, 1, D_MODEL) for k in range(6)]
        cm = [jnp.broadcast_to(m[B, k * D_MODEL:(k + 1) * D_MODEL].reshape(1, 1, D_MODEL), (B, 1, D_MODEL))
              for k in range(6)]
        update_ctx = layer < last_ctx_reader
        g_mix = norm_mix_g[layer].reshape(1, D_MODEL)
        g_ffn = norm_ffn_g[layer].reshape(1, D_MODEL)
        fin = final_g.reshape(1, D_MODEL) if layer == DEPTH - 1 else None
        j = layer // 2
        if layer % 2 == 0:
            w_in = mix_in_w[j].astype(BF16)
            w_out = mix_out_w[j].astype(BF16)
            sink = attn_sink[j]
            zr_c, zi_c, q_c, k_c, v_c = _in_proj(hc, g_mix, cm[0], cm[1], w_in, cos_id, sin_id, wc, tm_ctx)
            zr, zi, q, k, v = _in_proj(h, g_mix, lat[0], lat[1], w_in, cosf, sinf, wc, 4 * IN_PART_ROWS)
            four = _fourier_4096(zr, zi, w1_dft, w2_dft)
            attn = _attention(q, k, v, k_c, v_c, sink, True)
            h = _ffn(h, ("proj", four, attn, w_out, lat[2]), g_ffn, lat[3], lat[4], lat[5], ffn_w, layer, fin, tm_ffn)
            if update_ctx:
                four_c = _fourier_small(zr_c, zi_c, w_ctx_dft)
                attn_c = _attention(q_c, None, None, k_c, v_c, sink, False)
                hc = _ffn(hc, ("proj", four_c, attn_c, w_out, cm[2]), g_ffn, cm[3], cm[4], cm[5], ffn_w, layer,
                          None, tm_ctx)
        else:
            pw = pool_w[j].astype(BF16)
            psc = pool_scale[j].reshape(1, D_MODEL)
            h = _ffn(h, ("pool", g_mix, lat[0], lat[1], lat[2], pw, psc), g_ffn, lat[3], lat[4], lat[5],
                     ffn_w, layer, fin, tm_ffn)
            if update_ctx:
                hc = _ffn(hc, ("pool", g_mix, cm[0], cm[1], cm[2], pw, psc), g_ffn, cm[3], cm[4], cm[5],
                          ffn_w, layer, None, tm_ctx)
    return h
```

```python
import functools

import numpy as np
import jax
import jax.numpy as jnp
from jax import lax
from jax.experimental import pallas as pl
from jax.experimental.pallas import tpu as pltpu

F32 = jnp.float32
BF16 = jnp.bfloat16

D_MODEL = 1024
DEPTH = 4
GRID_W = 64
EPS = 1e-6
FOURIER_HEADS = 4
FOURIER_HEAD_DIM = 128
FOURIER_WIDTH = 512
HEAD_DIM = 64
N_Q_HEADS = 8
N_KV_HEADS = 2
GQA_GROUP = 4
ATTN_WIDTH = 512
KV_WIDTH = 128
Q_END = 1024
IN_WIDTH = 1280
BLOCK = 128
ATT_QBLOCKS = 4
ROPE_THETA = 10000.0
POOL_WINDOWS = (2, 4, 8, 16)
POOL_GROUP = 256
POOL_HALO = 8
POOL_PAD = 2 * POOL_HALO
FFN_HIDDEN = 2816
FFN_CHUNKS = ((0, 1024), (1024, 1024), (2048, 768))
FFN_PART_ROWS = 512
IN_PART_ROWS = 512
FFT_RADIX = 64
FFT_ROWS = 16
LANES = 128
LOG2E = 1.4426950408889634
ADA_ROWS = 16
ADA_CTX_ROW = 8
VMEM_LIMIT = 56 * 1024 * 1024


def _params(*sem):
    return pltpu.CompilerParams(dimension_semantics=sem, vmem_limit_bytes=VMEM_LIMIT)


def _rms(x, g):
    ms = jnp.mean(x * x, axis=-1, keepdims=True)
    return x * lax.rsqrt(ms + EPS) * g


def _rms_mod(x, g, shift, scale):
    return _rms(x, g) * (1.0 + scale) + shift


def _dot(a, b):
    return jnp.dot(a, b, preferred_element_type=F32)


def _ada_kernel(c_ref, w_ref, b_ref, o_ref):
    a = jax.nn.silu(c_ref[...])
    m = jnp.dot(a, w_ref[0], precision=lax.Precision.HIGHEST, preferred_element_type=F32) + b_ref[0]
    for r in range(ADA_ROWS):
        o_ref[0, 0, r] = m[r:r + 1]


def _ada(cond, ada_w, ada_b):
    n_chunks = ada_w.shape[-1] // D_MODEL
    return pl.pallas_call(
        _ada_kernel,
        grid=(DEPTH, n_chunks),
        in_specs=[
            pl.BlockSpec((ADA_ROWS, D_MODEL), lambda l, n: (0, 0)),
            pl.BlockSpec((1, D_MODEL, D_MODEL), lambda l, n: (l, 0, n)),
            pl.BlockSpec((1, 1, D_MODEL), lambda l, n: (l, 0, n)),
        ],
        out_specs=pl.BlockSpec((1, 1, ADA_ROWS, 1, D_MODEL), lambda l, n: (l, n, 0, 0, 0)),
        out_shape=jax.ShapeDtypeStruct((DEPTH, n_chunks, ADA_ROWS, 1, D_MODEL), F32),
        compiler_params=_params("arbitrary", "arbitrary"),
        name="ada_mod",
    )(cond, ada_w, ada_b.reshape(DEPTH, 1, ada_w.shape[-1]))


def _mod_operand(mods, layer, chunk, ctx):
    spec = pl.BlockSpec((1, 1, 1, 1, D_MODEL),
                        lambda b, i: (layer, chunk, ADA_CTX_ROW if ctx else b, 0, 0))
    return spec, mods


def _v(ref):
    return ref[0, 0, 0]


def _in_kernel(h_ref, g_ref, sh_ref, sc_ref, w_ref, cos_ref, sin_ref, wc_ref,
               zr_ref, zi_ref, q_ref, k_ref, v_ref, *, tm):
    n_parts = max(1, tm // IN_PART_ROWS)
    pr = tm // n_parts
    rows = [slice(p * pr, (p + 1) * pr) for p in range(n_parts)]
    projs = [_dot(_rms_mod(h_ref[0, r], g_ref[...], _v(sh_ref), _v(sc_ref)).astype(BF16), w_ref[...])
             for r in rows]
    wc = wc_ref[...]
    zs = [[_dot(proj[:, hh * LANES:(hh + 1) * LANES].astype(BF16), wc) for hh in range(FOURIER_HEADS)]
          for proj in projs]
    lane = lax.broadcasted_iota(jnp.int32, (pr, LANES), 1)
    even = (lane & 1) == 0
    low = lane < HEAD_DIM
    for r, proj, z4 in zip(rows, projs, zs):
        cosf = cos_ref[r]
        sinf = sin_ref[r]

        def rope(t):
            partner = jnp.where(even, pltpu.roll(t, LANES - 1, 1), pltpu.roll(t, 1, 1))
            return t * cosf + partner * sinf

        for hh, z in enumerate(z4):
            sl = slice(hh * LANES, (hh + 1) * LANES)
            zr_ref[0, r, sl] = z[:, :LANES].astype(BF16)
            zi_ref[0, r, sl] = z[:, LANES:].astype(BF16)
        for p in range(ATTN_WIDTH // LANES):
            sl = slice(p * LANES, (p + 1) * LANES)
            qq = proj[:, FOURIER_WIDTH + p * LANES:FOURIER_WIDTH + (p + 1) * LANES]
            q_ref[0, r, sl] = (rope(qq) * (HEAD_DIM ** -0.5 * LOG2E)).astype(BF16)
        kk = rope(proj[:, Q_END:Q_END + KV_WIDTH])
        vv = proj[:, Q_END + KV_WIDTH:]
        k_sw = pltpu.roll(kk, HEAD_DIM, 1)
        v_sw = pltpu.roll(vv, HEAD_DIM, 1)
        k_ref[0, 0, r] = jnp.where(low, kk, k_sw).astype(BF16)
        k_ref[0, 1, r] = jnp.where(low, k_sw, kk).astype(BF16)
        v_ref[0, 0, r] = jnp.where(low, vv, v_sw).astype(BF16)
        v_ref[0, 1, r] = jnp.where(low, v_sw, vv).astype(BF16)


def _in_proj(h, g, mods, layer, ctx, w_in, cosf, sinf, wc, tm):
    B, S, _ = h.shape
    (shift_spec, shift), (scale_spec, scale) = (_mod_operand(mods, layer, k, ctx) for k in (0, 1))
    tok = lambda w: pl.BlockSpec((1, tm, w), lambda b, i: (b, i, 0))
    kv = pl.BlockSpec((1, N_KV_HEADS, tm, LANES), lambda b, i: (b, 0, i, 0))
    return pl.pallas_call(
        functools.partial(_in_kernel, tm=tm),
        grid=(B, S // tm),
        in_specs=[
            tok(D_MODEL),
            pl.BlockSpec((1, D_MODEL), lambda b, i: (0, 0)),
            shift_spec, scale_spec,
            pl.BlockSpec((D_MODEL, IN_WIDTH), lambda b, i: (0, 0)),
            pl.BlockSpec((tm, LANES), lambda b, i: (i, 0)),
            pl.BlockSpec((tm, LANES), lambda b, i: (i, 0)),
            pl.BlockSpec((FOURIER_HEAD_DIM, 2 * FOURIER_HEAD_DIM), lambda b, i: (0, 0)),
        ],
        out_specs=[tok(FOURIER_WIDTH), tok(FOURIER_WIDTH), tok(ATTN_WIDTH), kv, kv],
        out_shape=[
            jax.ShapeDtypeStruct((B, S, FOURIER_WIDTH), BF16),
            jax.ShapeDtypeStruct((B, S, FOURIER_WIDTH), BF16),
            jax.ShapeDtypeStruct((B, S, ATTN_WIDTH), BF16),
            jax.ShapeDtypeStruct((B, N_KV_HEADS, S, LANES), BF16),
            jax.ShapeDtypeStruct((B, N_KV_HEADS, S, LANES), BF16),
        ],
        compiler_params=_params("arbitrary", "arbitrary"),
        name="in_proj",
    )(h, g, shift, scale, w_in, cosf, sinf, wc)


def _fft_kernel(zr_ref, zi_ref, w1_ref, w2_ref, o_ref, y_ref, *, t):
    r = FFT_RADIX
    w1 = w1_ref[...]
    for c in range(r // t):
        cs = slice(c * t, (c + 1) * t)
        xr = jnp.swapaxes(zr_ref[0, :, cs, :], 0, 1)
        xi = jnp.swapaxes(zi_ref[0, :, cs, :], 0, 1)
        outs_r, outs_i = [], []
        for j in range(t):
            y = _dot(w1, jnp.concatenate([xr[j], xi[j]], axis=0))
            outs_r.append(y[:r].astype(BF16))
            outs_i.append(y[r:].astype(BF16))
        y_ref[0, :, cs, :] = jnp.swapaxes(jnp.stack(outs_r, axis=0), 0, 1)
        y_ref[1, :, cs, :] = jnp.swapaxes(jnp.stack(outs_i, axis=0), 0, 1)
    for c in range(r // t):
        outs = []
        for j in range(t):
            k1 = c * t + j
            x = jnp.concatenate([y_ref[0, k1], y_ref[1, k1]], axis=0)
            outs.append(_dot(w2_ref[k1], x).astype(BF16))
        o_ref[0, :, c * t:(c + 1) * t, :] = jnp.swapaxes(jnp.stack(outs, axis=0), 0, 1)


def _fourier_4096(zr, zi, w1, w2):
    B, S, W = zr.shape
    r = FFT_RADIX
    zr4 = zr.reshape(B, r, r, W)
    zi4 = zi.reshape(B, r, r, W)
    blk = pl.BlockSpec((1, r, r, W), lambda b: (b, 0, 0, 0))
    out = pl.pallas_call(
        functools.partial(_fft_kernel, t=FFT_ROWS),
        grid=(B,),
        in_specs=[blk, blk, pl.BlockSpec((2 * r, 2 * r), lambda b: (0, 0)),
                  pl.BlockSpec((r, r, 2 * r), lambda b: (0, 0, 0))],
        out_specs=blk,
        out_shape=jax.ShapeDtypeStruct((B, r, r, W), BF16),
        scratch_shapes=[pltpu.VMEM((2, r, r, W), BF16)],
        compiler_params=_params("arbitrary"),
        name="fft_4096",
    )(zr4, zi4, w1, w2)
    return out.reshape(B, S, W)


def _dft_small_kernel(zr_ref, zi_ref, w_ref, o_ref, *, scale):
    x = jnp.concatenate([zr_ref[0], zi_ref[0]], axis=0)
    o_ref[0] = (_dot(w_ref[...], x) * scale).astype(BF16)


def _fourier_small(zr, zi, w):
    B, L, W = zr.shape
    scale = float((L * FOURIER_HEAD_DIM) ** -0.5)
    blk = pl.BlockSpec((1, L, W), lambda b: (b, 0, 0))
    return pl.pallas_call(
        functools.partial(_dft_small_kernel, scale=scale),
        grid=(B,),
        in_specs=[blk, blk, pl.BlockSpec((L, 2 * L), lambda b: (0, 0))],
        out_specs=blk,
        out_shape=jax.ShapeDtypeStruct((B, L, W), BF16),
        compiler_params=_params("arbitrary"),
        name="dft_ctx",
    )(zr, zi, w)


def _att_kernel(sink_ref, q_ref, *refs, n_steps, qb, window):
    if window:
        kp_ref, kc_ref, kn_ref, vp_ref, vc_ref, vn_ref, kx_ref, vx_ref, o_ref = refs
    else:
        kx_ref, vx_ref, o_ref = refs
    i = pl.program_id(1)
    lane = lax.broadcasted_iota(jnp.int32, (BLOCK, LANES), 1)
    low = lane < HEAD_DIM
    if window:
        row = lax.broadcasted_iota(jnp.int32, (BLOCK, BLOCK), 0)
        col = lax.broadcasted_iota(jnp.int32, (BLOCK, BLOCK), 1)
        ninf = jnp.float32(-jnp.inf)
        before = col >= row
        after = col <= row

    def window_tiles(p_ref, c_ref, n_ref, kh, a):
        cur = lambda j: c_ref[0, kh, j * BLOCK:(j + 1) * BLOCK]
        first = p_ref[0, kh] if a == 0 else cur(a - 1)
        last = n_ref[0, kh] if a == qb - 1 else cur(a + 1)
        return [first, cur(a), last]

    def scores(a, kh):
        if window:
            k = jnp.concatenate(window_tiles(kp_ref, kc_ref, kn_ref, kh, a) + [kx_ref[0, kh]], axis=0)
        else:
            k = kx_ref[0, kh]
        qs = []
        for g in range(GQA_GROUP):
            h = kh * GQA_GROUP + g
            qp = q_ref[0, a * BLOCK:(a + 1) * BLOCK, (h // 2) * LANES:(h // 2 + 1) * LANES]
            keep = low if h % 2 == 0 else jnp.logical_not(low)
            qs.append(jnp.where(keep, qp, jnp.zeros_like(qp)))
        qst = jnp.concatenate(qs, axis=0)
        return lax.dot_general(qst, k, (((1,), (1,)), ((), ())), preferred_element_type=F32)

    def finish(a, kh, s):
        if window:
            v = jnp.concatenate(window_tiles(vp_ref, vc_ref, vn_ref, kh, a) + [vx_ref[0, kh]], axis=0)
            ok_first = before & (i > 0) if a == 0 else before
            ok_last = after & (i < n_steps - 1) if a == qb - 1 else after
            b_first = jnp.where(ok_first, 0.0, ninf)
            b_last = jnp.where(ok_last, 0.0, ninf)
        else:
            v = vx_ref[0, kh]
        ps, ls = [], []
        for g in range(GQA_GROUP):
            sk = sink_ref[kh * GQA_GROUP + g] * LOG2E
            sg = s[g * BLOCK:(g + 1) * BLOCK]
            if window:
                sg = jnp.concatenate(
                    [sg[:, :BLOCK] + b_first, sg[:, BLOCK:2 * BLOCK],
                     sg[:, 2 * BLOCK:3 * BLOCK] + b_last, sg[:, 3 * BLOCK:]], axis=1)
            m = jnp.maximum(jnp.max(sg, axis=-1, keepdims=True), sk)
            p = jnp.exp2(sg - m)
            ls.append(jnp.sum(p, axis=-1, keepdims=True) + jnp.exp2(sk - m))
            ps.append(p.astype(BF16))
        o = _dot(jnp.concatenate(ps, axis=0), v)
        og = [o[g * BLOCK:(g + 1) * BLOCK] / ls[g] for g in range(GQA_GROUP)]
        for pr in range(GQA_GROUP // 2):
            c0 = kh * GQA_GROUP * HEAD_DIM + pr * LANES
            o_ref[0, a * BLOCK:(a + 1) * BLOCK, c0:c0 + LANES] = jnp.where(
                low, og[2 * pr], og[2 * pr + 1]).astype(BF16)

    units = [(a, kh) for a in range(qb) for kh in range(N_KV_HEADS)]
    ahead = 2
    pending = [scores(*u) for u in units[:ahead]]
    for n, u in enumerate(units):
        s = pending.pop(0)
        if n + ahead < len(units):
            pending.append(scores(*units[n + ahead]))
        finish(*u, s)


def _attention(q, k, v, kx, vx, sink, window):
    B, S, _ = q.shape
    nb = S // BLOCK
    qb = min(ATT_QBLOCKS, nb)
    rows = qb * BLOCK
    n_steps = S // rows
    L = kx.shape[2]
    qspec = pl.BlockSpec((1, rows, ATTN_WIDTH), lambda b, i: (b, i, 0))
    xspec = pl.BlockSpec((1, N_KV_HEADS, L, LANES), lambda b, i: (b, 0, 0, 0))
    smem = pl.BlockSpec(memory_space=pltpu.SMEM)
    if window:
        blk = (1, N_KV_HEADS, BLOCK, LANES)
        prev = pl.BlockSpec(blk, lambda b, i: (b, 0, jnp.maximum(i * qb - 1, 0), 0))
        cur = pl.BlockSpec((1, N_KV_HEADS, rows, LANES), lambda b, i: (b, 0, i, 0))
        nxt = pl.BlockSpec(blk, lambda b, i: (b, 0, jnp.minimum((i + 1) * qb, nb - 1), 0))
        in_specs = [smem, qspec, prev, cur, nxt, prev, cur, nxt, xspec, xspec]
        args = (sink, q, k, k, k, v, v, v, kx, vx)
    else:
        in_specs = [smem, qspec, xspec, xspec]
        args = (sink, q, kx, vx)
    return pl.pallas_call(
        functools.partial(_att_kernel, n_steps=n_steps, qb=qb, window=window),
        grid=(B, n_steps),
        in_specs=in_specs,
        out_specs=qspec,
        out_shape=jax.ShapeDtypeStruct((B, S, ATTN_WIDTH), BF16),
        compiler_params=_params("arbitrary", "arbitrary"),
        name="window_attn" if window else "ctx_attn",
    )(*args)


def _pool_fill(xs_ref, h_ref, hp_ref, hx_ref, g, sh, sc, lo, hi, tm, first_tile, last_tile):
    hal = POOL_HALO
    if lo == 0:
        xs_ref[0:hal] = jnp.where(first_tile, 0.0, _rms_mod(hp_ref[0], g, sh, sc))
    xs_ref[hal + lo:hal + hi] = _rms_mod(h_ref[0, lo:hi], g, sh, sc)
    if hi == tm:
        xs_ref[hal + tm:2 * hal + tm] = jnp.where(last_tile, 0.0, _rms_mod(hx_ref[0], g, sh, sc))
        xs_ref[2 * hal + tm:] = jnp.zeros((POOL_PAD, D_MODEL), F32)


def _pool_part(xs_ref, p_ref, ic_ref, pw_ref, r0, pr):
    hal = POOL_HALO
    ys = []
    for gi, w in enumerate(POOL_WINDOWS):
        cs = slice(gi * POOL_GROUP, (gi + 1) * POOL_GROUP)
        x = lambda off, n: xs_ref[r0 + off:r0 + off + n, cs]
        if w == 2:
            acc = x(hal - 1, pr) + x(hal, pr)
        else:
            n2, n4, n8 = pr + 3 * hal, pr + 2 * hal, pr + hal
            p_ref[0:n2] = x(0, n2) + x(1, n2)
            if w == 4:
                acc = p_ref[hal - 2:hal - 2 + pr] + p_ref[hal:hal + pr]
            else:
                p_ref[0:n4] = p_ref[0:n4] + p_ref[2:n4 + 2]
                if w == 8:
                    acc = p_ref[hal - 4:hal - 4 + pr] + p_ref[hal:hal + pr]
                else:
                    p_ref[0:n8] = p_ref[0:n8] + p_ref[4:n8 + 4]
                    acc = p_ref[0:pr] + p_ref[hal:hal + pr]
        yg = acc * ic_ref[r0:r0 + pr, gi:gi + 1] - x(hal, pr)
        ys.append(_dot(yg.astype(BF16), pw_ref[gi]))
    return jnp.concatenate(ys, axis=1)


def _ffn_kernel(*refs, tm, seq, mode, final):
    refs = list(refs)
    take = lambda n: [refs.pop(0) for _ in range(n)]
    (h_ref,) = take(1)
    if mode == "proj":
        four_ref, attn_ref, wo_ref, g1_ref = take(4)
    elif mode == "pool":
        hp_ref, hx_ref, gm_ref, sh1_ref, sc1_ref, g1_ref, pw_ref, ps_ref, ic_ref = take(9)
    gn_ref, sh_ref, sc_ref, g2_ref, w1_ref, w3_ref, w2_ref = take(7)
    if final:
        (fg_ref,) = take(1)
    (o_ref,) = take(1)
    if mode == "pool":
        xs_ref, p_ref = take(2)
        first_tile = pl.program_id(1) == 0
        last_tile = pl.program_id(1) == seq // tm - 1
    n_parts = max(1, tm // FFN_PART_ROWS)
    pr = tm // n_parts
    rows = [slice(p * pr, (p + 1) * pr) for p in range(n_parts)]

    def mixed(p):
        h = h_ref[0, rows[p]]
        if mode == "proj":
            y = (_dot(four_ref[0, rows[p]], wo_ref[:FOURIER_WIDTH])
                 + _dot(attn_ref[0, rows[p]], wo_ref[FOURIER_WIDTH:]))
            return h + _v(g1_ref) * y
        if mode == "pool":
            lo = 0 if p == 0 else p * pr + POOL_PAD
            hi = tm if p == n_parts - 1 else (p + 1) * pr + POOL_PAD
            _pool_fill(xs_ref, h_ref, hp_ref, hx_ref, gm_ref[...], _v(sh1_ref), _v(sc1_ref), lo, hi, tm,
                       first_tile, last_tile)
            y = _pool_part(xs_ref, p_ref, ic_ref, pw_ref, p * pr, pr) * ps_ref[...]
            return h + _v(g1_ref) * y
        return h

    def normed(h):
        return _rms_mod(h, gn_ref[...], _v(sh_ref), _v(sc_ref)).astype(BF16)

    def up(hn, c0, cn):
        return _dot(hn, w1_ref[0, :, c0:c0 + cn]), _dot(hn, w3_ref[0, :, c0:c0 + cn])

    c0, cn = FFN_CHUNKS[0]
    if mode == "pool":
        hs, hns, ab = [], [], []
        for p in range(n_parts):
            hs.append(mixed(p))
            hns.append(normed(hs[p]))
            ab.append(up(hns[p], c0, cn))
    else:
        hs = [mixed(p) for p in range(n_parts)]
        hns = [normed(h) for h in hs]
        ab = [up(hn, c0, cn) for hn in hns]
    accs = [None] * n_parts
    for ci, (c0, cn) in enumerate(FFN_CHUNKS):
        if ci > 0:
            ab = [up(hn, c0, cn) for hn in hns]
        for p, (a, b) in enumerate(ab):
            t = (jax.nn.silu(a) * b).astype(BF16)
            d = _dot(t, w2_ref[0, c0:c0 + cn, :])
            accs[p] = d if accs[p] is None else accs[p] + d
    for p, r in enumerate(rows):
        out = hs[p] + _v(g2_ref) * accs[p]
        if final:
            out = _rms(out, fg_ref[...])
        o_ref[0, r] = out


def _ffn(h, mix, gn, mods, layer, ctx, ffn_w, final_g, tm):
    B, S, _ = h.shape
    mode = "none" if mix is None else mix[0]
    vecs = [_mod_operand(mods, layer, k, ctx) for k in range(6)]
    vspec = [v[0] for v in vecs]
    row = pl.BlockSpec((1, D_MODEL), lambda b, i: (0, 0))
    tok = lambda w: pl.BlockSpec((1, tm, w), lambda b, i: (b, i, 0))
    whole = pl.BlockSpec(memory_space=pltpu.VMEM)
    in_specs = [tok(D_MODEL)]
    args = [h]
    scratch = []
    if mode == "proj":
        _, four, attn, w_out = mix
        in_specs += [tok(FOURIER_WIDTH), tok(ATTN_WIDTH), whole, vspec[2]]
        args += [four, attn, w_out, mods]
    elif mode == "pool":
        _, g_mix, pool_w, pool_scale = mix
        hb = tm // POOL_HALO
        halo = (1, POOL_HALO, D_MODEL)
        t = np.arange(S)
        inv_count = jnp.asarray(np.stack(
            [1.0 / (np.minimum(t + w // 2, S) - np.maximum(t - w // 2, 0)) for w in POOL_WINDOWS], axis=1), F32)
        in_specs += [
            pl.BlockSpec(halo, lambda b, i: (b, jnp.maximum(i * hb - 1, 0), 0)),
            pl.BlockSpec(halo, lambda b, i: (b, jnp.minimum((i + 1) * hb, S // POOL_HALO - 1), 0)),
            row, vspec[0], vspec[1], vspec[2],
            pl.BlockSpec((len(POOL_WINDOWS), POOL_GROUP, POOL_GROUP), lambda b, i: (0, 0, 0)),
            row,
            pl.BlockSpec((tm, len(POOL_WINDOWS)), lambda b, i: (i, 0)),
        ]
        args += [h, h, g_mix, mods, mods, mods, pool_w, pool_scale, inv_count]
        pr = tm // max(1, tm // FFN_PART_ROWS)
        scratch = [pltpu.VMEM((tm + 2 * POOL_HALO + POOL_PAD, D_MODEL), F32),
                   pltpu.VMEM((pr + 3 * POOL_HALO, POOL_GROUP), F32)]
    in_specs += [row, vspec[3], vspec[4], vspec[5]] + [
        pl.BlockSpec((1,) + w.shape[1:], lambda b, i: (layer, 0, 0), pipeline_mode=pl.Buffered(1)) for w in ffn_w]
    args += [gn, mods, mods, mods, *ffn_w]
    if final_g is not None:
        in_specs.append(row)
        args.append(final_g)
    return pl.pallas_call(
        functools.partial(_ffn_kernel, tm=tm, seq=S, mode=mode, final=final_g is not None),
        grid=(B, S // tm),
        in_specs=in_specs,
        out_specs=tok(D_MODEL),
        out_shape=jax.ShapeDtypeStruct((B, S, D_MODEL), F32),
        scratch_shapes=scratch,
        compiler_params=_params("arbitrary", "arbitrary"),
        name="ffn_" + mode,
    )(*args)


def _dft_cos_sin(n):
    idx = np.arange(n, dtype=np.int64)
    ang = 2.0 * np.pi * ((idx[:, None] * idx[None, :]) % n) / n
    return np.cos(ang), np.sin(ang)


def _rope_tables(n_tokens):
    rows = n_tokens // GRID_W
    row = jnp.repeat(jnp.arange(rows, dtype=F32), GRID_W)
    col = jnp.tile(jnp.arange(GRID_W, dtype=F32), rows)
    n_freq = HEAD_DIM // 4
    inv = ROPE_THETA ** (-jnp.arange(n_freq, dtype=F32) / n_freq)
    ang = jnp.concatenate([row[:, None] * inv[None], col[:, None] * inv[None]], axis=-1)
    cos = jnp.repeat(jnp.cos(ang), 2, axis=-1)
    sin = jnp.repeat(jnp.sin(ang), 2, axis=-1)
    sign = jnp.tile(jnp.asarray([-1.0, 1.0], F32), HEAD_DIM // 2)
    return jnp.tile(cos, (1, 2)), jnp.tile(sin * sign, (1, 2))


def _fourier_tables():
    r = FFT_RADIX
    c, s = _dft_cos_sin(r)
    w1 = np.block([[c, s], [-s, c]])
    n_pos = r * r
    idx = np.arange(r, dtype=np.int64)
    k = idx[:, None, None] + r * idx[None, :, None]
    ang = 2.0 * np.pi * ((k * idx[None, None, :]) % n_pos) / n_pos
    scale = (n_pos * FOURIER_HEAD_DIM) ** -0.5
    w2 = np.concatenate([np.cos(ang), np.sin(ang)], axis=2) * scale
    cc, sc = _dft_cos_sin(FOURIER_HEAD_DIM)
    wc = np.concatenate([cc, -sc], axis=1)
    f32 = lambda a: jnp.asarray(np.ascontiguousarray(a), F32)
    return f32(w1).astype(BF16), f32(w2).astype(BF16), f32(wc).astype(BF16)


def kernel(x, c, ctx, c_ctx, ada_w, ada_b, norm_mix_g, norm_ffn_g, mix_in_w, mix_out_w, attn_sink,
           pool_w, pool_scale, ffn_w1, ffn_w3, ffn_w2, final_g):
    B, S, _ = x.shape
    L = ctx.shape[1]
    tm_ffn = 2 * FFN_PART_ROWS
    tm_ctx = L

    cond = jnp.zeros((ADA_ROWS, D_MODEL), F32).at[:B].set(c).at[B].set(c_ctx)
    mods = _ada(cond, ada_w, ada_b)

    cosf, sinf = _rope_tables(S)
    cos_id = jnp.ones((L, LANES), F32)
    sin_id = jnp.zeros((L, LANES), F32)
    w1_dft, w2_dft, wc = _fourier_tables()
    cl, sl_ = _dft_cos_sin(L)
    w_ctx_dft = jnp.asarray(np.concatenate([cl, sl_], axis=1), F32).astype(BF16)

    ffn_w = (ffn_w1.astype(BF16), ffn_w3.astype(BF16), ffn_w2.astype(BF16))
    last_ctx_reader = max(range(0, DEPTH, 2))
    h, hc = x, ctx
    for layer in range(DEPTH):
        update_ctx = layer < last_ctx_reader
        g_mix = norm_mix_g[layer].reshape(1, D_MODEL)
        g_ffn = norm_ffn_g[layer].reshape(1, D_MODEL)
        fin = final_g.reshape(1, D_MODEL) if layer == DEPTH - 1 else None
        j = layer // 2
        if layer % 2 == 0:
            w_in = mix_in_w[j].astype(BF16)
            w_out = mix_out_w[j].astype(BF16)
            sink = attn_sink[j]
            zr_c, zi_c, q_c, k_c, v_c = _in_proj(hc, g_mix, mods, layer, True, w_in, cos_id, sin_id, wc, tm_ctx)
            zr, zi, q, k, v = _in_proj(h, g_mix, mods, layer, False, w_in, cosf, sinf, wc, 4 * IN_PART_ROWS)
            four = _fourier_4096(zr, zi, w1_dft, w2_dft)
            attn = _attention(q, k, v, k_c, v_c, sink, True)
            h = _ffn(h, ("proj", four, attn, w_out), g_ffn, mods, layer, False, ffn_w, fin, tm_ffn)
            if update_ctx:
                four_c = _fourier_small(zr_c, zi_c, w_ctx_dft)
                attn_c = _attention(q_c, None, None, k_c, v_c, sink, False)
                hc = _ffn(hc, ("proj", four_c, attn_c, w_out), g_ffn, mods, layer, True, ffn_w, None, tm_ctx)
        else:
            pw = pool_w[j].astype(BF16)
            psc = pool_scale[j].reshape(1, D_MODEL)
            h = _ffn(h, ("pool", g_mix, pw, psc), g_ffn, mods, layer, False, ffn_w, fin, tm_ffn)
            if update_ctx:
                hc = _ffn(hc, ("pool", g_mix, pw, psc), g_ffn, mods, layer, True, ffn_w, None, tm_ctx)
    return h
```

```python
import functools

import numpy as np
import jax
import jax.numpy as jnp
from jax import lax
from jax.experimental import pallas as pl
from jax.experimental.pallas import tpu as pltpu

F32 = jnp.float32
BF16 = jnp.bfloat16

D_MODEL = 1024
DEPTH = 4
GRID_W = 64
EPS = 1e-6
FOURIER_HEADS = 4
FOURIER_HEAD_DIM = 128
FOURIER_WIDTH = 512
HEAD_DIM = 64
N_Q_HEADS = 8
N_KV_HEADS = 2
GQA_GROUP = 4
ATTN_WIDTH = 512
KV_WIDTH = 128
Q_END = 1024
IN_WIDTH = 1280
BLOCK = 128
ATT_QBLOCKS = 4
ROPE_THETA = 10000.0
POOL_WINDOWS = (2, 4, 8, 16)
POOL_GROUP = 256
POOL_HALO = 8
POOL_PAD = 2 * POOL_HALO
FFN_HIDDEN = 2816
FFN_CHUNKS = ((0, 1024), (1024, 1024), (2048, 768))
FFN_PART_ROWS = 512
IN_PART_ROWS = 512
FFT_RADIX = 64
FFT_ROWS = 16
LANES = 128
BF16_ROWS = 16
LOG2E = 1.4426950408889634
ADA_ROWS = 16
ADA_CTX_ROW = 8
VMEM_LIMIT = 56 * 1024 * 1024


def _params(*sem):
    return pltpu.CompilerParams(dimension_semantics=sem, vmem_limit_bytes=VMEM_LIMIT)


def _rms(x, g):
    ms = jnp.mean(x * x, axis=-1, keepdims=True)
    return x * lax.rsqrt(ms + EPS) * g


def _rms_mod(x, g, shift, scale):
    return _rms(x, g) * (1.0 + scale) + shift


def _dot(a, b):
    return jnp.dot(a, b, preferred_element_type=F32)


def _ada_kernel(c_ref, w_ref, b_ref, o_ref):
    a = jax.nn.silu(c_ref[...])
    m = jnp.dot(a, w_ref[0], precision=lax.Precision.HIGHEST, preferred_element_type=F32) + b_ref[0]
    for r in range(ADA_ROWS):
        o_ref[0, 0, r] = m[r:r + 1]


def _ada(cond, ada_w, ada_b):
    n_chunks = ada_w.shape[-1] // D_MODEL
    return pl.pallas_call(
        _ada_kernel,
        grid=(DEPTH, n_chunks),
        in_specs=[
            pl.BlockSpec((ADA_ROWS, D_MODEL), lambda l, n: (0, 0)),
            pl.BlockSpec((1, D_MODEL, D_MODEL), lambda l, n: (l, 0, n)),
            pl.BlockSpec((1, 1, D_MODEL), lambda l, n: (l, 0, n)),
        ],
        out_specs=pl.BlockSpec((1, 1, ADA_ROWS, 1, D_MODEL), lambda l, n: (l, n, 0, 0, 0)),
        out_shape=jax.ShapeDtypeStruct((DEPTH, n_chunks, ADA_ROWS, 1, D_MODEL), F32),
        compiler_params=_params("arbitrary", "arbitrary"),
        name="ada_mod",
    )(cond, ada_w, ada_b.reshape(DEPTH, 1, ada_w.shape[-1]))


def _mod_operand(mods, layer, chunk, ctx):
    spec = pl.BlockSpec((1, 1, 1, 1, D_MODEL),
                        lambda b, i: (layer, chunk, ADA_CTX_ROW if ctx else b, 0, 0))
    return spec, mods


def _v(ref):
    return ref[0, 0, 0]


def _in_kernel(h_ref, g_ref, sh_ref, sc_ref, w_ref, cos_ref, sin_ref, wc_ref,
               zr_ref, zi_ref, q_ref, k_ref, v_ref, *, tm):
    n_parts = max(1, tm // IN_PART_ROWS)
    pr = tm // n_parts
    rows = [slice(p * pr, (p + 1) * pr) for p in range(n_parts)]
    projs = [_dot(_rms_mod(h_ref[0, r], g_ref[...], _v(sh_ref), _v(sc_ref)).astype(BF16), w_ref[...])
             for r in rows]
    wc = wc_ref[...]
    zs = [[_dot(proj[:, hh * LANES:(hh + 1) * LANES].astype(BF16), wc) for hh in range(FOURIER_HEADS)]
          for proj in projs]
    lane = lax.broadcasted_iota(jnp.int32, (pr, LANES), 1)
    even = (lane & 1) == 0
    low = lane < HEAD_DIM
    for r, proj, z4 in zip(rows, projs, zs):
        cosf = cos_ref[r]
        sinf = sin_ref[r]

        def rope(t):
            partner = jnp.where(even, pltpu.roll(t, LANES - 1, 1), pltpu.roll(t, 1, 1))
            return t * cosf + partner * sinf

        for hh, z in enumerate(z4):
            sl = slice(hh * LANES, (hh + 1) * LANES)
            zr_ref[0, r, sl] = z[:, :LANES].astype(BF16)
            zi_ref[0, r, sl] = z[:, LANES:].astype(BF16)
        for p in range(ATTN_WIDTH // LANES):
            sl = slice(p * LANES, (p + 1) * LANES)
            qq = proj[:, FOURIER_WIDTH + p * LANES:FOURIER_WIDTH + (p + 1) * LANES]
            q_ref[0, r, sl] = (rope(qq) * (HEAD_DIM ** -0.5 * LOG2E)).astype(BF16)
        kk = rope(proj[:, Q_END:Q_END + KV_WIDTH])
        vv = proj[:, Q_END + KV_WIDTH:]
        k_sw = pltpu.roll(kk, HEAD_DIM, 1)
        v_sw = pltpu.roll(vv, HEAD_DIM, 1)
        k_ref[0, 0, r] = jnp.where(low, kk, k_sw).astype(BF16)
        k_ref[0, 1, r] = jnp.where(low, k_sw, kk).astype(BF16)
        v_ref[0, 0, r] = jnp.where(low, vv, v_sw).astype(BF16)
        v_ref[0, 1, r] = jnp.where(low, v_sw, vv).astype(BF16)


def _in_proj(h, g, mods, layer, ctx, w_in, cosf, sinf, wc, tm):
    B, S, _ = h.shape
    (shift_spec, shift), (scale_spec, scale) = (_mod_operand(mods, layer, k, ctx) for k in (0, 1))
    tok = lambda w: pl.BlockSpec((1, tm, w), lambda b, i: (b, i, 0))
    kv = pl.BlockSpec((1, N_KV_HEADS, tm, LANES), lambda b, i: (b, 0, i, 0))
    return pl.pallas_call(
        functools.partial(_in_kernel, tm=tm),
        grid=(B, S // tm),
        in_specs=[
            tok(D_MODEL),
            pl.BlockSpec((1, D_MODEL), lambda b, i: (0, 0)),
            shift_spec, scale_spec,
            pl.BlockSpec((D_MODEL, IN_WIDTH), lambda b, i: (0, 0)),
            pl.BlockSpec((tm, LANES), lambda b, i: (i, 0)),
            pl.BlockSpec((tm, LANES), lambda b, i: (i, 0)),
            pl.BlockSpec((FOURIER_HEAD_DIM, 2 * FOURIER_HEAD_DIM), lambda b, i: (0, 0)),
        ],
        out_specs=[tok(FOURIER_WIDTH), tok(FOURIER_WIDTH), tok(ATTN_WIDTH), kv, kv],
        out_shape=[
            jax.ShapeDtypeStruct((B, S, FOURIER_WIDTH), BF16),
            jax.ShapeDtypeStruct((B, S, FOURIER_WIDTH), BF16),
            jax.ShapeDtypeStruct((B, S, ATTN_WIDTH), BF16),
            jax.ShapeDtypeStruct((B, N_KV_HEADS, S, LANES), BF16),
            jax.ShapeDtypeStruct((B, N_KV_HEADS, S, LANES), BF16),
        ],
        compiler_params=_params("arbitrary", "arbitrary"),
        name="in_proj",
    )(h, g, shift, scale, w_in, cosf, sinf, wc)


def _fft_kernel(zr_ref, zi_ref, w1_ref, w2_ref, o_ref, y_ref, *, t):
    r = FFT_RADIX
    w1 = w1_ref[...]
    for c in range(r // t):
        cs = slice(c * t, (c + 1) * t)
        xr = jnp.swapaxes(zr_ref[0, :, cs, :], 0, 1)
        xi = jnp.swapaxes(zi_ref[0, :, cs, :], 0, 1)
        outs_r, outs_i = [], []
        for j in range(t):
            y = _dot(w1, jnp.concatenate([xr[j], xi[j]], axis=0))
            outs_r.append(y[:r].astype(BF16))
            outs_i.append(y[r:].astype(BF16))
        y_ref[0, :, cs, :] = jnp.swapaxes(jnp.stack(outs_r, axis=0), 0, 1)
        y_ref[1, :, cs, :] = jnp.swapaxes(jnp.stack(outs_i, axis=0), 0, 1)
    for c in range(r // t):
        outs = []
        for j in range(t):
            k1 = c * t + j
            x = jnp.concatenate([y_ref[0, k1], y_ref[1, k1]], axis=0)
            outs.append(_dot(w2_ref[k1], x).astype(BF16))
        o_ref[0, :, c * t:(c + 1) * t, :] = jnp.swapaxes(jnp.stack(outs, axis=0), 0, 1)


def _fourier_4096(zr, zi, w1, w2):
    B, S, W = zr.shape
    r = FFT_RADIX
    zr4 = zr.reshape(B, r, r, W)
    zi4 = zi.reshape(B, r, r, W)
    blk = pl.BlockSpec((1, r, r, W), lambda b: (b, 0, 0, 0))
    out = pl.pallas_call(
        functools.partial(_fft_kernel, t=FFT_ROWS),
        grid=(B,),
        in_specs=[blk, blk, pl.BlockSpec((2 * r, 2 * r), lambda b: (0, 0)),
                  pl.BlockSpec((r, r, 2 * r), lambda b: (0, 0, 0))],
        out_specs=blk,
        out_shape=jax.ShapeDtypeStruct((B, r, r, W), BF16),
        scratch_shapes=[pltpu.VMEM((2, r, r, W), BF16)],
        compiler_params=_params("arbitrary"),
        name="fft_4096",
    )(zr4, zi4, w1, w2)
    return out.reshape(B, S, W)


def _dft_small_kernel(zr_ref, zi_ref, w_ref, o_ref, *, scale):
    x = jnp.concatenate([zr_ref[0], zi_ref[0]], axis=0)
    o_ref[0] = (_dot(w_ref[...], x) * scale).astype(BF16)


def _fourier_small(zr, zi, w):
    B, L, W = zr.shape
    scale = float((L * FOURIER_HEAD_DIM) ** -0.5)
    blk = pl.BlockSpec((1, L, W), lambda b: (b, 0, 0))
    return pl.pallas_call(
        functools.partial(_dft_small_kernel, scale=scale),
        grid=(B,),
        in_specs=[blk, blk, pl.BlockSpec((L, 2 * L), lambda b: (0, 0))],
        out_specs=blk,
        out_shape=jax.ShapeDtypeStruct((B, L, W), BF16),
        compiler_params=_params("arbitrary"),
        name="dft_ctx",
    )(zr, zi, w)


def _att_kernel(sink_ref, q_ref, *refs, n_steps, qb, window, n_cast):
    n_kv = 8 if window else 2
    kv_refs, cast_in = refs[:n_kv], refs[n_kv:n_kv + n_cast]
    o_ref, cast_out = refs[n_kv + n_cast], refs[n_kv + n_cast + 1:]
    if window:
        kp_ref, kc_ref, kn_ref, vp_ref, vc_ref, vn_ref, kx_ref, vx_ref = kv_refs
    else:
        kx_ref, vx_ref = kv_refs
    for src, dst in zip(cast_in, cast_out):
        dst[...] = src[...].astype(BF16)
    i = pl.program_id(1)
    lane = lax.broadcasted_iota(jnp.int32, (BLOCK, LANES), 1)
    low = lane < HEAD_DIM
    if window:
        row = lax.broadcasted_iota(jnp.int32, (BLOCK, BLOCK), 0)
        col = lax.broadcasted_iota(jnp.int32, (BLOCK, BLOCK), 1)
        ninf = jnp.float32(-jnp.inf)
        before = col >= row
        after = col <= row

    def window_tiles(p_ref, c_ref, n_ref, kh, a):
        cur = lambda j: c_ref[0, kh, j * BLOCK:(j + 1) * BLOCK]
        first = p_ref[0, kh] if a == 0 else cur(a - 1)
        last = n_ref[0, kh] if a == qb - 1 else cur(a + 1)
        return [first, cur(a), last]

    def scores(a, kh):
        if window:
            k = jnp.concatenate(window_tiles(kp_ref, kc_ref, kn_ref, kh, a) + [kx_ref[0, kh]], axis=0)
        else:
            k = kx_ref[0, kh]
        qs = []
        for g in range(GQA_GROUP):
            h = kh * GQA_GROUP + g
            qp = q_ref[0, a * BLOCK:(a + 1) * BLOCK, (h // 2) * LANES:(h // 2 + 1) * LANES]
            keep = low if h % 2 == 0 else jnp.logical_not(low)
            qs.append(jnp.where(keep, qp, jnp.zeros_like(qp)))
        qst = jnp.concatenate(qs, axis=0)
        return lax.dot_general(qst, k, (((1,), (1,)), ((), ())), preferred_element_type=F32)

    def finish(a, kh, s):
        if window:
            v = jnp.concatenate(window_tiles(vp_ref, vc_ref, vn_ref, kh, a) + [vx_ref[0, kh]], axis=0)
            ok_first = before & (i > 0) if a == 0 else before
            ok_last = after & (i < n_steps - 1) if a == qb - 1 else after
            b_first = jnp.where(ok_first, 0.0, ninf)
            b_last = jnp.where(ok_last, 0.0, ninf)
        else:
            v = vx_ref[0, kh]
        ps, ls = [], []
        for g in range(GQA_GROUP):
            sk = sink_ref[kh * GQA_GROUP + g] * LOG2E
            sg = s[g * BLOCK:(g + 1) * BLOCK]
            if window:
                sg = jnp.concatenate(
                    [sg[:, :BLOCK] + b_first, sg[:, BLOCK:2 * BLOCK],
                     sg[:, 2 * BLOCK:3 * BLOCK] + b_last, sg[:, 3 * BLOCK:]], axis=1)
            m = jnp.maximum(jnp.max(sg, axis=-1, keepdims=True), sk)
            p = jnp.exp2(sg - m)
            ls.append(jnp.sum(p, axis=-1, keepdims=True) + jnp.exp2(sk - m))
            ps.append(p.astype(BF16))
        o = _dot(jnp.concatenate(ps, axis=0), v)
        og = [o[g * BLOCK:(g + 1) * BLOCK] / ls[g] for g in range(GQA_GROUP)]
        for pr in range(GQA_GROUP // 2):
            c0 = kh * GQA_GROUP * HEAD_DIM + pr * LANES
            o_ref[0, a * BLOCK:(a + 1) * BLOCK, c0:c0 + LANES] = jnp.where(
                low, og[2 * pr], og[2 * pr + 1]).astype(BF16)

    units = [(a, kh) for a in range(qb) for kh in range(N_KV_HEADS)]
    ahead = 2
    pending = [scores(*u) for u in units[:ahead]]
    for n, u in enumerate(units):
        s = pending.pop(0)
        if n + ahead < len(units):
            pending.append(scores(*units[n + ahead]))
        finish(*u, s)


def _cast_operands(cast, grid_steps, flat_step):
    in_specs, out_specs, out_shapes = [], [], []
    for w, first, n in cast:
        per, every = n // grid_steps, 1
        while per % BF16_ROWS:
            per, every = 2 * per, 2 * every
        assert n % per == 0 and first % per == 0
        blk = (per, w.shape[1])
        in_specs.append(pl.BlockSpec(blk, lambda b, i, e=every, o=first // per: (o + flat_step(b, i) // e, 0)))
        out_specs.append(pl.BlockSpec(blk, lambda b, i, e=every: (flat_step(b, i) // e, 0)))
        out_shapes.append(jax.ShapeDtypeStruct((n, w.shape[1]), BF16))
    return in_specs, out_specs, out_shapes


def _attention(q, k, v, kx, vx, sink, window, cast=()):
    B, S, _ = q.shape
    nb = S // BLOCK
    qb = min(ATT_QBLOCKS, nb)
    rows = qb * BLOCK
    n_steps = S // rows
    L = kx.shape[2]
    cast_in, cast_out, cast_shapes = _cast_operands(cast, B * n_steps, lambda b, i: b * n_steps + i)
    qspec = pl.BlockSpec((1, rows, ATTN_WIDTH), lambda b, i: (b, i, 0))
    xspec = pl.BlockSpec((1, N_KV_HEADS, L, LANES), lambda b, i: (b, 0, 0, 0))
    smem = pl.BlockSpec(memory_space=pltpu.SMEM)
    if window:
        blk = (1, N_KV_HEADS, BLOCK, LANES)
        prev = pl.BlockSpec(blk, lambda b, i: (b, 0, jnp.maximum(i * qb - 1, 0), 0))
        cur = pl.BlockSpec((1, N_KV_HEADS, rows, LANES), lambda b, i: (b, 0, i, 0))
        nxt = pl.BlockSpec(blk, lambda b, i: (b, 0, jnp.minimum((i + 1) * qb, nb - 1), 0))
        in_specs = [smem, qspec, prev, cur, nxt, prev, cur, nxt, xspec, xspec]
        args = (sink, q, k, k, k, v, v, v, kx, vx)
    else:
        in_specs = [smem, qspec, xspec, xspec]
        args = (sink, q, kx, vx)
    out, *casted = pl.pallas_call(
        functools.partial(_att_kernel, n_steps=n_steps, qb=qb, window=window, n_cast=len(cast)),
        grid=(B, n_steps),
        in_specs=in_specs + cast_in,
        out_specs=[qspec] + cast_out,
        out_shape=[jax.ShapeDtypeStruct((B, S, ATTN_WIDTH), BF16)] + cast_shapes,
        compiler_params=_params("arbitrary", "arbitrary"),
        name="window_attn" if window else "ctx_attn",
    )(*args, *(w for w, _, _ in cast))
    return (out, casted) if cast else out


def _pool_fill(xs_ref, h_ref, hp_ref, hx_ref, g, sh, sc, lo, hi, tm, first_tile, last_tile):
    hal = POOL_HALO
    if lo == 0:
        xs_ref[0:hal] = jnp.where(first_tile, 0.0, _rms_mod(hp_ref[0], g, sh, sc))
    xs_ref[hal + lo:hal + hi] = _rms_mod(h_ref[0, lo:hi], g, sh, sc)
    if hi == tm:
        xs_ref[hal + tm:2 * hal + tm] = jnp.where(last_tile, 0.0, _rms_mod(hx_ref[0], g, sh, sc))
        xs_ref[2 * hal + tm:] = jnp.zeros((POOL_PAD, D_MODEL), F32)


def _pool_part(xs_ref, p_ref, ic_ref, pw_ref, r0, pr):
    hal = POOL_HALO
    ys = []
    for gi, w in enumerate(POOL_WINDOWS):
        cs = slice(gi * POOL_GROUP, (gi + 1) * POOL_GROUP)
        x = lambda off, n: xs_ref[r0 + off:r0 + off + n, cs]
        if w == 2:
            acc = x(hal - 1, pr) + x(hal, pr)
        else:
            n2, n4, n8 = pr + 3 * hal, pr + 2 * hal, pr + hal
            p_ref[0:n2] = x(0, n2) + x(1, n2)
            if w == 4:
                acc = p_ref[hal - 2:hal - 2 + pr] + p_ref[hal:hal + pr]
            else:
                p_ref[0:n4] = p_ref[0:n4] + p_ref[2:n4 + 2]
                if w == 8:
                    acc = p_ref[hal - 4:hal - 4 + pr] + p_ref[hal:hal + pr]
                else:
                    p_ref[0:n8] = p_ref[0:n8] + p_ref[4:n8 + 4]
                    acc = p_ref[0:pr] + p_ref[hal:hal + pr]
        yg = acc * ic_ref[r0:r0 + pr, gi:gi + 1] - x(hal, pr)
        ys.append(_dot(yg.astype(BF16), pw_ref[gi]))
    return jnp.concatenate(ys, axis=1)


def _ffn_kernel(*refs, tm, seq, mode, final):
    refs = list(refs)
    take = lambda n: [refs.pop(0) for _ in range(n)]
    (h_ref,) = take(1)
    if mode == "proj":
        four_ref, attn_ref, wo_ref, g1_ref = take(4)
    elif mode == "pool":
        hp_ref, hx_ref, gm_ref, sh1_ref, sc1_ref, g1_ref, pw_ref, ps_ref, ic_ref = take(9)
    gn_ref, sh_ref, sc_ref, g2_ref, w1_ref, w3_ref, w2_ref = take(7)
    if final:
        (fg_ref,) = take(1)
    (o_ref,) = take(1)
    if mode == "pool":
        xs_ref, p_ref = take(2)
        first_tile = pl.program_id(1) == 0
        last_tile = pl.program_id(1) == seq // tm - 1
    n_parts = max(1, tm // FFN_PART_ROWS)
    pr = tm // n_parts
    rows = [slice(p * pr, (p + 1) * pr) for p in range(n_parts)]

    def mixed(p):
        h = h_ref[0, rows[p]]
        if mode == "proj":
            y = (_dot(four_ref[0, rows[p]], wo_ref[:FOURIER_WIDTH])
                 + _dot(attn_ref[0, rows[p]], wo_ref[FOURIER_WIDTH:]))
            return h + _v(g1_ref) * y
        if mode == "pool":
            lo = 0 if p == 0 else p * pr + POOL_PAD
            hi = tm if p == n_parts - 1 else (p + 1) * pr + POOL_PAD
            _pool_fill(xs_ref, h_ref, hp_ref, hx_ref, gm_ref[...], _v(sh1_ref), _v(sc1_ref), lo, hi, tm,
                       first_tile, last_tile)
            y = _pool_part(xs_ref, p_ref, ic_ref, pw_ref, p * pr, pr) * ps_ref[...]
            return h + _v(g1_ref) * y
        return h

    def normed(h):
        return _rms_mod(h, gn_ref[...], _v(sh_ref), _v(sc_ref)).astype(BF16)

    def up(hn, c0, cn):
        return _dot(hn, w1_ref[0, :, c0:c0 + cn]), _dot(hn, w3_ref[0, :, c0:c0 + cn])

    c0, cn = FFN_CHUNKS[0]
    if mode == "pool":
        hs, hns, ab = [], [], []
        for p in range(n_parts):
            hs.append(mixed(p))
            hns.append(normed(hs[p]))
            ab.append(up(hns[p], c0, cn))
    else:
        hs = [mixed(p) for p in range(n_parts)]
        hns = [normed(h) for h in hs]
        ab = [up(hn, c0, cn) for hn in hns]
    accs = [None] * n_parts
    for ci, (c0, cn) in enumerate(FFN_CHUNKS):
        if ci > 0:
            ab = [up(hn, c0, cn) for hn in hns]
        for p, (a, b) in enumerate(ab):
            t = (jax.nn.silu(a) * b).astype(BF16)
            d = _dot(t, w2_ref[0, c0:c0 + cn, :])
            accs[p] = d if accs[p] is None else accs[p] + d
    for p, r in enumerate(rows):
        out = hs[p] + _v(g2_ref) * accs[p]
        if final:
            out = _rms(out, fg_ref[...])
        o_ref[0, r] = out


def _ffn(h, mix, gn, mods, layer, ctx, ffn_w, final_g, tm):
    B, S, _ = h.shape
    mode = "none" if mix is None else mix[0]
    vecs = [_mod_operand(mods, layer, k, ctx) for k in range(6)]
    vspec = [v[0] for v in vecs]
    row = pl.BlockSpec((1, D_MODEL), lambda b, i: (0, 0))
    tok = lambda w: pl.BlockSpec((1, tm, w), lambda b, i: (b, i, 0))
    whole = pl.BlockSpec(memory_space=pltpu.VMEM)
    in_specs = [tok(D_MODEL)]
    args = [h]
    scratch = []
    if mode == "proj":
        _, four, attn, w_out = mix
        in_specs += [tok(FOURIER_WIDTH), tok(ATTN_WIDTH), whole, vspec[2]]
        args += [four, attn, w_out, mods]
    elif mode == "pool":
        _, g_mix, pool_w, pool_scale = mix
        hb = tm // POOL_HALO
        halo = (1, POOL_HALO, D_MODEL)
        t = np.arange(S)
        inv_count = jnp.asarray(np.stack(
            [1.0 / (np.minimum(t + w // 2, S) - np.maximum(t - w // 2, 0)) for w in POOL_WINDOWS], axis=1), F32)
        in_specs += [
            pl.BlockSpec(halo, lambda b, i: (b, jnp.maximum(i * hb - 1, 0), 0)),
            pl.BlockSpec(halo, lambda b, i: (b, jnp.minimum((i + 1) * hb, S // POOL_HALO - 1), 0)),
            row, vspec[0], vspec[1], vspec[2],
            pl.BlockSpec((len(POOL_WINDOWS), POOL_GROUP, POOL_GROUP), lambda b, i: (0, 0, 0)),
            row,
            pl.BlockSpec((tm, len(POOL_WINDOWS)), lambda b, i: (i, 0)),
        ]
        args += [h, h, g_mix, mods, mods, mods, pool_w, pool_scale, inv_count]
        pr = tm // max(1, tm // FFN_PART_ROWS)
        scratch = [pltpu.VMEM((tm + 2 * POOL_HALO + POOL_PAD, D_MODEL), F32),
                   pltpu.VMEM((pr + 3 * POOL_HALO, POOL_GROUP), F32)]
    in_specs += [row, vspec[3], vspec[4], vspec[5]] + [
        pl.BlockSpec((1,) + w.shape[1:], lambda b, i, n=w.shape[0]: (layer % n, 0, 0),
                     pipeline_mode=pl.Buffered(1)) for w in ffn_w]
    args += [gn, mods, mods, mods, *ffn_w]
    if final_g is not None:
        in_specs.append(row)
        args.append(final_g)
    return pl.pallas_call(
        functools.partial(_ffn_kernel, tm=tm, seq=S, mode=mode, final=final_g is not None),
        grid=(B, S // tm),
        in_specs=in_specs,
        out_specs=tok(D_MODEL),
        out_shape=jax.ShapeDtypeStruct((B, S, D_MODEL), F32),
        scratch_shapes=scratch,
        compiler_params=_params("arbitrary", "arbitrary"),
        name="ffn_" + mode,
    )(*args)


def _dft_cos_sin(n):
    idx = np.arange(n, dtype=np.int64)
    ang = 2.0 * np.pi * ((idx[:, None] * idx[None, :]) % n) / n
    return np.cos(ang), np.sin(ang)


def _rope_tables(n_tokens):
    rows = n_tokens // GRID_W
    row = jnp.repeat(jnp.arange(rows, dtype=F32), GRID_W)
    col = jnp.tile(jnp.arange(GRID_W, dtype=F32), rows)
    n_freq = HEAD_DIM // 4
    inv = ROPE_THETA ** (-jnp.arange(n_freq, dtype=F32) / n_freq)
    ang = jnp.concatenate([row[:, None] * inv[None], col[:, None] * inv[None]], axis=-1)
    cos = jnp.repeat(jnp.cos(ang), 2, axis=-1)
    sin = jnp.repeat(jnp.sin(ang), 2, axis=-1)
    sign = jnp.tile(jnp.asarray([-1.0, 1.0], F32), HEAD_DIM // 2)
    return jnp.tile(cos, (1, 2)), jnp.tile(sin * sign, (1, 2))


def _fourier_tables():
    r = FFT_RADIX
    c, s = _dft_cos_sin(r)
    w1 = np.block([[c, s], [-s, c]])
    n_pos = r * r
    idx = np.arange(r, dtype=np.int64)
    k = idx[:, None, None] + r * idx[None, :, None]
    ang = 2.0 * np.pi * ((k * idx[None, None, :]) % n_pos) / n_pos
    scale = (n_pos * FOURIER_HEAD_DIM) ** -0.5
    w2 = np.concatenate([np.cos(ang), np.sin(ang)], axis=2) * scale
    cc, sc = _dft_cos_sin(FOURIER_HEAD_DIM)
    wc = np.concatenate([cc, -sc], axis=1)
    f32 = lambda a: jnp.asarray(np.ascontiguousarray(a), F32)
    return f32(w1).astype(BF16), f32(w2).astype(BF16), f32(wc).astype(BF16)


def kernel(x, c, ctx, c_ctx, ada_w, ada_b, norm_mix_g, norm_ffn_g, mix_in_w, mix_out_w, attn_sink,
           pool_w, pool_scale, ffn_w1, ffn_w3, ffn_w2, final_g):
    B, S, _ = x.shape
    L = ctx.shape[1]
    tm_ffn = 2 * FFN_PART_ROWS
    tm_ctx = L

    cond = jnp.zeros((ADA_ROWS, D_MODEL), F32).at[:B].set(c).at[B].set(c_ctx)
    mods = _ada(cond, ada_w, ada_b)

    cosf, sinf = _rope_tables(S)
    cos_id = jnp.ones((L, LANES), F32)
    sin_id = jnp.zeros((L, LANES), F32)
    w1_dft, w2_dft, wc = _fourier_tables()
    cl, sl_ = _dft_cos_sin(L)
    w_ctx_dft = jnp.asarray(np.concatenate([cl, sl_], axis=1), F32).astype(BF16)

    last_ctx_reader = max(range(0, DEPTH, 2))
    h, hc = x, ctx
    for layer in range(DEPTH):
        update_ctx = layer < last_ctx_reader
        g_mix = norm_mix_g[layer].reshape(1, D_MODEL)
        g_ffn = norm_ffn_g[layer].reshape(1, D_MODEL)
        fin = final_g.reshape(1, D_MODEL) if layer == DEPTH - 1 else None
        j = layer // 2
        if layer % 2 == 0:
            w_in = mix_in_w[j].astype(BF16)
            w_out = mix_out_w[j].astype(BF16)
            sink = attn_sink[j]
            zr_c, zi_c, q_c, k_c, v_c = _in_proj(hc, g_mix, mods, layer, True, w_in, cos_id, sin_id, wc, tm_ctx)
            zr, zi, q, k, v = _in_proj(h, g_mix, mods, layer, False, w_in, cosf, sinf, wc, 4 * IN_PART_ROWS)
            four = _fourier_4096(zr, zi, w1_dft, w2_dft)
            f32_w = (ffn_w1, ffn_w3, ffn_w2)
            cast = [(w.reshape(-1, w.shape[-1]), layer * w.shape[1], 2 * w.shape[1]) for w in f32_w]
            attn, casted = _attention(q, k, v, k_c, v_c, sink, True, cast=cast)
            ffn_w = tuple(cw.reshape(2, *w.shape[1:]) for cw, w in zip(casted, f32_w))
            h = _ffn(h, ("proj", four, attn, w_out), g_ffn, mods, layer, False, ffn_w, fin, tm_ffn)
            if update_ctx:
                four_c = _fourier_small(zr_c, zi_c, w_ctx_dft)
                attn_c = _attention(q_c, None, None, k_c, v_c, sink, False)
                hc = _ffn(hc, ("proj", four_c, attn_c, w_out), g_ffn, mods, layer, True, ffn_w, None, tm_ctx)
        else:
            pw = pool_w[j].astype(BF16)
            psc = pool_scale[j].reshape(1, D_MODEL)
            h = _ffn(h, ("pool", g_mix, pw, psc), g_ffn, mods, layer, False, ffn_w, fin, tm_ffn)
            if update_ctx:
                hc = _ffn(hc, ("pool", g_mix, pw, psc), g_ffn, mods, layer, True, ffn_w, None, tm_ctx)
    return h
```

```python
import functools

import numpy as np
import jax
import jax.numpy as jnp
from jax import lax
from jax.experimental import pallas as pl
from jax.experimental.pallas import tpu as pltpu

F32 = jnp.float32
BF16 = jnp.bfloat16

D_MODEL = 1024
DEPTH = 4
GRID_W = 64
EPS = 1e-6
FOURIER_HEADS = 4
FOURIER_HEAD_DIM = 128
FOURIER_WIDTH = 512
HEAD_DIM = 64
N_Q_HEADS = 8
N_KV_HEADS = 2
GQA_GROUP = 4
ATTN_WIDTH = 512
KV_WIDTH = 128
Q_END = 1024
IN_WIDTH = 1280
BLOCK = 128
ATT_QBLOCKS = 4
ROPE_THETA = 10000.0
POOL_WINDOWS = (2, 4, 8, 16)
POOL_GROUP = 256
POOL_HALO = 8
POOL_PAD = 2 * POOL_HALO
FFN_HIDDEN = 2816
FFN_CHUNKS = ((0, 1024), (1024, 1024), (2048, 768))
FFN_PART_ROWS = 512
IN_PART_ROWS = 512
FFT_RADIX = 64
FFT_ROWS = 16
LANES = 128
BF16_ROWS = 16
LOG2E = 1.4426950408889634
ADA_ROWS = 16
ADA_CTX_ROW = 8
VMEM_LIMIT = 56 * 1024 * 1024


def _params(*sem):
    return pltpu.CompilerParams(dimension_semantics=sem, vmem_limit_bytes=VMEM_LIMIT)


def _rms(x, g):
    ms = jnp.mean(x * x, axis=-1, keepdims=True)
    return x * lax.rsqrt(ms + EPS) * g


def _rms_mod(x, g, shift, scale):
    return _rms(x, g) * (1.0 + scale) + shift


def _dot(a, b):
    return jnp.dot(a, b, preferred_element_type=F32)


def _ada_kernel(c_ref, w_ref, b_ref, o_ref):
    a = jax.nn.silu(c_ref[...])
    m = jnp.dot(a, w_ref[0], precision=lax.Precision.HIGHEST, preferred_element_type=F32) + b_ref[0]
    for r in range(ADA_ROWS):
        o_ref[0, 0, r] = m[r:r + 1]


def _ada(cond, ada_w, ada_b):
    n_chunks = ada_w.shape[-1] // D_MODEL
    return pl.pallas_call(
        _ada_kernel,
        grid=(DEPTH, n_chunks),
        in_specs=[
            pl.BlockSpec((ADA_ROWS, D_MODEL), lambda l, n: (0, 0)),
            pl.BlockSpec((1, D_MODEL, D_MODEL), lambda l, n: (l, 0, n)),
            pl.BlockSpec((1, 1, D_MODEL), lambda l, n: (l, 0, n)),
        ],
        out_specs=pl.BlockSpec((1, 1, ADA_ROWS, 1, D_MODEL), lambda l, n: (l, n, 0, 0, 0)),
        out_shape=jax.ShapeDtypeStruct((DEPTH, n_chunks, ADA_ROWS, 1, D_MODEL), F32),
        compiler_params=_params("arbitrary", "arbitrary"),
        name="ada_mod",
    )(cond, ada_w, ada_b.reshape(DEPTH, 1, ada_w.shape[-1]))


def _mod_operand(mods, layer, chunk, ctx):
    spec = pl.BlockSpec((1, 1, 1, 1, D_MODEL),
                        lambda b, i: (layer, chunk, ADA_CTX_ROW if ctx else b, 0, 0))
    return spec, mods


def _v(ref):
    return ref[0, 0, 0]


def _in_kernel(h_ref, g_ref, sh_ref, sc_ref, w_ref, cos_ref, sin_ref, wc_ref,
               zr_ref, zi_ref, q_ref, k_ref, v_ref, *, tm):
    n_parts = max(1, tm // IN_PART_ROWS)
    pr = tm // n_parts
    rows = [slice(p * pr, (p + 1) * pr) for p in range(n_parts)]
    projs = [_dot(_rms_mod(h_ref[0, r], g_ref[...], _v(sh_ref), _v(sc_ref)).astype(BF16), w_ref[...])
             for r in rows]
    wc = wc_ref[...]
    zs = [[_dot(proj[:, hh * LANES:(hh + 1) * LANES].astype(BF16), wc) for hh in range(FOURIER_HEADS)]
          for proj in projs]
    lane = lax.broadcasted_iota(jnp.int32, (pr, LANES), 1)
    even = (lane & 1) == 0
    low = lane < HEAD_DIM
    for r, proj, z4 in zip(rows, projs, zs):
        cosf = cos_ref[r]
        sinf = sin_ref[r]

        def rope(t):
            partner = jnp.where(even, pltpu.roll(t, LANES - 1, 1), pltpu.roll(t, 1, 1))
            return t * cosf + partner * sinf

        for hh, z in enumerate(z4):
            sl = slice(hh * LANES, (hh + 1) * LANES)
            zr_ref[0, r, sl] = z[:, :LANES].astype(BF16)
            zi_ref[0, r, sl] = z[:, LANES:].astype(BF16)
        for p in range(ATTN_WIDTH // LANES):
            sl = slice(p * LANES, (p + 1) * LANES)
            qq = proj[:, FOURIER_WIDTH + p * LANES:FOURIER_WIDTH + (p + 1) * LANES]
            q_ref[0, r, sl] = (rope(qq) * (HEAD_DIM ** -0.5 * LOG2E)).astype(BF16)
        kk = rope(proj[:, Q_END:Q_END + KV_WIDTH])
        vv = proj[:, Q_END + KV_WIDTH:]
        k_sw = pltpu.roll(kk, HEAD_DIM, 1)
        v_sw = pltpu.roll(vv, HEAD_DIM, 1)
        k_ref[0, 0, r] = jnp.where(low, kk, k_sw).astype(BF16)
        k_ref[0, 1, r] = jnp.where(low, k_sw, kk).astype(BF16)
        v_ref[0, 0, r] = jnp.where(low, vv, v_sw).astype(BF16)
        v_ref[0, 1, r] = jnp.where(low, v_sw, vv).astype(BF16)


def _in_proj(h, g, mods, layer, ctx, w_in, cosf, sinf, wc, tm):
    B, S, _ = h.shape
    (shift_spec, shift), (scale_spec, scale) = (_mod_operand(mods, layer, k, ctx) for k in (0, 1))
    tok = lambda w: pl.BlockSpec((1, tm, w), lambda b, i: (b, i, 0))
    kv = pl.BlockSpec((1, N_KV_HEADS, tm, LANES), lambda b, i: (b, 0, i, 0))
    return pl.pallas_call(
        functools.partial(_in_kernel, tm=tm),
        grid=(B, S // tm),
        in_specs=[
            tok(D_MODEL),
            pl.BlockSpec((1, D_MODEL), lambda b, i: (0, 0)),
            shift_spec, scale_spec,
            pl.BlockSpec((D_MODEL, IN_WIDTH), lambda b, i: (0, 0)),
            pl.BlockSpec((tm, LANES), lambda b, i: (i, 0)),
            pl.BlockSpec((tm, LANES), lambda b, i: (i, 0)),
            pl.BlockSpec((FOURIER_HEAD_DIM, 2 * FOURIER_HEAD_DIM), lambda b, i: (0, 0)),
        ],
        out_specs=[tok(FOURIER_WIDTH), tok(FOURIER_WIDTH), tok(ATTN_WIDTH), kv, kv],
        out_shape=[
            jax.ShapeDtypeStruct((B, S, FOURIER_WIDTH), BF16),
            jax.ShapeDtypeStruct((B, S, FOURIER_WIDTH), BF16),
            jax.ShapeDtypeStruct((B, S, ATTN_WIDTH), BF16),
            jax.ShapeDtypeStruct((B, N_KV_HEADS, S, LANES), BF16),
            jax.ShapeDtypeStruct((B, N_KV_HEADS, S, LANES), BF16),
        ],
        compiler_params=_params("arbitrary", "arbitrary"),
        name="in_proj",
    )(h, g, shift, scale, w_in, cosf, sinf, wc)


def _fft_kernel(zr_ref, zi_ref, w1_ref, w2_ref, o_ref, y_ref, *, t):
    r = FFT_RADIX
    w1 = w1_ref[...]
    for c in range(r // t):
        cs = slice(c * t, (c + 1) * t)
        xr = jnp.swapaxes(zr_ref[0, :, cs, :], 0, 1)
        xi = jnp.swapaxes(zi_ref[0, :, cs, :], 0, 1)
        outs_r, outs_i = [], []
        for j in range(t):
            y = _dot(w1, jnp.concatenate([xr[j], xi[j]], axis=0))
            outs_r.append(y[:r].astype(BF16))
            outs_i.append(y[r:].astype(BF16))
        y_ref[0, :, cs, :] = jnp.swapaxes(jnp.stack(outs_r, axis=0), 0, 1)
        y_ref[1, :, cs, :] = jnp.swapaxes(jnp.stack(outs_i, axis=0), 0, 1)
    for c in range(r // t):
        outs = []
        for j in range(t):
            k1 = c * t + j
            x = jnp.concatenate([y_ref[0, k1], y_ref[1, k1]], axis=0)
            outs.append(_dot(w2_ref[k1], x).astype(BF16))
        o_ref[0, :, c * t:(c + 1) * t, :] = jnp.swapaxes(jnp.stack(outs, axis=0), 0, 1)


def _fourier_4096(zr, zi, w1, w2):
    B, S, W = zr.shape
    r = FFT_RADIX
    zr4 = zr.reshape(B, r, r, W)
    zi4 = zi.reshape(B, r, r, W)
    blk = pl.BlockSpec((1, r, r, W), lambda b: (b, 0, 0, 0))
    out = pl.pallas_call(
        functools.partial(_fft_kernel, t=FFT_ROWS),
        grid=(B,),
        in_specs=[blk, blk, pl.BlockSpec((2 * r, 2 * r), lambda b: (0, 0)),
                  pl.BlockSpec((r, r, 2 * r), lambda b: (0, 0, 0))],
        out_specs=blk,
        out_shape=jax.ShapeDtypeStruct((B, r, r, W), BF16),
        scratch_shapes=[pltpu.VMEM((2, r, r, W), BF16)],
        compiler_params=_params("arbitrary"),
        name="fft_4096",
    )(zr4, zi4, w1, w2)
    return out.reshape(B, S, W)


def _dft_small_kernel(zr_ref, zi_ref, w_ref, o_ref, *, scale):
    x = jnp.concatenate([zr_ref[0], zi_ref[0]], axis=0)
    o_ref[0] = (_dot(w_ref[...], x) * scale).astype(BF16)


def _fourier_small(zr, zi, w):
    B, L, W = zr.shape
    scale = float((L * FOURIER_HEAD_DIM) ** -0.5)
    blk = pl.BlockSpec((1, L, W), lambda b: (b, 0, 0))
    return pl.pallas_call(
        functools.partial(_dft_small_kernel, scale=scale),
        grid=(B,),
        in_specs=[blk, blk, pl.BlockSpec((L, 2 * L), lambda b: (0, 0))],
        out_specs=blk,
        out_shape=jax.ShapeDtypeStruct((B, L, W), BF16),
        compiler_params=_params("arbitrary"),
        name="dft_ctx",
    )(zr, zi, w)


def _att_kernel(sink_ref, q_ref, *refs, n_steps, qb, window, n_cast):
    n_kv = 8 if window else 2
    kv_refs, cast_in = refs[:n_kv], refs[n_kv:n_kv + n_cast]
    o_ref, cast_out = refs[n_kv + n_cast], refs[n_kv + n_cast + 1:]
    if window:
        kp_ref, kc_ref, kn_ref, vp_ref, vc_ref, vn_ref, kx_ref, vx_ref = kv_refs
    else:
        kx_ref, vx_ref = kv_refs
    for src, dst in zip(cast_in, cast_out):
        dst[...] = src[...].astype(BF16)
    i = pl.program_id(1)
    lane = lax.broadcasted_iota(jnp.int32, (BLOCK, LANES), 1)
    low = lane < HEAD_DIM
    if window:
        row = lax.broadcasted_iota(jnp.int32, (BLOCK, BLOCK), 0)
        col = lax.broadcasted_iota(jnp.int32, (BLOCK, BLOCK), 1)
        ninf = jnp.float32(-jnp.inf)
        before = col >= row
        after = col <= row

    def window_tiles(p_ref, c_ref, n_ref, kh, a):
        cur = lambda j: c_ref[0, kh, j * BLOCK:(j + 1) * BLOCK]
        first = p_ref[0, kh] if a == 0 else cur(a - 1)
        last = n_ref[0, kh] if a == qb - 1 else cur(a + 1)
        return [first, cur(a), last]

    def scores(a, kh):
        if window:
            k = jnp.concatenate(window_tiles(kp_ref, kc_ref, kn_ref, kh, a) + [kx_ref[0, kh]], axis=0)
        else:
            k = kx_ref[0, kh]
        qs = []
        for g in range(GQA_GROUP):
            h = kh * GQA_GROUP + g
            qp = q_ref[0, a * BLOCK:(a + 1) * BLOCK, (h // 2) * LANES:(h // 2 + 1) * LANES]
            keep = low if h % 2 == 0 else jnp.logical_not(low)
            qs.append(jnp.where(keep, qp, jnp.zeros_like(qp)))
        qst = jnp.concatenate(qs, axis=0)
        return lax.dot_general(qst, k, (((1,), (1,)), ((), ())), preferred_element_type=F32)

    def finish(a, kh, s):
        if window:
            v = jnp.concatenate(window_tiles(vp_ref, vc_ref, vn_ref, kh, a) + [vx_ref[0, kh]], axis=0)
            ok_first = before & (i > 0) if a == 0 else before
            ok_last = after & (i < n_steps - 1) if a == qb - 1 else after
            b_first = jnp.where(ok_first, 0.0, ninf)
            b_last = jnp.where(ok_last, 0.0, ninf)
        else:
            v = vx_ref[0, kh]
        ps, ls = [], []
        for g in range(GQA_GROUP):
            sk = sink_ref[kh * GQA_GROUP + g] * LOG2E
            sg = s[g * BLOCK:(g + 1) * BLOCK]
            if window:
                sg = jnp.concatenate(
                    [sg[:, :BLOCK] + b_first, sg[:, BLOCK:2 * BLOCK],
                     sg[:, 2 * BLOCK:3 * BLOCK] + b_last, sg[:, 3 * BLOCK:]], axis=1)
            m = jnp.maximum(jnp.max(sg, axis=-1, keepdims=True), sk)
            p = jnp.exp2(sg - m)
            ls.append(jnp.sum(p, axis=-1, keepdims=True) + jnp.exp2(sk - m))
            ps.append(p.astype(BF16))
        o = _dot(jnp.concatenate(ps, axis=0), v)
        og = [o[g * BLOCK:(g + 1) * BLOCK] / ls[g] for g in range(GQA_GROUP)]
        for pr in range(GQA_GROUP // 2):
            c0 = kh * GQA_GROUP * HEAD_DIM + pr * LANES
            o_ref[0, a * BLOCK:(a + 1) * BLOCK, c0:c0 + LANES] = jnp.where(
                low, og[2 * pr], og[2 * pr + 1]).astype(BF16)

    units = [(a, kh) for a in range(qb) for kh in range(N_KV_HEADS)]
    ahead = 2
    pending = [scores(*u) for u in units[:ahead]]
    for n, u in enumerate(units):
        s = pending.pop(0)
        if n + ahead < len(units):
            pending.append(scores(*units[n + ahead]))
        finish(*u, s)


def _cast_operands(cast, grid_steps, flat_step):
    in_specs, out_specs, out_shapes = [], [], []
    for w, first, n in cast:
        per, every = n // grid_steps, 1
        while per % BF16_ROWS:
            per, every = 2 * per, 2 * every
        assert n % per == 0 and first % per == 0
        blk = (per, w.shape[1])
        in_specs.append(pl.BlockSpec(blk, lambda b, i, e=every, o=first // per: (o + flat_step(b, i) // e, 0)))
        out_specs.append(pl.BlockSpec(blk, lambda b, i, e=every: (flat_step(b, i) // e, 0)))
        out_shapes.append(jax.ShapeDtypeStruct((n, w.shape[1]), BF16))
    return in_specs, out_specs, out_shapes


def _attention(q, k, v, kx, vx, sink, window, cast=()):
    B, S, _ = q.shape
    nb = S // BLOCK
    qb = min(ATT_QBLOCKS, nb)
    rows = qb * BLOCK
    n_steps = S // rows
    L = kx.shape[2] // (B // kx.shape[0])
    cast_in, cast_out, cast_shapes = _cast_operands(cast, B * n_steps, lambda b, i: b * n_steps + i)
    qspec = pl.BlockSpec((1, rows, ATTN_WIDTH), lambda b, i: (b, i, 0))
    if kx.shape[0] == B:
        xspec = pl.BlockSpec((1, N_KV_HEADS, L, LANES), lambda b, i: (b, 0, 0, 0))
    else:
        xspec = pl.BlockSpec((1, N_KV_HEADS, L, LANES), lambda b, i: (0, 0, b, 0))
    smem = pl.BlockSpec(memory_space=pltpu.SMEM)
    if window:
        blk = (1, N_KV_HEADS, BLOCK, LANES)
        prev = pl.BlockSpec(blk, lambda b, i: (b, 0, jnp.maximum(i * qb - 1, 0), 0))
        cur = pl.BlockSpec((1, N_KV_HEADS, rows, LANES), lambda b, i: (b, 0, i, 0))
        nxt = pl.BlockSpec(blk, lambda b, i: (b, 0, jnp.minimum((i + 1) * qb, nb - 1), 0))
        in_specs = [smem, qspec, prev, cur, nxt, prev, cur, nxt, xspec, xspec]
        args = (sink, q, k, k, k, v, v, v, kx, vx)
    else:
        in_specs = [smem, qspec, xspec, xspec]
        args = (sink, q, kx, vx)
    out, *casted = pl.pallas_call(
        functools.partial(_att_kernel, n_steps=n_steps, qb=qb, window=window, n_cast=len(cast)),
        grid=(B, n_steps),
        in_specs=in_specs + cast_in,
        out_specs=[qspec] + cast_out,
        out_shape=[jax.ShapeDtypeStruct((B, S, ATTN_WIDTH), BF16)] + cast_shapes,
        compiler_params=_params("arbitrary", "arbitrary"),
        name="window_attn" if window else "ctx_attn",
    )(*args, *(w for w, _, _ in cast))
    return (out, casted) if cast else out


def _pool_fill(xs_ref, h_ref, hp_ref, hx_ref, g, sh, sc, lo, hi, tm, first_tile, last_tile):
    hal = POOL_HALO
    if lo == 0:
        xs_ref[0:hal] = jnp.where(first_tile, 0.0, _rms_mod(hp_ref[0], g, sh, sc))
    xs_ref[hal + lo:hal + hi] = _rms_mod(h_ref[0, lo:hi], g, sh, sc)
    if hi == tm:
        xs_ref[hal + tm:2 * hal + tm] = jnp.where(last_tile, 0.0, _rms_mod(hx_ref[0], g, sh, sc))
        xs_ref[2 * hal + tm:] = jnp.zeros((POOL_PAD, D_MODEL), F32)


def _pool_part(xs_ref, p_ref, ic_ref, pw_ref, r0, pr):
    hal = POOL_HALO
    ys = []
    for gi, w in enumerate(POOL_WINDOWS):
        cs = slice(gi * POOL_GROUP, (gi + 1) * POOL_GROUP)
        x = lambda off, n: xs_ref[r0 + off:r0 + off + n, cs]
        if w == 2:
            acc = x(hal - 1, pr) + x(hal, pr)
        else:
            n2, n4, n8 = pr + 3 * hal, pr + 2 * hal, pr + hal
            p_ref[0:n2] = x(0, n2) + x(1, n2)
            if w == 4:
                acc = p_ref[hal - 2:hal - 2 + pr] + p_ref[hal:hal + pr]
            else:
                p_ref[0:n4] = p_ref[0:n4] + p_ref[2:n4 + 2]
                if w == 8:
                    acc = p_ref[hal - 4:hal - 4 + pr] + p_ref[hal:hal + pr]
                else:
                    p_ref[0:n8] = p_ref[0:n8] + p_ref[4:n8 + 4]
                    acc = p_ref[0:pr] + p_ref[hal:hal + pr]
        yg = acc * ic_ref[r0:r0 + pr, gi:gi + 1] - x(hal, pr)
        ys.append(_dot(yg.astype(BF16), pw_ref[gi]))
    return jnp.concatenate(ys, axis=1)


def _ffn_kernel(*refs, tm, seq, mode, final):
    refs = list(refs)
    take = lambda n: [refs.pop(0) for _ in range(n)]
    (h_ref,) = take(1)
    if mode == "proj":
        four_ref, attn_ref, wo_ref, g1_ref = take(4)
    elif mode == "pool":
        hp_ref, hx_ref, gm_ref, sh1_ref, sc1_ref, g1_ref, pw_ref, ps_ref, ic_ref = take(9)
    gn_ref, sh_ref, sc_ref, g2_ref, w1_ref, w3_ref, w2_ref = take(7)
    if final:
        (fg_ref,) = take(1)
    (o_ref,) = take(1)
    if mode == "pool":
        xs_ref, p_ref = take(2)
        first_tile = pl.program_id(1) == 0
        last_tile = pl.program_id(1) == seq // tm - 1
    n_parts = max(1, tm // FFN_PART_ROWS)
    pr = tm // n_parts
    rows = [slice(p * pr, (p + 1) * pr) for p in range(n_parts)]

    def mixed(p):
        h = h_ref[0, rows[p]]
        if mode == "proj":
            y = (_dot(four_ref[0, rows[p]], wo_ref[:FOURIER_WIDTH])
                 + _dot(attn_ref[0, rows[p]], wo_ref[FOURIER_WIDTH:]))
            return h + _v(g1_ref) * y
        if mode == "pool":
            lo = 0 if p == 0 else p * pr + POOL_PAD
            hi = tm if p == n_parts - 1 else (p + 1) * pr + POOL_PAD
            _pool_fill(xs_ref, h_ref, hp_ref, hx_ref, gm_ref[...], _v(sh1_ref), _v(sc1_ref), lo, hi, tm,
                       first_tile, last_tile)
            y = _pool_part(xs_ref, p_ref, ic_ref, pw_ref, p * pr, pr) * ps_ref[...]
            return h + _v(g1_ref) * y
        return h

    def normed(h):
        return _rms_mod(h, gn_ref[...], _v(sh_ref), _v(sc_ref)).astype(BF16)

    def up(hn, c0, cn):
        return _dot(hn, w1_ref[0, :, c0:c0 + cn]), _dot(hn, w3_ref[0, :, c0:c0 + cn])

    c0, cn = FFN_CHUNKS[0]
    if mode == "pool":
        hs, hns, ab = [], [], []
        for p in range(n_parts):
            hs.append(mixed(p))
            hns.append(normed(hs[p]))
            ab.append(up(hns[p], c0, cn))
    else:
        hs = [mixed(p) for p in range(n_parts)]
        hns = [normed(h) for h in hs]
        ab = [up(hn, c0, cn) for hn in hns]
    accs = [None] * n_parts
    for ci, (c0, cn) in enumerate(FFN_CHUNKS):
        if ci > 0:
            ab = [up(hn, c0, cn) for hn in hns]
        for p, (a, b) in enumerate(ab):
            t = (jax.nn.silu(a) * b).astype(BF16)
            d = _dot(t, w2_ref[0, c0:c0 + cn, :])
            accs[p] = d if accs[p] is None else accs[p] + d
    for p, r in enumerate(rows):
        out = hs[p] + _v(g2_ref) * accs[p]
        if final:
            out = _rms(out, fg_ref[...])
        o_ref[0, r] = out


def _ffn(h, mix, gn, mods, layer, ctx, ffn_w, final_g, tm):
    B, S, _ = h.shape
    mode = "none" if mix is None else mix[0]
    vecs = [_mod_operand(mods, layer, k, ctx) for k in range(6)]
    vspec = [v[0] for v in vecs]
    row = pl.BlockSpec((1, D_MODEL), lambda b, i: (0, 0))
    tok = lambda w: pl.BlockSpec((1, tm, w), lambda b, i: (b, i, 0))
    whole = pl.BlockSpec(memory_space=pltpu.VMEM)
    in_specs = [tok(D_MODEL)]
    args = [h]
    scratch = []
    if mode == "proj":
        _, four, attn, w_out = mix
        in_specs += [tok(FOURIER_WIDTH), tok(ATTN_WIDTH), whole, vspec[2]]
        args += [four, attn, w_out, mods]
    elif mode == "pool":
        _, g_mix, pool_w, pool_scale = mix
        hb = tm // POOL_HALO
        halo = (1, POOL_HALO, D_MODEL)
        t = np.arange(S)
        inv_count = jnp.asarray(np.stack(
            [1.0 / (np.minimum(t + w // 2, S) - np.maximum(t - w // 2, 0)) for w in POOL_WINDOWS], axis=1), F32)
        in_specs += [
            pl.BlockSpec(halo, lambda b, i: (b, jnp.maximum(i * hb - 1, 0), 0)),
            pl.BlockSpec(halo, lambda b, i: (b, jnp.minimum((i + 1) * hb, S // POOL_HALO - 1), 0)),
            row, vspec[0], vspec[1], vspec[2],
            pl.BlockSpec((len(POOL_WINDOWS), POOL_GROUP, POOL_GROUP), lambda b, i: (0, 0, 0)),
            row,
            pl.BlockSpec((tm, len(POOL_WINDOWS)), lambda b, i: (i, 0)),
        ]
        args += [h, h, g_mix, mods, mods, mods, pool_w, pool_scale, inv_count]
        pr = tm // max(1, tm // FFN_PART_ROWS)
        scratch = [pltpu.VMEM((tm + 2 * POOL_HALO + POOL_PAD, D_MODEL), F32),
                   pltpu.VMEM((pr + 3 * POOL_HALO, POOL_GROUP), F32)]
    in_specs += [row, vspec[3], vspec[4], vspec[5]] + [
        pl.BlockSpec((1,) + w.shape[1:], lambda b, i, n=w.shape[0]: (layer % n, 0, 0),
                     pipeline_mode=pl.Buffered(1)) for w in ffn_w]
    args += [gn, mods, mods, mods, *ffn_w]
    if final_g is not None:
        in_specs.append(row)
        args.append(final_g)
    return pl.pallas_call(
        functools.partial(_ffn_kernel, tm=tm, seq=S, mode=mode, final=final_g is not None),
        grid=(B, S // tm),
        in_specs=in_specs,
        out_specs=tok(D_MODEL),
        out_shape=jax.ShapeDtypeStruct((B, S, D_MODEL), F32),
        scratch_shapes=scratch,
        compiler_params=_params("arbitrary", "arbitrary"),
        name="ffn_" + mode,
    )(*args)


def _dft_cos_sin(n):
    idx = np.arange(n, dtype=np.int64)
    ang = 2.0 * np.pi * ((idx[:, None] * idx[None, :]) % n) / n
    return np.cos(ang), np.sin(ang)


def _rope_tables(n_tokens):
    rows = n_tokens // GRID_W
    row = jnp.repeat(jnp.arange(rows, dtype=F32), GRID_W)
    col = jnp.tile(jnp.arange(GRID_W, dtype=F32), rows)
    n_freq = HEAD_DIM // 4
    inv = ROPE_THETA ** (-jnp.arange(n_freq, dtype=F32) / n_freq)
    ang = jnp.concatenate([row[:, None] * inv[None], col[:, None] * inv[None]], axis=-1)
    cos = jnp.repeat(jnp.cos(ang), 2, axis=-1)
    sin = jnp.repeat(jnp.sin(ang), 2, axis=-1)
    sign = jnp.tile(jnp.asarray([-1.0, 1.0], F32), HEAD_DIM // 2)
    return jnp.tile(cos, (1, 2)), jnp.tile(sin * sign, (1, 2))


def _fourier_tables():
    r = FFT_RADIX
    c, s = _dft_cos_sin(r)
    w1 = np.block([[c, s], [-s, c]])
    n_pos = r * r
    idx = np.arange(r, dtype=np.int64)
    k = idx[:, None, None] + r * idx[None, :, None]
    ang = 2.0 * np.pi * ((k * idx[None, None, :]) % n_pos) / n_pos
    scale = (n_pos * FOURIER_HEAD_DIM) ** -0.5
    w2 = np.concatenate([np.cos(ang), np.sin(ang)], axis=2) * scale
    cc, sc = _dft_cos_sin(FOURIER_HEAD_DIM)
    wc = np.concatenate([cc, -sc], axis=1)
    f32 = lambda a: jnp.asarray(np.ascontiguousarray(a), F32)
    return f32(w1).astype(BF16), f32(w2).astype(BF16), f32(wc).astype(BF16)


def kernel(x, c, ctx, c_ctx, ada_w, ada_b, norm_mix_g, norm_ffn_g, mix_in_w, mix_out_w, attn_sink,
           pool_w, pool_scale, ffn_w1, ffn_w3, ffn_w2, final_g):
    B, S, _ = x.shape
    L = ctx.shape[1]
    tm_ffn = 2 * FFN_PART_ROWS
    tm_ctx = L
    tm_flat = min(B * L, 2 * FFN_PART_ROWS)

    cond = jnp.zeros((ADA_ROWS, D_MODEL), F32).at[:B].set(c).at[B].set(c_ctx)
    mods = _ada(cond, ada_w, ada_b)

    cosf, sinf = _rope_tables(S)
    cos_id = jnp.ones((B * L, LANES), F32)
    sin_id = jnp.zeros((B * L, LANES), F32)
    w1_dft, w2_dft, wc = _fourier_tables()
    cl, sl_ = _dft_cos_sin(L)
    w_ctx_dft = jnp.asarray(np.concatenate([cl, sl_], axis=1), F32).astype(BF16)

    last_ctx_reader = max(range(0, DEPTH, 2))
    h, hc = x, ctx
    for layer in range(DEPTH):
        update_ctx = layer < last_ctx_reader
        g_mix = norm_mix_g[layer].reshape(1, D_MODEL)
        g_ffn = norm_ffn_g[layer].reshape(1, D_MODEL)
        fin = final_g.reshape(1, D_MODEL) if layer == DEPTH - 1 else None
        j = layer // 2
        if layer % 2 == 0:
            w_in = mix_in_w[j].astype(BF16)
            w_out = mix_out_w[j].astype(BF16)
            sink = attn_sink[j]
            flat = lambda a: a.reshape(1, B * L, a.shape[-1])
            unflat = lambda a: a.reshape(B, L, a.shape[-1])
            zr_c, zi_c, q_c, k_c, v_c = _in_proj(flat(hc), g_mix, mods, layer, True, w_in, cos_id, sin_id, wc,
                                                 tm_flat)
            zr, zi, q, k, v = _in_proj(h, g_mix, mods, layer, False, w_in, cosf, sinf, wc, 4 * IN_PART_ROWS)
            four = _fourier_4096(zr, zi, w1_dft, w2_dft)
            f32_w = (ffn_w1, ffn_w3, ffn_w2)
            cast = [(w.reshape(-1, w.shape[-1]), layer * w.shape[1], 2 * w.shape[1]) for w in f32_w]
            attn, casted = _attention(q, k, v, k_c, v_c, sink, True, cast=cast)
            ffn_w = tuple(cw.reshape(2, *w.shape[1:]) for cw, w in zip(casted, f32_w))
            h = _ffn(h, ("proj", four, attn, w_out), g_ffn, mods, layer, False, ffn_w, fin, tm_ffn)
            if update_ctx:
                four_c = _fourier_small(unflat(zr_c), unflat(zi_c), w_ctx_dft)
                attn_c = _attention(unflat(q_c), None, None, k_c, v_c, sink, False)
                hc = unflat(_ffn(flat(hc), ("proj", flat(four_c), flat(attn_c), w_out), g_ffn, mods, layer, True,
                                 ffn_w, None, tm_flat))
        else:
            pw = pool_w[j].astype(BF16)
            psc = pool_scale[j].reshape(1, D_MODEL)
            h = _ffn(h, ("pool", g_mix, pw, psc), g_ffn, mods, layer, False, ffn_w, fin, tm_ffn)
            if update_ctx:
                hc = _ffn(hc, ("pool", g_mix, pw, psc), g_ffn, mods, layer, True, ffn_w, None, tm_ctx)
    return h
```

```python
import functools

import numpy as np
import jax
import jax.numpy as jnp
from jax import lax
from jax.experimental import pallas as pl
from jax.experimental.pallas import tpu as pltpu

F32 = jnp.float32
BF16 = jnp.bfloat16

D_MODEL = 1024
DEPTH = 4
GRID_W = 64
EPS = 1e-6
FOURIER_HEADS = 4
FOURIER_HEAD_DIM = 128
FOURIER_WIDTH = 512
HEAD_DIM = 64
N_Q_HEADS = 8
N_KV_HEADS = 2
GQA_GROUP = 4
ATTN_WIDTH = 512
KV_WIDTH = 128
Q_END = 1024
IN_WIDTH = 1280
BLOCK = 128
ATT_QBLOCKS = 8
ROPE_THETA = 10000.0
POOL_WINDOWS = (2, 4, 8, 16)
POOL_GROUP = 256
POOL_HALO = 8
POOL_PAD = 2 * POOL_HALO
FFN_HIDDEN = 2816
FFN_CHUNKS = ((0, 1024), (1024, 1024), (2048, 768))
FFN_PART_ROWS = 512
FFN_POOL_PART_ROWS = 256
IN_PART_ROWS = 512
FFT_RADIX = 64
FFT_ROWS = 16
LANES = 128
BF16_ROWS = 16
LOG2E = 1.4426950408889634
ADA_ROWS = 16
ADA_CTX_ROW = 8
VMEM_LIMIT = 56 * 1024 * 1024


def _params(*sem):
    return pltpu.CompilerParams(dimension_semantics=sem, vmem_limit_bytes=VMEM_LIMIT)


def _rms(x, g):
    ms = jnp.mean(x * x, axis=-1, keepdims=True)
    return x * lax.rsqrt(ms + EPS) * g


def _rms_mod(x, g, shift, scale):
    return _rms(x, g) * (1.0 + scale) + shift


def _dot(a, b):
    return jnp.dot(a, b, preferred_element_type=F32)


def _ada_kernel(c_ref, w_ref, b_ref, o_ref):
    a = jax.nn.silu(c_ref[...])
    m = jnp.dot(a, w_ref[0], precision=lax.Precision.HIGHEST, preferred_element_type=F32) + b_ref[0]
    for r in range(ADA_ROWS):
        o_ref[0, 0, r] = m[r:r + 1]


def _ada(cond, ada_w, ada_b):
    n_chunks = ada_w.shape[-1] // D_MODEL
    return pl.pallas_call(
        _ada_kernel,
        grid=(DEPTH, n_chunks),
        in_specs=[
            pl.BlockSpec((ADA_ROWS, D_MODEL), lambda l, n: (0, 0)),
            pl.BlockSpec((1, D_MODEL, D_MODEL), lambda l, n: (l, 0, n)),
            pl.BlockSpec((1, 1, D_MODEL), lambda l, n: (l, 0, n)),
        ],
        out_specs=pl.BlockSpec((1, 1, ADA_ROWS, 1, D_MODEL), lambda l, n: (l, n, 0, 0, 0)),
        out_shape=jax.ShapeDtypeStruct((DEPTH, n_chunks, ADA_ROWS, 1, D_MODEL), F32),
        compiler_params=_params("arbitrary", "arbitrary"),
        name="ada_mod",
    )(cond, ada_w, ada_b.reshape(DEPTH, 1, ada_w.shape[-1]))


def _mod_operand(mods, layer, chunk, ctx):
    spec = pl.BlockSpec((1, 1, 1, 1, D_MODEL),
                        lambda b, i: (layer, chunk, ADA_CTX_ROW if ctx else b, 0, 0))
    return spec, mods


def _v(ref):
    return ref[0, 0, 0]


def _in_kernel(h_ref, g_ref, sh_ref, sc_ref, w_ref, cos_ref, sin_ref, wc_ref,
               zr_ref, zi_ref, q_ref, k_ref, v_ref, *, tm):
    n_parts = max(1, tm // IN_PART_ROWS)
    pr = tm // n_parts
    rows = [slice(p * pr, (p + 1) * pr) for p in range(n_parts)]
    projs = [_dot(_rms_mod(h_ref[0, r], g_ref[...], _v(sh_ref), _v(sc_ref)).astype(BF16), w_ref[...])
             for r in rows]
    wc = wc_ref[...]
    zs = [[_dot(proj[:, hh * LANES:(hh + 1) * LANES].astype(BF16), wc) for hh in range(FOURIER_HEADS)]
          for proj in projs]
    lane = lax.broadcasted_iota(jnp.int32, (pr, LANES), 1)
    even = (lane & 1) == 0
    low = lane < HEAD_DIM
    for r, proj, z4 in zip(rows, projs, zs):
        cosf = cos_ref[r]
        sinf = sin_ref[r]

        def rope(t):
            partner = jnp.where(even, pltpu.roll(t, LANES - 1, 1), pltpu.roll(t, 1, 1))
            return t * cosf + partner * sinf

        for hh, z in enumerate(z4):
            sl = slice(hh * LANES, (hh + 1) * LANES)
            zr_ref[0, r, sl] = z[:, :LANES].astype(BF16)
            zi_ref[0, r, sl] = z[:, LANES:].astype(BF16)
        for p in range(ATTN_WIDTH // LANES):
            sl = slice(p * LANES, (p + 1) * LANES)
            qq = proj[:, FOURIER_WIDTH + p * LANES:FOURIER_WIDTH + (p + 1) * LANES]
            q_ref[0, r, sl] = (rope(qq) * (HEAD_DIM ** -0.5 * LOG2E)).astype(BF16)
        kk = rope(proj[:, Q_END:Q_END + KV_WIDTH])
        vv = proj[:, Q_END + KV_WIDTH:]
        k_sw = pltpu.roll(kk, HEAD_DIM, 1)
        v_sw = pltpu.roll(vv, HEAD_DIM, 1)
        k_ref[0, 0, r] = jnp.where(low, kk, k_sw).astype(BF16)
        k_ref[0, 1, r] = jnp.where(low, k_sw, kk).astype(BF16)
        v_ref[0, 0, r] = jnp.where(low, vv, v_sw).astype(BF16)
        v_ref[0, 1, r] = jnp.where(low, v_sw, vv).astype(BF16)


def _in_proj(h, g, mods, layer, ctx, w_in, cosf, sinf, wc, tm):
    B, S, _ = h.shape
    (shift_spec, shift), (scale_spec, scale) = (_mod_operand(mods, layer, k, ctx) for k in (0, 1))
    tok = lambda w: pl.BlockSpec((1, tm, w), lambda b, i: (b, i, 0))
    kv = pl.BlockSpec((1, N_KV_HEADS, tm, LANES), lambda b, i: (b, 0, i, 0))
    return pl.pallas_call(
        functools.partial(_in_kernel, tm=tm),
        grid=(B, S // tm),
        in_specs=[
            tok(D_MODEL),
            pl.BlockSpec((1, D_MODEL), lambda b, i: (0, 0)),
            shift_spec, scale_spec,
            pl.BlockSpec((D_MODEL, IN_WIDTH), lambda b, i: (0, 0)),
            pl.BlockSpec((tm, LANES), lambda b, i: (i, 0)),
            pl.BlockSpec((tm, LANES), lambda b, i: (i, 0)),
            pl.BlockSpec((FOURIER_HEAD_DIM, 2 * FOURIER_HEAD_DIM), lambda b, i: (0, 0)),
        ],
        out_specs=[tok(FOURIER_WIDTH), tok(FOURIER_WIDTH), tok(ATTN_WIDTH), kv, kv],
        out_shape=[
            jax.ShapeDtypeStruct((B, S, FOURIER_WIDTH), BF16),
            jax.ShapeDtypeStruct((B, S, FOURIER_WIDTH), BF16),
            jax.ShapeDtypeStruct((B, S, ATTN_WIDTH), BF16),
            jax.ShapeDtypeStruct((B, N_KV_HEADS, S, LANES), BF16),
            jax.ShapeDtypeStruct((B, N_KV_HEADS, S, LANES), BF16),
        ],
        compiler_params=_params("arbitrary", "arbitrary"),
        name="in_proj",
    )(h, g, shift, scale, w_in, cosf, sinf, wc)


def _fft_kernel(zr_ref, zi_ref, w1_ref, w2_ref, o_ref, y_ref, *, t):
    r = FFT_RADIX
    w1 = w1_ref[...]
    for c in range(r // t):
        cs = slice(c * t, (c + 1) * t)
        xr = jnp.swapaxes(zr_ref[0, :, cs, :], 0, 1)
        xi = jnp.swapaxes(zi_ref[0, :, cs, :], 0, 1)
        outs_r, outs_i = [], []
        for j in range(t):
            y = _dot(w1, jnp.concatenate([xr[j], xi[j]], axis=0))
            outs_r.append(y[:r].astype(BF16))
            outs_i.append(y[r:].astype(BF16))
        y_ref[0, :, cs, :] = jnp.swapaxes(jnp.stack(outs_r, axis=0), 0, 1)
        y_ref[1, :, cs, :] = jnp.swapaxes(jnp.stack(outs_i, axis=0), 0, 1)
    for c in range(r // t):
        outs = []
        for j in range(t):
            k1 = c * t + j
            x = jnp.concatenate([y_ref[0, k1], y_ref[1, k1]], axis=0)
            outs.append(_dot(w2_ref[k1], x).astype(BF16))
        o_ref[0, :, c * t:(c + 1) * t, :] = jnp.swapaxes(jnp.stack(outs, axis=0), 0, 1)


def _fourier_4096(zr, zi, w1, w2):
    B, S, W = zr.shape
    r = FFT_RADIX
    zr4 = zr.reshape(B, r, r, W)
    zi4 = zi.reshape(B, r, r, W)
    blk = pl.BlockSpec((1, r, r, W), lambda b: (b, 0, 0, 0))
    out = pl.pallas_call(
        functools.partial(_fft_kernel, t=FFT_ROWS),
        grid=(B,),
        in_specs=[blk, blk, pl.BlockSpec((2 * r, 2 * r), lambda b: (0, 0)),
                  pl.BlockSpec((r, r, 2 * r), lambda b: (0, 0, 0))],
        out_specs=blk,
        out_shape=jax.ShapeDtypeStruct((B, r, r, W), BF16),
        scratch_shapes=[pltpu.VMEM((2, r, r, W), BF16)],
        compiler_params=_params("arbitrary"),
        name="fft_4096",
    )(zr4, zi4, w1, w2)
    return out.reshape(B, S, W)


def _dft_small_kernel(zr_ref, zi_ref, w_ref, o_ref, *, scale):
    x = jnp.concatenate([zr_ref[0], zi_ref[0]], axis=0)
    o_ref[0] = (_dot(w_ref[...], x) * scale).astype(BF16)


def _fourier_small(zr, zi, w):
    B, L, W = zr.shape
    scale = float((L * FOURIER_HEAD_DIM) ** -0.5)
    blk = pl.BlockSpec((1, L, W), lambda b: (b, 0, 0))
    return pl.pallas_call(
        functools.partial(_dft_small_kernel, scale=scale),
        grid=(B,),
        in_specs=[blk, blk, pl.BlockSpec((L, 2 * L), lambda b: (0, 0))],
        out_specs=blk,
        out_shape=jax.ShapeDtypeStruct((B, L, W), BF16),
        compiler_params=_params("arbitrary"),
        name="dft_ctx",
    )(zr, zi, w)


def _att_kernel(sink_ref, q_ref, *refs, n_steps, qb, window, n_cast):
    n_kv = 8 if window else 2
    kv_refs, cast_in = refs[:n_kv], refs[n_kv:n_kv + n_cast]
    o_ref, cast_out = refs[n_kv + n_cast], refs[n_kv + n_cast + 1:]
    if window:
        kp_ref, kc_ref, kn_ref, vp_ref, vc_ref, vn_ref, kx_ref, vx_ref = kv_refs
    else:
        kx_ref, vx_ref = kv_refs
    for src, dst in zip(cast_in, cast_out):
        dst[...] = src[...].astype(BF16)
    i = pl.program_id(1)
    lane = lax.broadcasted_iota(jnp.int32, (BLOCK, LANES), 1)
    low = lane < HEAD_DIM
    if window:
        row = lax.broadcasted_iota(jnp.int32, (BLOCK, BLOCK), 0)
        col = lax.broadcasted_iota(jnp.int32, (BLOCK, BLOCK), 1)
        ninf = jnp.float32(-jnp.inf)
        before = col >= row
        after = col <= row

    def window_tiles(p_ref, c_ref, n_ref, kh, a):
        cur = lambda j: c_ref[0, kh, j * BLOCK:(j + 1) * BLOCK]
        first = p_ref[0, kh] if a == 0 else cur(a - 1)
        last = n_ref[0, kh] if a == qb - 1 else cur(a + 1)
        return [first, cur(a), last]

    def scores(a, kh):
        if window:
            k = jnp.concatenate(window_tiles(kp_ref, kc_ref, kn_ref, kh, a) + [kx_ref[0, kh]], axis=0)
        else:
            k = kx_ref[0, kh]
        qs = []
        for g in range(GQA_GROUP):
            h = kh * GQA_GROUP + g
            qp = q_ref[0, a * BLOCK:(a + 1) * BLOCK, (h // 2) * LANES:(h // 2 + 1) * LANES]
            keep = low if h % 2 == 0 else jnp.logical_not(low)
            qs.append(jnp.where(keep, qp, jnp.zeros_like(qp)))
        qst = jnp.concatenate(qs, axis=0)
        return lax.dot_general(qst, k, (((1,), (1,)), ((), ())), preferred_element_type=F32)

    def finish(a, kh, s):
        if window:
            v = jnp.concatenate(window_tiles(vp_ref, vc_ref, vn_ref, kh, a) + [vx_ref[0, kh]], axis=0)
            ok_first = before & (i > 0) if a == 0 else before
            ok_last = after & (i < n_steps - 1) if a == qb - 1 else after
            b_first = jnp.where(ok_first, 0.0, ninf)
            b_last = jnp.where(ok_last, 0.0, ninf)
        else:
            v = vx_ref[0, kh]
        ps, ls = [], []
        for g in range(GQA_GROUP):
            sk = sink_ref[kh * GQA_GROUP + g] * LOG2E
            sg = s[g * BLOCK:(g + 1) * BLOCK]
            if window:
                sg = jnp.concatenate(
                    [sg[:, :BLOCK] + b_first, sg[:, BLOCK:2 * BLOCK],
                     sg[:, 2 * BLOCK:3 * BLOCK] + b_last, sg[:, 3 * BLOCK:]], axis=1)
            m = jnp.maximum(jnp.max(sg, axis=-1, keepdims=True), sk)
            p = jnp.exp2(sg - m)
            ls.append(jnp.sum(p, axis=-1, keepdims=True) + jnp.exp2(sk - m))
            ps.append(p.astype(BF16))
        o = _dot(jnp.concatenate(ps, axis=0), v)
        og = [o[g * BLOCK:(g + 1) * BLOCK] / ls[g] for g in range(GQA_GROUP)]
        for pr in range(GQA_GROUP // 2):
            c0 = kh * GQA_GROUP * HEAD_DIM + pr * LANES
            o_ref[0, a * BLOCK:(a + 1) * BLOCK, c0:c0 + LANES] = jnp.where(
                low, og[2 * pr], og[2 * pr + 1]).astype(BF16)

    units = [(a, kh) for a in range(qb) for kh in range(N_KV_HEADS)]
    ahead = 2
    pending = [scores(*u) for u in units[:ahead]]
    for n, u in enumerate(units):
        s = pending.pop(0)
        if n + ahead < len(units):
            pending.append(scores(*units[n + ahead]))
        finish(*u, s)


def _cast_operands(cast, grid_steps, flat_step):
    in_specs, out_specs, out_shapes = [], [], []
    for w, first, n in cast:
        per, every = n // grid_steps, 1
        while per % BF16_ROWS:
            per, every = 2 * per, 2 * every
        assert n % per == 0 and first % per == 0
        blk = (per, w.shape[1])
        in_specs.append(pl.BlockSpec(blk, lambda b, i, e=every, o=first // per: (o + flat_step(b, i) // e, 0)))
        out_specs.append(pl.BlockSpec(blk, lambda b, i, e=every: (flat_step(b, i) // e, 0)))
        out_shapes.append(jax.ShapeDtypeStruct((n, w.shape[1]), BF16))
    return in_specs, out_specs, out_shapes


def _attention(q, k, v, kx, vx, sink, window, cast=()):
    B, S, _ = q.shape
    nb = S // BLOCK
    qb = min(ATT_QBLOCKS, nb)
    rows = qb * BLOCK
    n_steps = S // rows
    L = kx.shape[2] // (B // kx.shape[0])
    cast_in, cast_out, cast_shapes = _cast_operands(cast, B * n_steps, lambda b, i: b * n_steps + i)
    qspec = pl.BlockSpec((1, rows, ATTN_WIDTH), lambda b, i: (b, i, 0))
    if kx.shape[0] == B:
        xspec = pl.BlockSpec((1, N_KV_HEADS, L, LANES), lambda b, i: (b, 0, 0, 0))
    else:
        xspec = pl.BlockSpec((1, N_KV_HEADS, L, LANES), lambda b, i: (0, 0, b, 0))
    smem = pl.BlockSpec(memory_space=pltpu.SMEM)
    if window:
        blk = (1, N_KV_HEADS, BLOCK, LANES)
        prev = pl.BlockSpec(blk, lambda b, i: (b, 0, jnp.maximum(i * qb - 1, 0), 0))
        cur = pl.BlockSpec((1, N_KV_HEADS, rows, LANES), lambda b, i: (b, 0, i, 0))
        nxt = pl.BlockSpec(blk, lambda b, i: (b, 0, jnp.minimum((i + 1) * qb, nb - 1), 0))
        in_specs = [smem, qspec, prev, cur, nxt, prev, cur, nxt, xspec, xspec]
        args = (sink, q, k, k, k, v, v, v, kx, vx)
    else:
        in_specs = [smem, qspec, xspec, xspec]
        args = (sink, q, kx, vx)
    out, *casted = pl.pallas_call(
        functools.partial(_att_kernel, n_steps=n_steps, qb=qb, window=window, n_cast=len(cast)),
        grid=(B, n_steps),
        in_specs=in_specs + cast_in,
        out_specs=[qspec] + cast_out,
        out_shape=[jax.ShapeDtypeStruct((B, S, ATTN_WIDTH), BF16)] + cast_shapes,
        compiler_params=_params("arbitrary", "arbitrary"),
        name="window_attn" if window else "ctx_attn",
    )(*args, *(w for w, _, _ in cast))
    return (out, casted) if cast else out


def _pool_fill(xs_ref, h_ref, hp_ref, hx_ref, g, sh, sc, lo, hi, tm, first_tile, last_tile):
    hal = POOL_HALO
    if lo == 0:
        xs_ref[0:hal] = jnp.where(first_tile, 0.0, _rms_mod(hp_ref[0], g, sh, sc))
    xs_ref[hal + lo:hal + hi] = _rms_mod(h_ref[0, lo:hi], g, sh, sc)
    if hi == tm:
        xs_ref[hal + tm:2 * hal + tm] = jnp.where(last_tile, 0.0, _rms_mod(hx_ref[0], g, sh, sc))
        xs_ref[2 * hal + tm:] = jnp.zeros((POOL_PAD, D_MODEL), F32)


def _pool_part(xs_ref, p_ref, ic_ref, pw_ref, r0, pr):
    hal = POOL_HALO
    ys = []
    for gi, w in enumerate(POOL_WINDOWS):
        cs = slice(gi * POOL_GROUP, (gi + 1) * POOL_GROUP)
        x = lambda off, n: xs_ref[r0 + off:r0 + off + n, cs]
        if w == 2:
            acc = x(hal - 1, pr) + x(hal, pr)
        else:
            n2, n4, n8 = pr + 3 * hal, pr + 2 * hal, pr + hal
            p_ref[0:n2] = x(0, n2) + x(1, n2)
            if w == 4:
                acc = p_ref[hal - 2:hal - 2 + pr] + p_ref[hal:hal + pr]
            else:
                p_ref[0:n4] = p_ref[0:n4] + p_ref[2:n4 + 2]
                if w == 8:
                    acc = p_ref[hal - 4:hal - 4 + pr] + p_ref[hal:hal + pr]
                else:
                    p_ref[0:n8] = p_ref[0:n8] + p_ref[4:n8 + 4]
                    acc = p_ref[0:pr] + p_ref[hal:hal + pr]
        yg = acc * ic_ref[r0:r0 + pr, gi:gi + 1] - x(hal, pr)
        ys.append(_dot(yg.astype(BF16), pw_ref[gi]))
    return jnp.concatenate(ys, axis=1)


def _ffn_parts(tm, mode):
    return max(1, tm // (FFN_POOL_PART_ROWS if mode == "pool" else FFN_PART_ROWS))


def _ffn_kernel(*refs, tm, seq, mode, final):
    refs = list(refs)
    take = lambda n: [refs.pop(0) for _ in range(n)]
    (h_ref,) = take(1)
    if mode == "proj":
        four_ref, attn_ref, wo_ref, g1_ref = take(4)
    elif mode == "pool":
        hp_ref, hx_ref, gm_ref, sh1_ref, sc1_ref, g1_ref, pw_ref, ps_ref, ic_ref = take(9)
    gn_ref, sh_ref, sc_ref, g2_ref, w1_ref, w3_ref, w2_ref = take(7)
    if final:
        (fg_ref,) = take(1)
    (o_ref,) = take(1)
    if mode == "pool":
        xs_ref, p_ref = take(2)
        first_tile = pl.program_id(1) == 0
        last_tile = pl.program_id(1) == seq // tm - 1
    n_parts = _ffn_parts(tm, mode)
    pr = tm // n_parts
    rows = [slice(p * pr, (p + 1) * pr) for p in range(n_parts)]

    def mixed(p):
        h = h_ref[0, rows[p]]
        if mode == "proj":
            y = (_dot(four_ref[0, rows[p]], wo_ref[:FOURIER_WIDTH])
                 + _dot(attn_ref[0, rows[p]], wo_ref[FOURIER_WIDTH:]))
            return h + _v(g1_ref) * y
        if mode == "pool":
            lo = 0 if p == 0 else p * pr + POOL_PAD
            hi = tm if p == n_parts - 1 else (p + 1) * pr + POOL_PAD
            _pool_fill(xs_ref, h_ref, hp_ref, hx_ref, gm_ref[...], _v(sh1_ref), _v(sc1_ref), lo, hi, tm,
                       first_tile, last_tile)
            y = _pool_part(xs_ref, p_ref, ic_ref, pw_ref, p * pr, pr) * ps_ref[...]
            return h + _v(g1_ref) * y
        return h

    def normed(h):
        return _rms_mod(h, gn_ref[...], _v(sh_ref), _v(sc_ref)).astype(BF16)

    def up(hn, c0, cn):
        return _dot(hn, w1_ref[0, :, c0:c0 + cn]), _dot(hn, w3_ref[0, :, c0:c0 + cn])

    c0, cn = FFN_CHUNKS[0]
    if mode == "pool":
        hs, hns, ab = [], [], []
        for p in range(n_parts):
            hs.append(mixed(p))
            hns.append(normed(hs[p]))
            ab.append(up(hns[p], c0, cn))
    else:
        hs = [mixed(p) for p in range(n_parts)]
        hns = [normed(h) for h in hs]
        ab = [up(hn, c0, cn) for hn in hns]
    accs = [None] * n_parts
    for ci, (c0, cn) in enumerate(FFN_CHUNKS):
        if ci > 0:
            ab = [up(hn, c0, cn) for hn in hns]
        for p, (a, b) in enumerate(ab):
            t = (jax.nn.silu(a) * b).astype(BF16)
            d = _dot(t, w2_ref[0, c0:c0 + cn, :])
            accs[p] = d if accs[p] is None else accs[p] + d
    for p, r in enumerate(rows):
        out = hs[p] + _v(g2_ref) * accs[p]
        if final:
            out = _rms(out, fg_ref[...])
        o_ref[0, r] = out


def _ffn(h, mix, gn, mods, layer, ctx, ffn_w, final_g, tm):
    B, S, _ = h.shape
    mode = "none" if mix is None else mix[0]
    vecs = [_mod_operand(mods, layer, k, ctx) for k in range(6)]
    vspec = [v[0] for v in vecs]
    row = pl.BlockSpec((1, D_MODEL), lambda b, i: (0, 0))
    tok = lambda w: pl.BlockSpec((1, tm, w), lambda b, i: (b, i, 0))
    whole = pl.BlockSpec(memory_space=pltpu.VMEM)
    in_specs = [tok(D_MODEL)]
    args = [h]
    scratch = []
    if mode == "proj":
        _, four, attn, w_out = mix
        in_specs += [tok(FOURIER_WIDTH), tok(ATTN_WIDTH), whole, vspec[2]]
        args += [four, attn, w_out, mods]
    elif mode == "pool":
        _, g_mix, pool_w, pool_scale = mix
        hb = tm // POOL_HALO
        halo = (1, POOL_HALO, D_MODEL)
        t = np.arange(S)
        inv_count = jnp.asarray(np.stack(
            [1.0 / (np.minimum(t + w // 2, S) - np.maximum(t - w // 2, 0)) for w in POOL_WINDOWS], axis=1), F32)
        in_specs += [
            pl.BlockSpec(halo, lambda b, i: (b, jnp.maximum(i * hb - 1, 0), 0)),
            pl.BlockSpec(halo, lambda b, i: (b, jnp.minimum((i + 1) * hb, S // POOL_HALO - 1), 0)),
            row, vspec[0], vspec[1], vspec[2],
            pl.BlockSpec((len(POOL_WINDOWS), POOL_GROUP, POOL_GROUP), lambda b, i: (0, 0, 0)),
            row,
            pl.BlockSpec((tm, len(POOL_WINDOWS)), lambda b, i: (i, 0)),
        ]
        args += [h, h, g_mix, mods, mods, mods, pool_w, pool_scale, inv_count]
        pr = tm // _ffn_parts(tm, mode)
        scratch = [pltpu.VMEM((tm + 2 * POOL_HALO + POOL_PAD, D_MODEL), F32),
                   pltpu.VMEM((pr + 3 * POOL_HALO, POOL_GROUP), F32)]
    in_specs += [row, vspec[3], vspec[4], vspec[5]] + [
        pl.BlockSpec((1,) + w.shape[1:], lambda b, i, n=w.shape[0]: (layer % n, 0, 0),
                     pipeline_mode=pl.Buffered(1)) for w in ffn_w]
    args += [gn, mods, mods, mods, *ffn_w]
    if final_g is not None:
        in_specs.append(row)
        args.append(final_g)
    return pl.pallas_call(
        functools.partial(_ffn_kernel, tm=tm, seq=S, mode=mode, final=final_g is not None),
        grid=(B, S // tm),
        in_specs=in_specs,
        out_specs=tok(D_MODEL),
        out_shape=jax.ShapeDtypeStruct((B, S, D_MODEL), F32),
        scratch_shapes=scratch,
        compiler_params=_params("arbitrary", "arbitrary"),
        name="ffn_" + mode,
    )(*args)


def _dft_cos_sin(n):
    idx = np.arange(n, dtype=np.int64)
    ang = 2.0 * np.pi * ((idx[:, None] * idx[None, :]) % n) / n
    return np.cos(ang), np.sin(ang)


def _rope_tables(n_tokens):
    rows = n_tokens // GRID_W
    row = jnp.repeat(jnp.arange(rows, dtype=F32), GRID_W)
    col = jnp.tile(jnp.arange(GRID_W, dtype=F32), rows)
    n_freq = HEAD_DIM // 4
    inv = ROPE_THETA ** (-jnp.arange(n_freq, dtype=F32) / n_freq)
    ang = jnp.concatenate([row[:, None] * inv[None], col[:, None] * inv[None]], axis=-1)
    cos = jnp.repeat(jnp.cos(ang), 2, axis=-1)
    sin = jnp.repeat(jnp.sin(ang), 2, axis=-1)
    sign = jnp.tile(jnp.asarray([-1.0, 1.0], F32), HEAD_DIM // 2)
    return jnp.tile(cos, (1, 2)), jnp.tile(sin * sign, (1, 2))


def _fourier_tables():
    r = FFT_RADIX
    c, s = _dft_cos_sin(r)
    w1 = np.block([[c, s], [-s, c]])
    n_pos = r * r
    idx = np.arange(r, dtype=np.int64)
    k = idx[:, None, None] + r * idx[None, :, None]
    ang = 2.0 * np.pi * ((k * idx[None, None, :]) % n_pos) / n_pos
    scale = (n_pos * FOURIER_HEAD_DIM) ** -0.5
    w2 = np.concatenate([np.cos(ang), np.sin(ang)], axis=2) * scale
    cc, sc = _dft_cos_sin(FOURIER_HEAD_DIM)
    wc = np.concatenate([cc, -sc], axis=1)
    f32 = lambda a: jnp.asarray(np.ascontiguousarray(a), F32)
    return f32(w1).astype(BF16), f32(w2).astype(BF16), f32(wc).astype(BF16)


def kernel(x, c, ctx, c_ctx, ada_w, ada_b, norm_mix_g, norm_ffn_g, mix_in_w, mix_out_w, attn_sink,
           pool_w, pool_scale, ffn_w1, ffn_w3, ffn_w2, final_g):
    B, S, _ = x.shape
    L = ctx.shape[1]
    tm_ffn = 2 * FFN_PART_ROWS
    tm_ctx = L
    tm_flat = min(B * L, 2 * FFN_PART_ROWS)

    cond = jnp.zeros((ADA_ROWS, D_MODEL), F32).at[:B].set(c).at[B].set(c_ctx)
    mods = _ada(cond, ada_w, ada_b)

    cosf, sinf = _rope_tables(S)
    cos_id = jnp.ones((B * L, LANES), F32)
    sin_id = jnp.zeros((B * L, LANES), F32)
    w1_dft, w2_dft, wc = _fourier_tables()
    cl, sl_ = _dft_cos_sin(L)
    w_ctx_dft = jnp.asarray(np.concatenate([cl, sl_], axis=1), F32).astype(BF16)

    last_ctx_reader = max(range(0, DEPTH, 2))
    h, hc = x, ctx
    for layer in range(DEPTH):
        update_ctx = layer < last_ctx_reader
        g_mix = norm_mix_g[layer].reshape(1, D_MODEL)
        g_ffn = norm_ffn_g[layer].reshape(1, D_MODEL)
        fin = final_g.reshape(1, D_MODEL) if layer == DEPTH - 1 else None
        j = layer // 2
        if layer % 2 == 0:
            w_in = mix_in_w[j].astype(BF16)
            w_out = mix_out_w[j].astype(BF16)
            sink = attn_sink[j]
            flat = lambda a: a.reshape(1, B * L, a.shape[-1])
            unflat = lambda a: a.reshape(B, L, a.shape[-1])
            zr_c, zi_c, q_c, k_c, v_c = _in_proj(flat(hc), g_mix, mods, layer, True, w_in, cos_id, sin_id, wc,
                                                 tm_flat)
            zr, zi, q, k, v = _in_proj(h, g_mix, mods, layer, False, w_in, cosf, sinf, wc, 4 * IN_PART_ROWS)
            four = _fourier_4096(zr, zi, w1_dft, w2_dft)
            f32_w = (ffn_w1, ffn_w3, ffn_w2)
            cast = [(w.reshape(-1, w.shape[-1]), layer * w.shape[1], 2 * w.shape[1]) for w in f32_w]
            attn, casted = _attention(q, k, v, k_c, v_c, sink, True, cast=cast)
            ffn_w = tuple(cw.reshape(2, *w.shape[1:]) for cw, w in zip(casted, f32_w))
            h = _ffn(h, ("proj", four, attn, w_out), g_ffn, mods, layer, False, ffn_w, fin, tm_ffn)
            if update_ctx:
                four_c = _fourier_small(unflat(zr_c), unflat(zi_c), w_ctx_dft)
                attn_c = _attention(unflat(q_c), None, None, k_c, v_c, sink, False)
                hc = unflat(_ffn(flat(hc), ("proj", flat(four_c), flat(attn_c), w_out), g_ffn, mods, layer, True,
                                 ffn_w, None, tm_flat))
        else:
            pw = pool_w[j].astype(BF16)
            psc = pool_scale[j].reshape(1, D_MODEL)
            h = _ffn(h, ("pool", g_mix, pw, psc), g_ffn, mods, layer, False, ffn_w, fin, tm_ffn)
            if update_ctx:
                hc = _ffn(hc, ("pool", g_mix, pw, psc), g_ffn, mods, layer, True, ffn_w, None, tm_ctx)
    return h
```

```python
import functools

import numpy as np
import jax
import jax.numpy as jnp
from jax import lax
from jax.experimental import pallas as pl
from jax.experimental.pallas import tpu as pltpu

F32 = jnp.float32
BF16 = jnp.bfloat16

D_MODEL = 1024
DEPTH = 4
GRID_W = 64
EPS = 1e-6
FOURIER_HEADS = 4
FOURIER_HEAD_DIM = 128
FOURIER_WIDTH = 512
HEAD_DIM = 64
N_Q_HEADS = 8
N_KV_HEADS = 2
GQA_GROUP = 4
ATTN_WIDTH = 512
KV_WIDTH = 128
Q_END = 1024
IN_WIDTH = 1280
BLOCK = 128
ATT_QBLOCKS = 8
ROPE_THETA = 10000.0
POOL_WINDOWS = (2, 4, 8, 16)
POOL_GROUP = 256
POOL_HALO = 8
POOL_PAD = 2 * POOL_HALO
FFN_HIDDEN = 2816
FFN_CHUNKS = ((0, 1024), (1024, 1024), (2048, 768))
FFN_PART_ROWS = 512
FFN_POOL_PART_ROWS = 256
IN_PART_ROWS = 512
FFT_RADIX = 64
FFT_ROWS = 16
LANES = 128
BF16_ROWS = 16
LOG2E = 1.4426950408889634
ADA_ROWS = 16
ADA_CTX_ROW = 8
VMEM_LIMIT = 56 * 1024 * 1024


def _params(*sem):
    return pltpu.CompilerParams(dimension_semantics=sem, vmem_limit_bytes=VMEM_LIMIT)


def _rms(x, g):
    ms = jnp.mean(x * x, axis=-1, keepdims=True)
    return x * lax.rsqrt(ms + EPS) * g


def _rms_mod(x, g, shift, scale):
    return _rms(x, g) * (1.0 + scale) + shift


def _dot(a, b):
    return jnp.dot(a, b, preferred_element_type=F32)


def _ada_kernel(c_ref, w_ref, b_ref, o_ref):
    a = jax.nn.silu(c_ref[...])
    w = w_ref[0]
    a_hi = a.astype(BF16)
    a_lo = (a - a_hi.astype(F32)).astype(BF16)
    w_hi = w.astype(BF16)
    w_lo = (w - w_hi.astype(F32)).astype(BF16)
    m2 = _dot(jnp.concatenate([a_hi, a_lo], axis=0), w_hi)
    m = m2[:ADA_ROWS] + m2[ADA_ROWS:] + _dot(a_hi, w_lo) + b_ref[0]
    for r in range(ADA_ROWS):
        o_ref[0, 0, r] = m[r:r + 1]


def _ada(cond, ada_w, ada_b):
    n_chunks = ada_w.shape[-1] // D_MODEL
    return pl.pallas_call(
        _ada_kernel,
        grid=(DEPTH, n_chunks),
        in_specs=[
            pl.BlockSpec((ADA_ROWS, D_MODEL), lambda l, n: (0, 0)),
            pl.BlockSpec((1, D_MODEL, D_MODEL), lambda l, n: (l, 0, n)),
            pl.BlockSpec((1, 1, D_MODEL), lambda l, n: (l, 0, n)),
        ],
        out_specs=pl.BlockSpec((1, 1, ADA_ROWS, 1, D_MODEL), lambda l, n: (l, n, 0, 0, 0)),
        out_shape=jax.ShapeDtypeStruct((DEPTH, n_chunks, ADA_ROWS, 1, D_MODEL), F32),
        compiler_params=_params("arbitrary", "arbitrary"),
        name="ada_mod",
    )(cond, ada_w, ada_b.reshape(DEPTH, 1, ada_w.shape[-1]))


def _mod_operand(mods, layer, chunk, ctx):
    spec = pl.BlockSpec((1, 1, 1, 1, D_MODEL),
                        lambda b, i: (layer, chunk, ADA_CTX_ROW if ctx else b, 0, 0))
    return spec, mods


def _v(ref):
    return ref[0, 0, 0]


def _in_kernel(h_ref, g_ref, sh_ref, sc_ref, w_ref, cos_ref, sin_ref, wc_ref,
               zr_ref, zi_ref, q_ref, k_ref, v_ref, *, tm):
    n_parts = max(1, tm // IN_PART_ROWS)
    pr = tm // n_parts
    rows = [slice(p * pr, (p + 1) * pr) for p in range(n_parts)]
    projs = [_dot(_rms_mod(h_ref[0, r], g_ref[...], _v(sh_ref), _v(sc_ref)).astype(BF16), w_ref[...])
             for r in rows]
    wc = wc_ref[...]
    zs = [[_dot(proj[:, hh * LANES:(hh + 1) * LANES].astype(BF16), wc) for hh in range(FOURIER_HEADS)]
          for proj in projs]
    lane = lax.broadcasted_iota(jnp.int32, (pr, LANES), 1)
    even = (lane & 1) == 0
    low = lane < HEAD_DIM
    for r, proj, z4 in zip(rows, projs, zs):
        cosf = cos_ref[r]
        sinf = sin_ref[r]

        def rope(t):
            partner = jnp.where(even, pltpu.roll(t, LANES - 1, 1), pltpu.roll(t, 1, 1))
            return t * cosf + partner * sinf

        for hh, z in enumerate(z4):
            sl = slice(hh * LANES, (hh + 1) * LANES)
            zr_ref[0, r, sl] = z[:, :LANES].astype(BF16)
            zi_ref[0, r, sl] = z[:, LANES:].astype(BF16)
        for p in range(ATTN_WIDTH // LANES):
            sl = slice(p * LANES, (p + 1) * LANES)
            qq = proj[:, FOURIER_WIDTH + p * LANES:FOURIER_WIDTH + (p + 1) * LANES]
            q_ref[0, r, sl] = (rope(qq) * (HEAD_DIM ** -0.5 * LOG2E)).astype(BF16)
        kk = rope(proj[:, Q_END:Q_END + KV_WIDTH])
        vv = proj[:, Q_END + KV_WIDTH:]
        k_sw = pltpu.roll(kk, HEAD_DIM, 1)
        v_sw = pltpu.roll(vv, HEAD_DIM, 1)
        k_ref[0, 0, r] = jnp.where(low, kk, k_sw).astype(BF16)
        k_ref[0, 1, r] = jnp.where(low, k_sw, kk).astype(BF16)
        v_ref[0, 0, r] = jnp.where(low, vv, v_sw).astype(BF16)
        v_ref[0, 1, r] = jnp.where(low, v_sw, vv).astype(BF16)


def _in_proj(h, g, mods, layer, ctx, w_in, cosf, sinf, wc, tm):
    B, S, _ = h.shape
    (shift_spec, shift), (scale_spec, scale) = (_mod_operand(mods, layer, k, ctx) for k in (0, 1))
    tok = lambda w: pl.BlockSpec((1, tm, w), lambda b, i: (b, i, 0))
    kv = pl.BlockSpec((1, N_KV_HEADS, tm, LANES), lambda b, i: (b, 0, i, 0))
    return pl.pallas_call(
        functools.partial(_in_kernel, tm=tm),
        grid=(B, S // tm),
        in_specs=[
            tok(D_MODEL),
            pl.BlockSpec((1, D_MODEL), lambda b, i: (0, 0)),
            shift_spec, scale_spec,
            pl.BlockSpec((D_MODEL, IN_WIDTH), lambda b, i: (0, 0)),
            pl.BlockSpec((tm, LANES), lambda b, i: (i, 0)),
            pl.BlockSpec((tm, LANES), lambda b, i: (i, 0)),
            pl.BlockSpec((FOURIER_HEAD_DIM, 2 * FOURIER_HEAD_DIM), lambda b, i: (0, 0)),
        ],
        out_specs=[tok(FOURIER_WIDTH), tok(FOURIER_WIDTH), tok(ATTN_WIDTH), kv, kv],
        out_shape=[
            jax.ShapeDtypeStruct((B, S, FOURIER_WIDTH), BF16),
            jax.ShapeDtypeStruct((B, S, FOURIER_WIDTH), BF16),
            jax.ShapeDtypeStruct((B, S, ATTN_WIDTH), BF16),
            jax.ShapeDtypeStruct((B, N_KV_HEADS, S, LANES), BF16),
            jax.ShapeDtypeStruct((B, N_KV_HEADS, S, LANES), BF16),
        ],
        compiler_params=_params("arbitrary", "arbitrary"),
        name="in_proj",
    )(h, g, shift, scale, w_in, cosf, sinf, wc)


def _fft_kernel(zr_ref, zi_ref, w1_ref, w2_ref, o_ref, y_ref, *, t):
    r = FFT_RADIX
    w1 = w1_ref[...]
    for c in range(r // t):
        cs = slice(c * t, (c + 1) * t)
        xr = jnp.swapaxes(zr_ref[0, :, cs, :], 0, 1)
        xi = jnp.swapaxes(zi_ref[0, :, cs, :], 0, 1)
        outs_r, outs_i = [], []
        for j in range(t):
            y = _dot(w1, jnp.concatenate([xr[j], xi[j]], axis=0))
            outs_r.append(y[:r].astype(BF16))
            outs_i.append(y[r:].astype(BF16))
        y_ref[0, :, cs, :] = jnp.swapaxes(jnp.stack(outs_r, axis=0), 0, 1)
        y_ref[1, :, cs, :] = jnp.swapaxes(jnp.stack(outs_i, axis=0), 0, 1)
    for c in range(r // t):
        outs = []
        for j in range(t):
            k1 = c * t + j
            x = jnp.concatenate([y_ref[0, k1], y_ref[1, k1]], axis=0)
            outs.append(_dot(w2_ref[k1], x).astype(BF16))
        o_ref[0, :, c * t:(c + 1) * t, :] = jnp.swapaxes(jnp.stack(outs, axis=0), 0, 1)


def _fourier_4096(zr, zi, w1, w2):
    B, S, W = zr.shape
    r = FFT_RADIX
    zr4 = zr.reshape(B, r, r, W)
    zi4 = zi.reshape(B, r, r, W)
    blk = pl.BlockSpec((1, r, r, W), lambda b: (b, 0, 0, 0))
    out = pl.pallas_call(
        functools.partial(_fft_kernel, t=FFT_ROWS),
        grid=(B,),
        in_specs=[blk, blk, pl.BlockSpec((2 * r, 2 * r), lambda b: (0, 0)),
                  pl.BlockSpec((r, r, 2 * r), lambda b: (0, 0, 0))],
        out_specs=blk,
        out_shape=jax.ShapeDtypeStruct((B, r, r, W), BF16),
        scratch_shapes=[pltpu.VMEM((2, r, r, W), BF16)],
        compiler_params=_params("arbitrary"),
        name="fft_4096",
    )(zr4, zi4, w1, w2)
    return out.reshape(B, S, W)


def _dft_small_kernel(zr_ref, zi_ref, w_ref, o_ref, *, scale):
    x = jnp.concatenate([zr_ref[0], zi_ref[0]], axis=0)
    o_ref[0] = (_dot(w_ref[...], x) * scale).astype(BF16)


def _fourier_small(zr, zi, w):
    B, L, W = zr.shape
    scale = float((L * FOURIER_HEAD_DIM) ** -0.5)
    blk = pl.BlockSpec((1, L, W), lambda b: (b, 0, 0))
    return pl.pallas_call(
        functools.partial(_dft_small_kernel, scale=scale),
        grid=(B,),
        in_specs=[blk, blk, pl.BlockSpec((L, 2 * L), lambda b: (0, 0))],
        out_specs=blk,
        out_shape=jax.ShapeDtypeStruct((B, L, W), BF16),
        compiler_params=_params("arbitrary"),
        name="dft_ctx",
    )(zr, zi, w)


def _att_kernel(sink_ref, q_ref, *refs, n_steps, qb, window, n_cast):
    n_kv = 8 if window else 2
    kv_refs, cast_in = refs[:n_kv], refs[n_kv:n_kv + n_cast]
    o_ref, cast_out = refs[n_kv + n_cast], refs[n_kv + n_cast + 1:]
    if window:
        kp_ref, kc_ref, kn_ref, vp_ref, vc_ref, vn_ref, kx_ref, vx_ref = kv_refs
    else:
        kx_ref, vx_ref = kv_refs
    for src, dst in zip(cast_in, cast_out):
        dst[...] = src[...].astype(BF16)
    i = pl.program_id(1)
    lane = lax.broadcasted_iota(jnp.int32, (BLOCK, LANES), 1)
    low = lane < HEAD_DIM
    if window:
        row = lax.broadcasted_iota(jnp.int32, (BLOCK, BLOCK), 0)
        col = lax.broadcasted_iota(jnp.int32, (BLOCK, BLOCK), 1)
        ninf = jnp.float32(-jnp.inf)
        before = col >= row
        after = col <= row

    def window_tiles(p_ref, c_ref, n_ref, kh, a):
        cur = lambda j: c_ref[0, kh, j * BLOCK:(j + 1) * BLOCK]
        first = p_ref[0, kh] if a == 0 else cur(a - 1)
        last = n_ref[0, kh] if a == qb - 1 else cur(a + 1)
        return [first, cur(a), last]

    def scores(a, kh):
        if window:
            k = jnp.concatenate(window_tiles(kp_ref, kc_ref, kn_ref, kh, a) + [kx_ref[0, kh]], axis=0)
        else:
            k = kx_ref[0, kh]
        qs = []
        for g in range(GQA_GROUP):
            h = kh * GQA_GROUP + g
            qp = q_ref[0, a * BLOCK:(a + 1) * BLOCK, (h // 2) * LANES:(h // 2 + 1) * LANES]
            keep = low if h % 2 == 0 else jnp.logical_not(low)
            qs.append(jnp.where(keep, qp, jnp.zeros_like(qp)))
        qst = jnp.concatenate(qs, axis=0)
        return lax.dot_general(qst, k, (((1,), (1,)), ((), ())), preferred_element_type=F32)

    def finish(a, kh, s):
        if window:
            v = jnp.concatenate(window_tiles(vp_ref, vc_ref, vn_ref, kh, a) + [vx_ref[0, kh]], axis=0)
            ok_first = before & (i > 0) if a == 0 else before
            ok_last = after & (i < n_steps - 1) if a == qb - 1 else after
            b_first = jnp.where(ok_first, 0.0, ninf)
            b_last = jnp.where(ok_last, 0.0, ninf)
        else:
            v = vx_ref[0, kh]
        ps, ls = [], []
        for g in range(GQA_GROUP):
            sk = sink_ref[kh * GQA_GROUP + g] * LOG2E
            sg = s[g * BLOCK:(g + 1) * BLOCK]
            if window:
                sg = jnp.concatenate(
                    [sg[:, :BLOCK] + b_first, sg[:, BLOCK:2 * BLOCK],
                     sg[:, 2 * BLOCK:3 * BLOCK] + b_last, sg[:, 3 * BLOCK:]], axis=1)
            m = jnp.maximum(jnp.max(sg, axis=-1, keepdims=True), sk)
            p = jnp.exp2(sg - m)
            ls.append(jnp.sum(p, axis=-1, keepdims=True) + jnp.exp2(sk - m))
            ps.append(p.astype(BF16))
        o = _dot(jnp.concatenate(ps, axis=0), v)
        og = [o[g * BLOCK:(g + 1) * BLOCK] / ls[g] for g in range(GQA_GROUP)]
        for pr in range(GQA_GROUP // 2):
            c0 = kh * GQA_GROUP * HEAD_DIM + pr * LANES
            o_ref[0, a * BLOCK:(a + 1) * BLOCK, c0:c0 + LANES] = jnp.where(
                low, og[2 * pr], og[2 * pr + 1]).astype(BF16)

    units = [(a, kh) for a in range(qb) for kh in range(N_KV_HEADS)]
    ahead = 2
    pending = [scores(*u) for u in units[:ahead]]
    for n, u in enumerate(units):
        s = pending.pop(0)
        if n + ahead < len(units):
            pending.append(scores(*units[n + ahead]))
        finish(*u, s)


def _cast_operands(cast, grid_steps, flat_step):
    in_specs, out_specs, out_shapes = [], [], []
    for w, first, n in cast:
        per, every = n // grid_steps, 1
        while per % BF16_ROWS:
            per, every = 2 * per, 2 * every
        assert n % per == 0 and first % per == 0
        blk = (per, w.shape[1])
        in_specs.append(pl.BlockSpec(blk, lambda b, i, e=every, o=first // per: (o + flat_step(b, i) // e, 0)))
        out_specs.append(pl.BlockSpec(blk, lambda b, i, e=every: (flat_step(b, i) // e, 0)))
        out_shapes.append(jax.ShapeDtypeStruct((n, w.shape[1]), BF16))
    return in_specs, out_specs, out_shapes


def _attention(q, k, v, kx, vx, sink, window, cast=()):
    B, S, _ = q.shape
    nb = S // BLOCK
    qb = min(ATT_QBLOCKS, nb)
    rows = qb * BLOCK
    n_steps = S // rows
    L = kx.shape[2] // (B // kx.shape[0])
    cast_in, cast_out, cast_shapes = _cast_operands(cast, B * n_steps, lambda b, i: b * n_steps + i)
    qspec = pl.BlockSpec((1, rows, ATTN_WIDTH), lambda b, i: (b, i, 0))
    if kx.shape[0] == B:
        xspec = pl.BlockSpec((1, N_KV_HEADS, L, LANES), lambda b, i: (b, 0, 0, 0))
    else:
        xspec = pl.BlockSpec((1, N_KV_HEADS, L, LANES), lambda b, i: (0, 0, b, 0))
    smem = pl.BlockSpec(memory_space=pltpu.SMEM)
    if window:
        blk = (1, N_KV_HEADS, BLOCK, LANES)
        prev = pl.BlockSpec(blk, lambda b, i: (b, 0, jnp.maximum(i * qb - 1, 0), 0))
        cur = pl.BlockSpec((1, N_KV_HEADS, rows, LANES), lambda b, i: (b, 0, i, 0))
        nxt = pl.BlockSpec(blk, lambda b, i: (b, 0, jnp.minimum((i + 1) * qb, nb - 1), 0))
        in_specs = [smem, qspec, prev, cur, nxt, prev, cur, nxt, xspec, xspec]
        args = (sink, q, k, k, k, v, v, v, kx, vx)
    else:
        in_specs = [smem, qspec, xspec, xspec]
        args = (sink, q, kx, vx)
    out, *casted = pl.pallas_call(
        functools.partial(_att_kernel, n_steps=n_steps, qb=qb, window=window, n_cast=len(cast)),
        grid=(B, n_steps),
        in_specs=in_specs + cast_in,
        out_specs=[qspec] + cast_out,
        out_shape=[jax.ShapeDtypeStruct((B, S, ATTN_WIDTH), BF16)] + cast_shapes,
        compiler_params=_params("arbitrary", "arbitrary"),
        name="window_attn" if window else "ctx_attn",
    )(*args, *(w for w, _, _ in cast))
    return (out, casted) if cast else out


def _pool_fill(xs_ref, h_ref, hp_ref, hx_ref, g, sh, sc, lo, hi, tm, first_tile, last_tile):
    hal = POOL_HALO
    if lo == 0:
        xs_ref[0:hal] = jnp.where(first_tile, 0.0, _rms_mod(hp_ref[0], g, sh, sc))
    xs_ref[hal + lo:hal + hi] = _rms_mod(h_ref[0, lo:hi], g, sh, sc)
    if hi == tm:
        xs_ref[hal + tm:2 * hal + tm] = jnp.where(last_tile, 0.0, _rms_mod(hx_ref[0], g, sh, sc))
        xs_ref[2 * hal + tm:] = jnp.zeros((POOL_PAD, D_MODEL), F32)


def _pool_part(xs_ref, p_ref, ic_ref, pw_ref, r0, pr):
    hal = POOL_HALO
    ys = []
    for gi, w in enumerate(POOL_WINDOWS):
        cs = slice(gi * POOL_GROUP, (gi + 1) * POOL_GROUP)
        x = lambda off, n: xs_ref[r0 + off:r0 + off + n, cs]
        if w == 2:
            acc = x(hal - 1, pr) + x(hal, pr)
        else:
            n2, n4, n8 = pr + 3 * hal, pr + 2 * hal, pr + hal
            p_ref[0:n2] = x(0, n2) + x(1, n2)
            if w == 4:
                acc = p_ref[hal - 2:hal - 2 + pr] + p_ref[hal:hal + pr]
            else:
                p_ref[0:n4] = p_ref[0:n4] + p_ref[2:n4 + 2]
                if w == 8:
                    acc = p_ref[hal - 4:hal - 4 + pr] + p_ref[hal:hal + pr]
                else:
                    p_ref[0:n8] = p_ref[0:n8] + p_ref[4:n8 + 4]
                    acc = p_ref[0:pr] + p_ref[hal:hal + pr]
        yg = acc * ic_ref[r0:r0 + pr, gi:gi + 1] - x(hal, pr)
        ys.append(_dot(yg.astype(BF16), pw_ref[gi]))
    return jnp.concatenate(ys, axis=1)


def _ffn_parts(tm, mode):
    return max(1, tm // (FFN_POOL_PART_ROWS if mode == "pool" else FFN_PART_ROWS))


def _ffn_kernel(*refs, tm, seq, mode, final):
    refs = list(refs)
    take = lambda n: [refs.pop(0) for _ in range(n)]
    (h_ref,) = take(1)
    if mode == "proj":
        four_ref, attn_ref, wo_ref, g1_ref = take(4)
    elif mode == "pool":
        hp_ref, hx_ref, gm_ref, sh1_ref, sc1_ref, g1_ref, pw_ref, ps_ref, ic_ref = take(9)
    gn_ref, sh_ref, sc_ref, g2_ref, w1_ref, w3_ref, w2_ref = take(7)
    if final:
        (fg_ref,) = take(1)
    (o_ref,) = take(1)
    if mode == "pool":
        xs_ref, p_ref = take(2)
        first_tile = pl.program_id(1) == 0
        last_tile = pl.program_id(1) == seq // tm - 1
    n_parts = _ffn_parts(tm, mode)
    pr = tm // n_parts
    rows = [slice(p * pr, (p + 1) * pr) for p in range(n_parts)]

    def mixed(p):
        h = h_ref[0, rows[p]]
        if mode == "proj":
            y = (_dot(four_ref[0, rows[p]], wo_ref[:FOURIER_WIDTH])
                 + _dot(attn_ref[0, rows[p]], wo_ref[FOURIER_WIDTH:]))
            return h + _v(g1_ref) * y
        if mode == "pool":
            lo = 0 if p == 0 else p * pr + POOL_PAD
            hi = tm if p == n_parts - 1 else (p + 1) * pr + POOL_PAD
            _pool_fill(xs_ref, h_ref, hp_ref, hx_ref, gm_ref[...], _v(sh1_ref), _v(sc1_ref), lo, hi, tm,
                       first_tile, last_tile)
            y = _pool_part(xs_ref, p_ref, ic_ref, pw_ref, p * pr, pr) * ps_ref[...]
            return h + _v(g1_ref) * y
        return h

    def normed(h):
        return _rms_mod(h, gn_ref[...], _v(sh_ref), _v(sc_ref)).astype(BF16)

    def up(hn, c0, cn):
        return _dot(hn, w1_ref[0, :, c0:c0 + cn]), _dot(hn, w3_ref[0, :, c0:c0 + cn])

    c0, cn = FFN_CHUNKS[0]
    if mode == "pool":
        hs, hns, ab = [], [], []
        for p in range(n_parts):
            hs.append(mixed(p))
            hns.append(normed(hs[p]))
            ab.append(up(hns[p], c0, cn))
    else:
        hs = [mixed(p) for p in range(n_parts)]
        hns = [normed(h) for h in hs]
        ab = [up(hn, c0, cn) for hn in hns]
    accs = [None] * n_parts
    for ci, (c0, cn) in enumerate(FFN_CHUNKS):
        if ci > 0:
            ab = [up(hn, c0, cn) for hn in hns]
        for p, (a, b) in enumerate(ab):
            t = (jax.nn.silu(a) * b).astype(BF16)
            d = _dot(t, w2_ref[0, c0:c0 + cn, :])
            accs[p] = d if accs[p] is None else accs[p] + d
    for p, r in enumerate(rows):
        out = hs[p] + _v(g2_ref) * accs[p]
        if final:
            out = _rms(out, fg_ref[...])
        o_ref[0, r] = out


def _ffn(h, mix, gn, mods, layer, ctx, ffn_w, final_g, tm):
    B, S, _ = h.shape
    mode = "none" if mix is None else mix[0]
    vecs = [_mod_operand(mods, layer, k, ctx) for k in range(6)]
    vspec = [v[0] for v in vecs]
    row = pl.BlockSpec((1, D_MODEL), lambda b, i: (0, 0))
    tok = lambda w: pl.BlockSpec((1, tm, w), lambda b, i: (b, i, 0))
    whole = pl.BlockSpec(memory_space=pltpu.VMEM)
    in_specs = [tok(D_MODEL)]
    args = [h]
    scratch = []
    if mode == "proj":
        _, four, attn, w_out = mix
        in_specs += [tok(FOURIER_WIDTH), tok(ATTN_WIDTH), whole, vspec[2]]
        args += [four, attn, w_out, mods]
    elif mode == "pool":
        _, g_mix, pool_w, pool_scale = mix
        hb = tm // POOL_HALO
        halo = (1, POOL_HALO, D_MODEL)
        t = np.arange(S)
        inv_count = jnp.asarray(np.stack(
            [1.0 / (np.minimum(t + w // 2, S) - np.maximum(t - w // 2, 0)) for w in POOL_WINDOWS], axis=1), F32)
        in_specs += [
            pl.BlockSpec(halo, lambda b, i: (b, jnp.maximum(i * hb - 1, 0), 0)),
            pl.BlockSpec(halo, lambda b, i: (b, jnp.minimum((i + 1) * hb, S // POOL_HALO - 1), 0)),
            row, vspec[0], vspec[1], vspec[2],
            pl.BlockSpec((len(POOL_WINDOWS), POOL_GROUP, POOL_GROUP), lambda b, i: (0, 0, 0)),
            row,
            pl.BlockSpec((tm, len(POOL_WINDOWS)), lambda b, i: (i, 0)),
        ]
        args += [h, h, g_mix, mods, mods, mods, pool_w, pool_scale, inv_count]
        pr = tm // _ffn_parts(tm, mode)
        scratch = [pltpu.VMEM((tm + 2 * POOL_HALO + POOL_PAD, D_MODEL), F32),
                   pltpu.VMEM((pr + 3 * POOL_HALO, POOL_GROUP), F32)]
    in_specs += [row, vspec[3], vspec[4], vspec[5]] + [
        pl.BlockSpec((1,) + w.shape[1:], lambda b, i, n=w.shape[0]: (layer % n, 0, 0),
                     pipeline_mode=pl.Buffered(1)) for w in ffn_w]
    args += [gn, mods, mods, mods, *ffn_w]
    if final_g is not None:
        in_specs.append(row)
        args.append(final_g)
    return pl.pallas_call(
        functools.partial(_ffn_kernel, tm=tm, seq=S, mode=mode, final=final_g is not None),
        grid=(B, S // tm),
        in_specs=in_specs,
        out_specs=tok(D_MODEL),
        out_shape=jax.ShapeDtypeStruct((B, S, D_MODEL), F32),
        scratch_shapes=scratch,
        compiler_params=_params("arbitrary", "arbitrary"),
        name="ffn_" + mode,
    )(*args)


def _dft_cos_sin(n):
    idx = np.arange(n, dtype=np.int64)
    ang = 2.0 * np.pi * ((idx[:, None] * idx[None, :]) % n) / n
    return np.cos(ang), np.sin(ang)


def _rope_tables(n_tokens):
    rows = n_tokens // GRID_W
    row = jnp.repeat(jnp.arange(rows, dtype=F32), GRID_W)
    col = jnp.tile(jnp.arange(GRID_W, dtype=F32), rows)
    n_freq = HEAD_DIM // 4
    inv = ROPE_THETA ** (-jnp.arange(n_freq, dtype=F32) / n_freq)
    ang = jnp.concatenate([row[:, None] * inv[None], col[:, None] * inv[None]], axis=-1)
    cos = jnp.repeat(jnp.cos(ang), 2, axis=-1)
    sin = jnp.repeat(jnp.sin(ang), 2, axis=-1)
    sign = jnp.tile(jnp.asarray([-1.0, 1.0], F32), HEAD_DIM // 2)
    return jnp.tile(cos, (1, 2)), jnp.tile(sin * sign, (1, 2))


def _fourier_tables():
    r = FFT_RADIX
    c, s = _dft_cos_sin(r)
    w1 = np.block([[c, s], [-s, c]])
    n_pos = r * r
    idx = np.arange(r, dtype=np.int64)
    k = idx[:, None, None] + r * idx[None, :, None]
    ang = 2.0 * np.pi * ((k * idx[None, None, :]) % n_pos) / n_pos
    scale = (n_pos * FOURIER_HEAD_DIM) ** -0.5
    w2 = np.concatenate([np.cos(ang), np.sin(ang)], axis=2) * scale
    cc, sc = _dft_cos_sin(FOURIER_HEAD_DIM)
    wc = np.concatenate([cc, -sc], axis=1)
    f32 = lambda a: jnp.asarray(np.ascontiguousarray(a), F32)
    return f32(w1).astype(BF16), f32(w2).astype(BF16), f32(wc).astype(BF16)


def kernel(x, c, ctx, c_ctx, ada_w, ada_b, norm_mix_g, norm_ffn_g, mix_in_w, mix_out_w, attn_sink,
           pool_w, pool_scale, ffn_w1, ffn_w3, ffn_w2, final_g):
    B, S, _ = x.shape
    L = ctx.shape[1]
    tm_ffn = 2 * FFN_PART_ROWS
    tm_ctx = L
    tm_flat = min(B * L, 2 * FFN_PART_ROWS)

    cond = jnp.zeros((ADA_ROWS, D_MODEL), F32).at[:B].set(c).at[B].set(c_ctx)
    mods = _ada(cond, ada_w, ada_b)

    cosf, sinf = _rope_tables(S)
    cos_id = jnp.ones((B * L, LANES), F32)
    sin_id = jnp.zeros((B * L, LANES), F32)
    w1_dft, w2_dft, wc = _fourier_tables()
    cl, sl_ = _dft_cos_sin(L)
    w_ctx_dft = jnp.asarray(np.concatenate([cl, sl_], axis=1), F32).astype(BF16)

    last_ctx_reader = max(range(0, DEPTH, 2))
    h, hc = x, ctx
    for layer in range(DEPTH):
        update_ctx = layer < last_ctx_reader
        g_mix = norm_mix_g[layer].reshape(1, D_MODEL)
        g_ffn = norm_ffn_g[layer].reshape(1, D_MODEL)
        fin = final_g.reshape(1, D_MODEL) if layer == DEPTH - 1 else None
        j = layer // 2
        if layer % 2 == 0:
            w_in = mix_in_w[j].astype(BF16)
            w_out = mix_out_w[j].astype(BF16)
            sink = attn_sink[j]
            flat = lambda a: a.reshape(1, B * L, a.shape[-1])
            unflat = lambda a: a.reshape(B, L, a.shape[-1])
            zr_c, zi_c, q_c, k_c, v_c = _in_proj(flat(hc), g_mix, mods, layer, True, w_in, cos_id, sin_id, wc,
                                                 tm_flat)
            zr, zi, q, k, v = _in_proj(h, g_mix, mods, layer, False, w_in, cosf, sinf, wc, 4 * IN_PART_ROWS)
            four = _fourier_4096(zr, zi, w1_dft, w2_dft)
            f32_w = (ffn_w1, ffn_w3, ffn_w2)
            cast = [(w.reshape(-1, w.shape[-1]), layer * w.shape[1], 2 * w.shape[1]) for w in f32_w]
            attn, casted = _attention(q, k, v, k_c, v_c, sink, True, cast=cast)
            ffn_w = tuple(cw.reshape(2, *w.shape[1:]) for cw, w in zip(casted, f32_w))
            h = _ffn(h, ("proj", four, attn, w_out), g_ffn, mods, layer, False, ffn_w, fin, tm_ffn)
            if update_ctx:
                four_c = _fourier_small(unflat(zr_c), unflat(zi_c), w_ctx_dft)
                attn_c = _attention(unflat(q_c), None, None, k_c, v_c, sink, False)
                hc = unflat(_ffn(flat(hc), ("proj", flat(four_c), flat(attn_c), w_out), g_ffn, mods, layer, True,
                                 ffn_w, None, tm_flat))
        else:
            pw = pool_w[j].astype(BF16)
            psc = pool_scale[j].reshape(1, D_MODEL)
            h = _ffn(h, ("pool", g_mix, pw, psc), g_ffn, mods, layer, False, ffn_w, fin, tm_ffn)
            if update_ctx:
                hc = _ffn(hc, ("pool", g_mix, pw, psc), g_ffn, mods, layer, True, ffn_w, None, tm_ctx)
    return h
```

```python
import functools

import numpy as np
import jax
import jax.numpy as jnp
from jax import lax
from jax.experimental import pallas as pl
from jax.experimental.pallas import tpu as pltpu

F32 = jnp.float32
BF16 = jnp.bfloat16

D_MODEL = 1024
DEPTH = 4
GRID_W = 64
EPS = 1e-6
FOURIER_HEADS = 4
FOURIER_HEAD_DIM = 128
FOURIER_WIDTH = 512
HEAD_DIM = 64
N_KV_HEADS = 2
GQA_GROUP = 4
ATTN_WIDTH = 512
KV_WIDTH = 128
Q_END = 1024
IN_WIDTH = 1280
BLOCK = 128
ATT_QBLOCKS = 8
ROPE_THETA = 10000.0
POOL_WINDOWS = (2, 4, 8, 16)
POOL_GROUP = 256
POOL_HALO = 8
POOL_PAD = 2 * POOL_HALO
FFN_CHUNKS = ((0, 1024), (1024, 1024), (2048, 768))
FFN_PART_ROWS = 512
FFN_POOL_PART_ROWS = 256
IN_PART_ROWS = 512
FFT_RADIX = 64
FFT_ROWS = 16
LANES = 128
BF16_ROWS = 16
LOG2E = 1.4426950408889634
ADA_ROWS = 16
ADA_CTX_ROW = 8
ADA_STEP_CHUNKS = 3
VMEM_LIMIT = 56 * 1024 * 1024


def _params(*sem):
    return pltpu.CompilerParams(dimension_semantics=sem, vmem_limit_bytes=VMEM_LIMIT)


def _rms(x, g):
    ms = jnp.mean(x * x, axis=-1, keepdims=True)
    return x * lax.rsqrt(ms + EPS) * g


def _rms_mod(x, g, shift, scale):
    return _rms(x, g) * (1.0 + scale) + shift


def _dot(a, b):
    return jnp.dot(a, b, preferred_element_type=F32)


def _ada_kernel(c_ref, w_ref, b_ref, o_ref):
    a = jax.nn.silu(c_ref[...])
    w = w_ref[0]
    a_hi = a.astype(BF16)
    a_lo = (a - a_hi.astype(F32)).astype(BF16)
    w_hi = w.astype(BF16)
    w_lo = (w - w_hi.astype(F32)).astype(BF16)
    m2 = _dot(jnp.concatenate([a_hi, a_lo], axis=0), w_hi)
    m = m2[:ADA_ROWS] + m2[ADA_ROWS:] + _dot(a_hi, w_lo) + b_ref[0]
    for c in range(ADA_STEP_CHUNKS):
        for r in range(ADA_ROWS):
            o_ref[0, c, r] = m[r:r + 1, c * D_MODEL:(c + 1) * D_MODEL]


def _ada(cond, ada_w, ada_b):
    n_chunks = ada_w.shape[-1] // D_MODEL
    tn = ADA_STEP_CHUNKS * D_MODEL
    return pl.pallas_call(
        _ada_kernel,
        grid=(DEPTH, n_chunks // ADA_STEP_CHUNKS),
        in_specs=[
            pl.BlockSpec((ADA_ROWS, D_MODEL), lambda l, n: (0, 0)),
            pl.BlockSpec((1, D_MODEL, tn), lambda l, n: (l, 0, n)),
            pl.BlockSpec((1, 1, tn), lambda l, n: (l, 0, n)),
        ],
        out_specs=pl.BlockSpec((1, ADA_STEP_CHUNKS, ADA_ROWS, 1, D_MODEL), lambda l, n: (l, n, 0, 0, 0)),
        out_shape=jax.ShapeDtypeStruct((DEPTH, n_chunks, ADA_ROWS, 1, D_MODEL), F32),
        compiler_params=_params("arbitrary", "arbitrary"),
        name="ada_mod",
    )(cond, ada_w, ada_b.reshape(DEPTH, 1, ada_w.shape[-1]))


def _mod_operand(mods, layer, chunk, ctx):
    spec = pl.BlockSpec((1, 1, 1, 1, D_MODEL),
                        lambda b, i: (layer, chunk, ADA_CTX_ROW if ctx else b, 0, 0))
    return spec, mods


def _v(ref):
    return ref[0, 0, 0]


def _in_kernel(h_ref, g_ref, sh_ref, sc_ref, w_ref, cos_ref, sin_ref, wc_ref,
               zr_ref, zi_ref, q_ref, k_ref, v_ref, *, tm):
    n_parts = max(1, tm // IN_PART_ROWS)
    pr = tm // n_parts
    rows = [slice(p * pr, (p + 1) * pr) for p in range(n_parts)]
    projs = [_dot(_rms_mod(h_ref[0, r], g_ref[...], _v(sh_ref), _v(sc_ref)).astype(BF16), w_ref[...])
             for r in rows]
    wc = wc_ref[...]
    zs = [[_dot(proj[:, hh * LANES:(hh + 1) * LANES].astype(BF16), wc) for hh in range(FOURIER_HEADS)]
          for proj in projs]
    lane = lax.broadcasted_iota(jnp.int32, (pr, LANES), 1)
    even = (lane & 1) == 0
    low = lane < HEAD_DIM
    for r, proj, z4 in zip(rows, projs, zs):
        cosf = cos_ref[r]
        sinf = sin_ref[r]

        def rope(t):
            partner = jnp.where(even, pltpu.roll(t, LANES - 1, 1), pltpu.roll(t, 1, 1))
            return t * cosf + partner * sinf

        for hh, z in enumerate(z4):
            sl = slice(hh * LANES, (hh + 1) * LANES)
            zr_ref[0, r, sl] = z[:, :LANES].astype(BF16)
            zi_ref[0, r, sl] = z[:, LANES:].astype(BF16)
        for p in range(ATTN_WIDTH // LANES):
            sl = slice(p * LANES, (p + 1) * LANES)
            qq = proj[:, FOURIER_WIDTH + p * LANES:FOURIER_WIDTH + (p + 1) * LANES]
            q_ref[0, r, sl] = (rope(qq) * (HEAD_DIM ** -0.5 * LOG2E)).astype(BF16)
        kk = rope(proj[:, Q_END:Q_END + KV_WIDTH])
        vv = proj[:, Q_END + KV_WIDTH:]
        k_sw = pltpu.roll(kk, HEAD_DIM, 1)
        v_sw = pltpu.roll(vv, HEAD_DIM, 1)
        k_ref[0, 0, r] = jnp.where(low, kk, k_sw).astype(BF16)
        k_ref[0, 1, r] = jnp.where(low, k_sw, kk).astype(BF16)
        v_ref[0, 0, r] = jnp.where(low, vv, v_sw).astype(BF16)
        v_ref[0, 1, r] = jnp.where(low, v_sw, vv).astype(BF16)


def _in_proj(h, g, mods, layer, ctx, w_in, cosf, sinf, wc, tm):
    B, S, _ = h.shape
    (shift_spec, shift), (scale_spec, scale) = (_mod_operand(mods, layer, k, ctx) for k in (0, 1))
    tok = lambda w: pl.BlockSpec((1, tm, w), lambda b, i: (b, i, 0))
    kv = pl.BlockSpec((1, N_KV_HEADS, tm, LANES), lambda b, i: (b, 0, i, 0))
    return pl.pallas_call(
        functools.partial(_in_kernel, tm=tm),
        grid=(B, S // tm),
        in_specs=[
            tok(D_MODEL),
            pl.BlockSpec((1, D_MODEL), lambda b, i: (0, 0)),
            shift_spec, scale_spec,
            pl.BlockSpec((D_MODEL, IN_WIDTH), lambda b, i: (0, 0)),
            pl.BlockSpec((tm, LANES), lambda b, i: (i, 0)),
            pl.BlockSpec((tm, LANES), lambda b, i: (i, 0)),
            pl.BlockSpec((FOURIER_HEAD_DIM, 2 * FOURIER_HEAD_DIM), lambda b, i: (0, 0)),
        ],
        out_specs=[tok(FOURIER_WIDTH), tok(FOURIER_WIDTH), tok(ATTN_WIDTH), kv, kv],
        out_shape=[
            jax.ShapeDtypeStruct((B, S, FOURIER_WIDTH), BF16),
            jax.ShapeDtypeStruct((B, S, FOURIER_WIDTH), BF16),
            jax.ShapeDtypeStruct((B, S, ATTN_WIDTH), BF16),
            jax.ShapeDtypeStruct((B, N_KV_HEADS, S, LANES), BF16),
            jax.ShapeDtypeStruct((B, N_KV_HEADS, S, LANES), BF16),
        ],
        compiler_params=_params("arbitrary", "arbitrary"),
        name="in_proj",
    )(h, g, shift, scale, w_in, cosf, sinf, wc)


def _fft_kernel(zr_ref, zi_ref, w1_ref, w2_ref, o_ref, y_ref, *, t):
    r = FFT_RADIX
    w1 = w1_ref[...]
    for c in range(r // t):
        cs = slice(c * t, (c + 1) * t)
        xr = jnp.swapaxes(zr_ref[0, :, cs, :], 0, 1)
        xi = jnp.swapaxes(zi_ref[0, :, cs, :], 0, 1)
        outs_r, outs_i = [], []
        for j in range(t):
            y = _dot(w1, jnp.concatenate([xr[j], xi[j]], axis=0))
            outs_r.append(y[:r].astype(BF16))
            outs_i.append(y[r:].astype(BF16))
        y_ref[0, :, cs, :] = jnp.swapaxes(jnp.stack(outs_r, axis=0), 0, 1)
        y_ref[1, :, cs, :] = jnp.swapaxes(jnp.stack(outs_i, axis=0), 0, 1)
    for c in range(r // t):
        outs = []
        for j in range(t):
            k1 = c * t + j
            x = jnp.concatenate([y_ref[0, k1], y_ref[1, k1]], axis=0)
            outs.append(_dot(w2_ref[k1], x).astype(BF16))
        o_ref[0, :, c * t:(c + 1) * t, :] = jnp.swapaxes(jnp.stack(outs, axis=0), 0, 1)


def _fourier_4096(zr, zi, w1, w2):
    B, S, W = zr.shape
    r = FFT_RADIX
    zr4 = zr.reshape(B, r, r, W)
    zi4 = zi.reshape(B, r, r, W)
    blk = pl.BlockSpec((1, r, r, W), lambda b: (b, 0, 0, 0))
    out = pl.pallas_call(
        functools.partial(_fft_kernel, t=FFT_ROWS),
        grid=(B,),
        in_specs=[blk, blk, pl.BlockSpec((2 * r, 2 * r), lambda b: (0, 0)),
                  pl.BlockSpec((r, r, 2 * r), lambda b: (0, 0, 0))],
        out_specs=blk,
        out_shape=jax.ShapeDtypeStruct((B, r, r, W), BF16),
        scratch_shapes=[pltpu.VMEM((2, r, r, W), BF16)],
        compiler_params=_params("arbitrary"),
        name="fft_4096",
    )(zr4, zi4, w1, w2)
    return out.reshape(B, S, W)


def _dft_small_kernel(zr_ref, zi_ref, w_ref, o_ref, *, scale):
    x = jnp.concatenate([zr_ref[0], zi_ref[0]], axis=0)
    o_ref[0] = (_dot(w_ref[...], x) * scale).astype(BF16)


def _fourier_small(zr, zi, w):
    B, L, W = zr.shape
    scale = float((L * FOURIER_HEAD_DIM) ** -0.5)
    blk = pl.BlockSpec((1, L, W), lambda b: (b, 0, 0))
    return pl.pallas_call(
        functools.partial(_dft_small_kernel, scale=scale),
        grid=(B,),
        in_specs=[blk, blk, pl.BlockSpec((L, 2 * L), lambda b: (0, 0))],
        out_specs=blk,
        out_shape=jax.ShapeDtypeStruct((B, L, W), BF16),
        compiler_params=_params("arbitrary"),
        name="dft_ctx",
    )(zr, zi, w)


def _att_kernel(sink_ref, q_ref, *refs, n_steps, qb, window, n_cast):
    n_kv = 8 if window else 2
    kv_refs, cast_in = refs[:n_kv], refs[n_kv:n_kv + n_cast]
    o_ref, cast_out = refs[n_kv + n_cast], refs[n_kv + n_cast + 1:]
    if window:
        kp_ref, kc_ref, kn_ref, vp_ref, vc_ref, vn_ref, kx_ref, vx_ref = kv_refs
    else:
        kx_ref, vx_ref = kv_refs
    for src, dst in zip(cast_in, cast_out):
        dst[...] = src[...].astype(BF16)
    i = pl.program_id(1)
    lane = lax.broadcasted_iota(jnp.int32, (BLOCK, LANES), 1)
    low = lane < HEAD_DIM
    if window:
        row = lax.broadcasted_iota(jnp.int32, (BLOCK, BLOCK), 0)
        col = lax.broadcasted_iota(jnp.int32, (BLOCK, BLOCK), 1)
        ninf = jnp.float32(-jnp.inf)
        before = col >= row
        after = col <= row

    def window_tiles(p_ref, c_ref, n_ref, kh, a):
        cur = lambda j: c_ref[0, kh, j * BLOCK:(j + 1) * BLOCK]
        first = p_ref[0, kh] if a == 0 else cur(a - 1)
        last = n_ref[0, kh] if a == qb - 1 else cur(a + 1)
        return [first, cur(a), last]

    def scores(a, kh):
        if window:
            k = jnp.concatenate(window_tiles(kp_ref, kc_ref, kn_ref, kh, a) + [kx_ref[0, kh]], axis=0)
        else:
            k = kx_ref[0, kh]
        qs = []
        for g in range(GQA_GROUP):
            h = kh * GQA_GROUP + g
            qp = q_ref[0, a * BLOCK:(a + 1) * BLOCK, (h // 2) * LANES:(h // 2 + 1) * LANES]
            keep = low if h % 2 == 0 else jnp.logical_not(low)
            qs.append(jnp.where(keep, qp, jnp.zeros_like(qp)))
        qst = jnp.concatenate(qs, axis=0)
        return lax.dot_general(qst, k, (((1,), (1,)), ((), ())), preferred_element_type=F32)

    def finish(a, kh, s):
        if window:
            v = jnp.concatenate(window_tiles(vp_ref, vc_ref, vn_ref, kh, a) + [vx_ref[0, kh]], axis=0)
            ok_first = before & (i > 0) if a == 0 else before
            ok_last = after & (i < n_steps - 1) if a == qb - 1 else after
            b_first = jnp.where(ok_first, 0.0, ninf)
            b_last = jnp.where(ok_last, 0.0, ninf)
        else:
            v = vx_ref[0, kh]
        ps, ls = [], []
        for g in range(GQA_GROUP):
            sk = sink_ref[kh * GQA_GROUP + g] * LOG2E
            sg = s[g * BLOCK:(g + 1) * BLOCK]
            if window:
                sg = jnp.concatenate(
                    [sg[:, :BLOCK] + b_first, sg[:, BLOCK:2 * BLOCK],
                     sg[:, 2 * BLOCK:3 * BLOCK] + b_last, sg[:, 3 * BLOCK:]], axis=1)
            m = jnp.maximum(jnp.max(sg, axis=-1, keepdims=True), sk)
            p = jnp.exp2(sg - m)
            ls.append(jnp.sum(p, axis=-1, keepdims=True) + jnp.exp2(sk - m))
            ps.append(p.astype(BF16))
        o = _dot(jnp.concatenate(ps, axis=0), v)
        og = [o[g * BLOCK:(g + 1) * BLOCK] / ls[g] for g in range(GQA_GROUP)]
        for pr in range(GQA_GROUP // 2):
            c0 = kh * GQA_GROUP * HEAD_DIM + pr * LANES
            o_ref[0, a * BLOCK:(a + 1) * BLOCK, c0:c0 + LANES] = jnp.where(
                low, og[2 * pr], og[2 * pr + 1]).astype(BF16)

    units = [(a, kh) for a in range(qb) for kh in range(N_KV_HEADS)]
    ahead = 2
    pending = [scores(*u) for u in units[:ahead]]
    for n, u in enumerate(units):
        s = pending.pop(0)
        if n + ahead < len(units):
            pending.append(scores(*units[n + ahead]))
        finish(*u, s)


def _cast_operands(cast, grid_steps, flat_step):
    in_specs, out_specs, out_shapes = [], [], []
    for w, first, n in cast:
        per, every = n // grid_steps, 1
        while per % BF16_ROWS:
            per, every = 2 * per, 2 * every
        assert n % per == 0 and first % per == 0
        blk = (per, w.shape[1])
        in_specs.append(pl.BlockSpec(blk, lambda b, i, e=every, o=first // per: (o + flat_step(b, i) // e, 0)))
        out_specs.append(pl.BlockSpec(blk, lambda b, i, e=every: (flat_step(b, i) // e, 0)))
        out_shapes.append(jax.ShapeDtypeStruct((n, w.shape[1]), BF16))
    return in_specs, out_specs, out_shapes


def _attention(q, k, v, kx, vx, sink, window, cast=()):
    B, S, _ = q.shape
    nb = S // BLOCK
    qb = min(ATT_QBLOCKS, nb)
    rows = qb * BLOCK
    n_steps = S // rows
    L = kx.shape[2] // (B // kx.shape[0])
    cast_in, cast_out, cast_shapes = _cast_operands(cast, B * n_steps, lambda b, i: b * n_steps + i)
    qspec = pl.BlockSpec((1, rows, ATTN_WIDTH), lambda b, i: (b, i, 0))
    if kx.shape[0] == B:
        xspec = pl.BlockSpec((1, N_KV_HEADS, L, LANES), lambda b, i: (b, 0, 0, 0))
    else:
        xspec = pl.BlockSpec((1, N_KV_HEADS, L, LANES), lambda b, i: (0, 0, b, 0))
    smem = pl.BlockSpec(memory_space=pltpu.SMEM)
    if window:
        blk = (1, N_KV_HEADS, BLOCK, LANES)
        prev = pl.BlockSpec(blk, lambda b, i: (b, 0, jnp.maximum(i * qb - 1, 0), 0))
        cur = pl.BlockSpec((1, N_KV_HEADS, rows, LANES), lambda b, i: (b, 0, i, 0))
        nxt = pl.BlockSpec(blk, lambda b, i: (b, 0, jnp.minimum((i + 1) * qb, nb - 1), 0))
        in_specs = [smem, qspec, prev, cur, nxt, prev, cur, nxt, xspec, xspec]
        args = (sink, q, k, k, k, v, v, v, kx, vx)
    else:
        in_specs = [smem, qspec, xspec, xspec]
        args = (sink, q, kx, vx)
    out, *casted = pl.pallas_call(
        functools.partial(_att_kernel, n_steps=n_steps, qb=qb, window=window, n_cast=len(cast)),
        grid=(B, n_steps),
        in_specs=in_specs + cast_in,
        out_specs=[qspec] + cast_out,
        out_shape=[jax.ShapeDtypeStruct((B, S, ATTN_WIDTH), BF16)] + cast_shapes,
        compiler_params=_params("arbitrary", "arbitrary"),
        name="window_attn" if window else "ctx_attn",
    )(*args, *(w for w, _, _ in cast))
    return (out, casted) if cast else out


def _pool_fill(xs_ref, h_ref, hp_ref, hx_ref, g, sh, sc, lo, hi, tm, first_tile, last_tile):
    hal = POOL_HALO
    if lo == 0:
        xs_ref[0:hal] = jnp.where(first_tile, 0.0, _rms_mod(hp_ref[0], g, sh, sc))
    xs_ref[hal + lo:hal + hi] = _rms_mod(h_ref[0, lo:hi], g, sh, sc)
    if hi == tm:
        xs_ref[hal + tm:2 * hal + tm] = jnp.where(last_tile, 0.0, _rms_mod(hx_ref[0], g, sh, sc))
        xs_ref[2 * hal + tm:] = jnp.zeros((POOL_PAD, D_MODEL), F32)


def _pool_part(xs_ref, p_ref, ic_ref, pw_ref, r0, pr):
    hal = POOL_HALO
    ys = []
    for gi, w in enumerate(POOL_WINDOWS):
        cs = slice(gi * POOL_GROUP, (gi + 1) * POOL_GROUP)
        x = lambda off, n: xs_ref[r0 + off:r0 + off + n, cs]
        if w == 2:
            acc = x(hal - 1, pr) + x(hal, pr)
        else:
            n2, n4, n8 = pr + 3 * hal, pr + 2 * hal, pr + hal
            p_ref[0:n2] = x(0, n2) + x(1, n2)
            if w == 4:
                acc = p_ref[hal - 2:hal - 2 + pr] + p_ref[hal:hal + pr]
            else:
                p_ref[0:n4] = p_ref[0:n4] + p_ref[2:n4 + 2]
                if w == 8:
                    acc = p_ref[hal - 4:hal - 4 + pr] + p_ref[hal:hal + pr]
                else:
                    p_ref[0:n8] = p_ref[0:n8] + p_ref[4:n8 + 4]
                    acc = p_ref[0:pr] + p_ref[hal:hal + pr]
        yg = acc * ic_ref[r0:r0 + pr, gi:gi + 1] - x(hal, pr)
        ys.append(_dot(yg.astype(BF16), pw_ref[gi]))
    return jnp.concatenate(ys, axis=1)


def _ffn_parts(tm, mode):
    return max(1, tm // (FFN_POOL_PART_ROWS if mode == "pool" else FFN_PART_ROWS))


def _ffn_kernel(*refs, tm, seq, mode, final):
    refs = list(refs)
    take = lambda n: [refs.pop(0) for _ in range(n)]
    (h_ref,) = take(1)
    if mode == "proj":
        four_ref, attn_ref, wo_ref, g1_ref = take(4)
    elif mode == "pool":
        hp_ref, hx_ref, gm_ref, sh1_ref, sc1_ref, g1_ref, pw_ref, ps_ref, ic_ref = take(9)
    gn_ref, sh_ref, sc_ref, g2_ref, w1_ref, w3_ref, w2_ref = take(7)
    if final:
        (fg_ref,) = take(1)
    (o_ref,) = take(1)
    if mode == "pool":
        xs_ref, p_ref = take(2)
        first_tile = pl.program_id(1) == 0
        last_tile = pl.program_id(1) == seq // tm - 1
    n_parts = _ffn_parts(tm, mode)
    pr = tm // n_parts
    rows = [slice(p * pr, (p + 1) * pr) for p in range(n_parts)]

    def mixed(p):
        h = h_ref[0, rows[p]]
        if mode == "proj":
            y = (_dot(four_ref[0, rows[p]], wo_ref[:FOURIER_WIDTH])
                 + _dot(attn_ref[0, rows[p]], wo_ref[FOURIER_WIDTH:]))
            return h + _v(g1_ref) * y
        if mode == "pool":
            lo = 0 if p == 0 else p * pr + POOL_PAD
            hi = tm if p == n_parts - 1 else (p + 1) * pr + POOL_PAD
            _pool_fill(xs_ref, h_ref, hp_ref, hx_ref, gm_ref[...], _v(sh1_ref), _v(sc1_ref), lo, hi, tm,
                       first_tile, last_tile)
            y = _pool_part(xs_ref, p_ref, ic_ref, pw_ref, p * pr, pr) * ps_ref[...]
            return h + _v(g1_ref) * y
        return h

    def normed(h):
        return _rms_mod(h, gn_ref[...], _v(sh_ref), _v(sc_ref)).astype(BF16)

    def up(hn, c0, cn):
        return _dot(hn, w1_ref[0, :, c0:c0 + cn]), _dot(hn, w3_ref[0, :, c0:c0 + cn])

    c0, cn = FFN_CHUNKS[0]
    if mode == "pool":
        hs, hns, ab = [], [], []
        for p in range(n_parts):
            hs.append(mixed(p))
            hns.append(normed(hs[p]))
            ab.append(up(hns[p], c0, cn))
    else:
        hs = [mixed(p) for p in range(n_parts)]
        hns = [normed(h) for h in hs]
        ab = [up(hn, c0, cn) for hn in hns]
    accs = [None] * n_parts
    for ci, (c0, cn) in enumerate(FFN_CHUNKS):
        if ci > 0:
            ab = [up(hn, c0, cn) for hn in hns]
        for p, (a, b) in enumerate(ab):
            t = (jax.nn.silu(a) * b).astype(BF16)
            d = _dot(t, w2_ref[0, c0:c0 + cn, :])
            accs[p] = d if accs[p] is None else accs[p] + d
    for p, r in enumerate(rows):
        out = hs[p] + _v(g2_ref) * accs[p]
        if final:
            out = _rms(out, fg_ref[...])
        o_ref[0, r] = out


def _ffn(h, mix, gn, mods, layer, ctx, ffn_w, final_g, tm):
    B, S, _ = h.shape
    mode = "none" if mix is None else mix[0]
    vecs = [_mod_operand(mods, layer, k, ctx) for k in range(6)]
    vspec = [v[0] for v in vecs]
    row = pl.BlockSpec((1, D_MODEL), lambda b, i: (0, 0))
    tok = lambda w: pl.BlockSpec((1, tm, w), lambda b, i: (b, i, 0))
    whole = pl.BlockSpec(memory_space=pltpu.VMEM)
    in_specs = [tok(D_MODEL)]
    args = [h]
    scratch = []
    if mode == "proj":
        _, four, attn, w_out = mix
        in_specs += [tok(FOURIER_WIDTH), tok(ATTN_WIDTH), whole, vspec[2]]
        args += [four, attn, w_out, mods]
    elif mode == "pool":
        _, g_mix, pool_w, pool_scale = mix
        hb = tm // POOL_HALO
        halo = (1, POOL_HALO, D_MODEL)
        t = np.arange(S)
        inv_count = jnp.asarray(np.stack(
            [1.0 / (np.minimum(t + w // 2, S) - np.maximum(t - w // 2, 0)) for w in POOL_WINDOWS], axis=1), F32)
        in_specs += [
            pl.BlockSpec(halo, lambda b, i: (b, jnp.maximum(i * hb - 1, 0), 0)),
            pl.BlockSpec(halo, lambda b, i: (b, jnp.minimum((i + 1) * hb, S // POOL_HALO - 1), 0)),
            row, vspec[0], vspec[1], vspec[2],
            pl.BlockSpec((len(POOL_WINDOWS), POOL_GROUP, POOL_GROUP), lambda b, i: (0, 0, 0)),
            row,
            pl.BlockSpec((tm, len(POOL_WINDOWS)), lambda b, i: (i, 0)),
        ]
        args += [h, h, g_mix, mods, mods, mods, pool_w, pool_scale, inv_count]
        pr = tm // _ffn_parts(tm, mode)
        scratch = [pltpu.VMEM((tm + 2 * POOL_HALO + POOL_PAD, D_MODEL), F32),
                   pltpu.VMEM((pr + 3 * POOL_HALO, POOL_GROUP), F32)]
    in_specs += [row, vspec[3], vspec[4], vspec[5]] + [
        pl.BlockSpec((1,) + w.shape[1:], lambda b, i, n=w.shape[0]: (layer % n, 0, 0),
                     pipeline_mode=pl.Buffered(1)) for w in ffn_w]
    args += [gn, mods, mods, mods, *ffn_w]
    if final_g is not None:
        in_specs.append(row)
        args.append(final_g)
    return pl.pallas_call(
        functools.partial(_ffn_kernel, tm=tm, seq=S, mode=mode, final=final_g is not None),
        grid=(B, S // tm),
        in_specs=in_specs,
        out_specs=tok(D_MODEL),
        out_shape=jax.ShapeDtypeStruct((B, S, D_MODEL), F32),
        scratch_shapes=scratch,
        compiler_params=_params("arbitrary", "arbitrary"),
        name="ffn_" + mode,
    )(*args)


def _dft_cos_sin(n):
    idx = np.arange(n, dtype=np.int64)
    ang = 2.0 * np.pi * ((idx[:, None] * idx[None, :]) % n) / n
    return np.cos(ang), np.sin(ang)


def _rope_tables(n_tokens):
    rows = n_tokens // GRID_W
    row = jnp.repeat(jnp.arange(rows, dtype=F32), GRID_W)
    col = jnp.tile(jnp.arange(GRID_W, dtype=F32), rows)
    n_freq = HEAD_DIM // 4
    inv = ROPE_THETA ** (-jnp.arange(n_freq, dtype=F32) / n_freq)
    ang = jnp.concatenate([row[:, None] * inv[None], col[:, None] * inv[None]], axis=-1)
    cos = jnp.repeat(jnp.cos(ang), 2, axis=-1)
    sin = jnp.repeat(jnp.sin(ang), 2, axis=-1)
    sign = jnp.tile(jnp.asarray([-1.0, 1.0], F32), HEAD_DIM // 2)
    return jnp.tile(cos, (1, 2)), jnp.tile(sin * sign, (1, 2))


def _fourier_tables():
    r = FFT_RADIX
    c, s = _dft_cos_sin(r)
    w1 = np.block([[c, s], [-s, c]])
    n_pos = r * r
    idx = np.arange(r, dtype=np.int64)
    k = idx[:, None, None] + r * idx[None, :, None]
    ang = 2.0 * np.pi * ((k * idx[None, None, :]) % n_pos) / n_pos
    scale = (n_pos * FOURIER_HEAD_DIM) ** -0.5
    w2 = np.concatenate([np.cos(ang), np.sin(ang)], axis=2) * scale
    cc, sc = _dft_cos_sin(FOURIER_HEAD_DIM)
    wc = np.concatenate([cc, -sc], axis=1)
    f32 = lambda a: jnp.asarray(np.ascontiguousarray(a), F32)
    return f32(w1).astype(BF16), f32(w2).astype(BF16), f32(wc).astype(BF16)


def kernel(x, c, ctx, c_ctx, ada_w, ada_b, norm_mix_g, norm_ffn_g, mix_in_w, mix_out_w, attn_sink,
           pool_w, pool_scale, ffn_w1, ffn_w3, ffn_w2, final_g):
    B, S, _ = x.shape
    L = ctx.shape[1]
    tm_ffn = 2 * FFN_PART_ROWS
    tm_ctx = L
    tm_flat = min(B * L, 2 * FFN_PART_ROWS)

    cond = jnp.zeros((ADA_ROWS, D_MODEL), F32).at[:B].set(c).at[B].set(c_ctx)
    mods = _ada(cond, ada_w, ada_b)

    cosf, sinf = _rope_tables(S)
    cos_id = jnp.ones((B * L, LANES), F32)
    sin_id = jnp.zeros((B * L, LANES), F32)
    w1_dft, w2_dft, wc = _fourier_tables()
    cl, sl_ = _dft_cos_sin(L)
    w_ctx_dft = jnp.asarray(np.concatenate([cl, sl_], axis=1), F32).astype(BF16)

    last_ctx_reader = max(range(0, DEPTH, 2))
    h, hc = x, ctx
    for layer in range(DEPTH):
        update_ctx = layer < last_ctx_reader
        g_mix = norm_mix_g[layer].reshape(1, D_MODEL)
        g_ffn = norm_ffn_g[layer].reshape(1, D_MODEL)
        fin = final_g.reshape(1, D_MODEL) if layer == DEPTH - 1 else None
        j = layer // 2
        if layer % 2 == 0:
            w_in = mix_in_w[j].astype(BF16)
            w_out = mix_out_w[j].astype(BF16)
            sink = attn_sink[j]
            flat = lambda a: a.reshape(1, B * L, a.shape[-1])
            unflat = lambda a: a.reshape(B, L, a.shape[-1])
            zr_c, zi_c, q_c, k_c, v_c = _in_proj(flat(hc), g_mix, mods, layer, True, w_in, cos_id, sin_id, wc,
                                                 tm_flat)
            zr, zi, q, k, v = _in_proj(h, g_mix, mods, layer, False, w_in, cosf, sinf, wc, 4 * IN_PART_ROWS)
            four = _fourier_4096(zr, zi, w1_dft, w2_dft)
            f32_w = (ffn_w1, ffn_w3, ffn_w2)
            cast = [(w.reshape(-1, w.shape[-1]), layer * w.shape[1], 2 * w.shape[1]) for w in f32_w]
            attn, casted = _attention(q, k, v, k_c, v_c, sink, True, cast=cast)
            ffn_w = tuple(cw.reshape(2, *w.shape[1:]) for cw, w in zip(casted, f32_w))
            h = _ffn(h, ("proj", four, attn, w_out), g_ffn, mods, layer, False, ffn_w, fin, tm_ffn)
            if update_ctx:
                four_c = _fourier_small(unflat(zr_c), unflat(zi_c), w_ctx_dft)
                attn_c = _attention(unflat(q_c), None, None, k_c, v_c, sink, False)
                hc = unflat(_ffn(flat(hc), ("proj", flat(four_c), flat(attn_c), w_out), g_ffn, mods, layer, True,
                                 ffn_w, None, tm_flat))
        else:
            pw = pool_w[j].astype(BF16)
            psc = pool_scale[j].reshape(1, D_MODEL)
            h = _ffn(h, ("pool", g_mix, pw, psc), g_ffn, mods, layer, False, ffn_w, fin, tm_ffn)
            if update_ctx:
                hc = _ffn(hc, ("pool", g_mix, pw, psc), g_ffn, mods, layer, True, ffn_w, None, tm_ctx)
    return h
```

```python
import functools

import numpy as np
import jax
import jax.numpy as jnp
from jax import lax
from jax.experimental import pallas as pl
from jax.experimental.pallas import tpu as pltpu

F32 = jnp.float32
BF16 = jnp.bfloat16

D_MODEL = 1024
DEPTH = 4
GRID_W = 64
EPS = 1e-6
FOURIER_HEADS = 4
FOURIER_HEAD_DIM = 128
FOURIER_WIDTH = 512
HEAD_DIM = 64
N_KV_HEADS = 2
GQA_GROUP = 4
ATTN_WIDTH = 512
KV_WIDTH = 128
Q_END = 1024
IN_WIDTH = 1280
BLOCK = 128
ATT_QBLOCKS = 16
ROPE_THETA = 10000.0
POOL_WINDOWS = (2, 4, 8, 16)
POOL_GROUP = 256
POOL_HALO = 8
POOL_PAD = 2 * POOL_HALO
FFN_CHUNKS = ((0, 1024), (1024, 1024), (2048, 768))
FFN_PART_ROWS = 512
FFN_POOL_PART_ROWS = 256
IN_PART_ROWS = 512
FFT_RADIX = 64
FFT_ROWS = 16
LANES = 128
BF16_ROWS = 16
LOG2E = 1.4426950408889634
ADA_ROWS = 16
ADA_CTX_ROW = 8
ADA_STEP_CHUNKS = 3
VMEM_LIMIT = 56 * 1024 * 1024


def _params(*sem):
    return pltpu.CompilerParams(dimension_semantics=sem, vmem_limit_bytes=VMEM_LIMIT)


def _rms(x, g):
    ms = jnp.mean(x * x, axis=-1, keepdims=True)
    return x * lax.rsqrt(ms + EPS) * g


def _rms_mod(x, g, shift, scale):
    return _rms(x, g) * (1.0 + scale) + shift


def _dot(a, b):
    return jnp.dot(a, b, preferred_element_type=F32)


def _ada_kernel(c_ref, w_ref, b_ref, o_ref):
    a = jax.nn.silu(c_ref[...])
    w = w_ref[0]
    a_hi = a.astype(BF16)
    a_lo = (a - a_hi.astype(F32)).astype(BF16)
    w_hi = w.astype(BF16)
    w_lo = (w - w_hi.astype(F32)).astype(BF16)
    m2 = _dot(jnp.concatenate([a_hi, a_lo], axis=0), w_hi)
    m = m2[:ADA_ROWS] + m2[ADA_ROWS:] + _dot(a_hi, w_lo) + b_ref[0]
    for c in range(ADA_STEP_CHUNKS):
        for r in range(ADA_ROWS):
            o_ref[0, c, r] = m[r:r + 1, c * D_MODEL:(c + 1) * D_MODEL]


def _ada(cond, ada_w, ada_b):
    n_chunks = ada_w.shape[-1] // D_MODEL
    tn = ADA_STEP_CHUNKS * D_MODEL
    return pl.pallas_call(
        _ada_kernel,
        grid=(DEPTH, n_chunks // ADA_STEP_CHUNKS),
        in_specs=[
            pl.BlockSpec((ADA_ROWS, D_MODEL), lambda l, n: (0, 0)),
            pl.BlockSpec((1, D_MODEL, tn), lambda l, n: (l, 0, n)),
            pl.BlockSpec((1, 1, tn), lambda l, n: (l, 0, n)),
        ],
        out_specs=pl.BlockSpec((1, ADA_STEP_CHUNKS, ADA_ROWS, 1, D_MODEL), lambda l, n: (l, n, 0, 0, 0)),
        out_shape=jax.ShapeDtypeStruct((DEPTH, n_chunks, ADA_ROWS, 1, D_MODEL), F32),
        compiler_params=_params("arbitrary", "arbitrary"),
        name="ada_mod",
    )(cond, ada_w, ada_b.reshape(DEPTH, 1, ada_w.shape[-1]))


def _mod_operand(mods, layer, chunk, ctx):
    spec = pl.BlockSpec((1, 1, 1, 1, D_MODEL),
                        lambda b, i: (layer, chunk, ADA_CTX_ROW if ctx else b, 0, 0))
    return spec, mods


def _v(ref):
    return ref[0, 0, 0]


def _in_kernel(h_ref, g_ref, sh_ref, sc_ref, w_ref, cos_ref, sin_ref, wc_ref,
               zr_ref, zi_ref, q_ref, k_ref, v_ref, *, tm):
    n_parts = max(1, tm // IN_PART_ROWS)
    pr = tm // n_parts
    rows = [slice(p * pr, (p + 1) * pr) for p in range(n_parts)]
    projs = [_dot(_rms_mod(h_ref[0, r], g_ref[...], _v(sh_ref), _v(sc_ref)).astype(BF16), w_ref[...])
             for r in rows]
    wc = wc_ref[...]
    zs = [[_dot(proj[:, hh * LANES:(hh + 1) * LANES].astype(BF16), wc) for hh in range(FOURIER_HEADS)]
          for proj in projs]
    lane = lax.broadcasted_iota(jnp.int32, (pr, LANES), 1)
    even = (lane & 1) == 0
    low = lane < HEAD_DIM
    for r, proj, z4 in zip(rows, projs, zs):
        cosf = cos_ref[r]
        sinf = sin_ref[r]

        def rope(t):
            partner = jnp.where(even, pltpu.roll(t, LANES - 1, 1), pltpu.roll(t, 1, 1))
            return t * cosf + partner * sinf

        for hh, z in enumerate(z4):
            sl = slice(hh * LANES, (hh + 1) * LANES)
            zr_ref[0, r, sl] = z[:, :LANES].astype(BF16)
            zi_ref[0, r, sl] = z[:, LANES:].astype(BF16)
        for p in range(ATTN_WIDTH // LANES):
            sl = slice(p * LANES, (p + 1) * LANES)
            qq = proj[:, FOURIER_WIDTH + p * LANES:FOURIER_WIDTH + (p + 1) * LANES]
            q_ref[0, r, sl] = (rope(qq) * (HEAD_DIM ** -0.5 * LOG2E)).astype(BF16)
        kk = rope(proj[:, Q_END:Q_END + KV_WIDTH])
        vv = proj[:, Q_END + KV_WIDTH:]
        k_sw = pltpu.roll(kk, HEAD_DIM, 1)
        v_sw = pltpu.roll(vv, HEAD_DIM, 1)
        k_ref[0, 0, r] = jnp.where(low, kk, k_sw).astype(BF16)
        k_ref[0, 1, r] = jnp.where(low, k_sw, kk).astype(BF16)
        v_ref[0, 0, r] = jnp.where(low, vv, v_sw).astype(BF16)
        v_ref[0, 1, r] = jnp.where(low, v_sw, vv).astype(BF16)


def _in_proj(h, g, mods, layer, ctx, w_in, cosf, sinf, wc, tm):
    B, S, _ = h.shape
    (shift_spec, shift), (scale_spec, scale) = (_mod_operand(mods, layer, k, ctx) for k in (0, 1))
    tok = lambda w: pl.BlockSpec((1, tm, w), lambda b, i: (b, i, 0))
    kv = pl.BlockSpec((1, N_KV_HEADS, tm, LANES), lambda b, i: (b, 0, i, 0))
    return pl.pallas_call(
        functools.partial(_in_kernel, tm=tm),
        grid=(B, S // tm),
        in_specs=[
            tok(D_MODEL),
            pl.BlockSpec((1, D_MODEL), lambda b, i: (0, 0)),
            shift_spec, scale_spec,
            pl.BlockSpec((D_MODEL, IN_WIDTH), lambda b, i: (0, 0)),
            pl.BlockSpec((tm, LANES), lambda b, i: (i, 0)),
            pl.BlockSpec((tm, LANES), lambda b, i: (i, 0)),
            pl.BlockSpec((FOURIER_HEAD_DIM, 2 * FOURIER_HEAD_DIM), lambda b, i: (0, 0)),
        ],
        out_specs=[tok(FOURIER_WIDTH), tok(FOURIER_WIDTH), tok(ATTN_WIDTH), kv, kv],
        out_shape=[
            jax.ShapeDtypeStruct((B, S, FOURIER_WIDTH), BF16),
            jax.ShapeDtypeStruct((B, S, FOURIER_WIDTH), BF16),
            jax.ShapeDtypeStruct((B, S, ATTN_WIDTH), BF16),
            jax.ShapeDtypeStruct((B, N_KV_HEADS, S, LANES), BF16),
            jax.ShapeDtypeStruct((B, N_KV_HEADS, S, LANES), BF16),
        ],
        compiler_params=_params("arbitrary", "arbitrary"),
        name="in_proj",
    )(h, g, shift, scale, w_in, cosf, sinf, wc)


def _fft_kernel(zr_ref, zi_ref, w1_ref, w2_ref, o_ref, y_ref, *, t):
    r = FFT_RADIX
    w1 = w1_ref[...]
    for c in range(r // t):
        cs = slice(c * t, (c + 1) * t)
        xr = jnp.swapaxes(zr_ref[0, :, cs, :], 0, 1)
        xi = jnp.swapaxes(zi_ref[0, :, cs, :], 0, 1)
        outs_r, outs_i = [], []
        for j in range(t):
            y = _dot(w1, jnp.concatenate([xr[j], xi[j]], axis=0))
            outs_r.append(y[:r].astype(BF16))
            outs_i.append(y[r:].astype(BF16))
        y_ref[0, :, cs, :] = jnp.swapaxes(jnp.stack(outs_r, axis=0), 0, 1)
        y_ref[1, :, cs, :] = jnp.swapaxes(jnp.stack(outs_i, axis=0), 0, 1)
    for c in range(r // t):
        outs = []
        for j in range(t):
            k1 = c * t + j
            x = jnp.concatenate([y_ref[0, k1], y_ref[1, k1]], axis=0)
            outs.append(_dot(w2_ref[k1], x).astype(BF16))
        o_ref[0, :, c * t:(c + 1) * t, :] = jnp.swapaxes(jnp.stack(outs, axis=0), 0, 1)


def _fourier_4096(zr, zi, w1, w2):
    B, S, W = zr.shape
    r = FFT_RADIX
    zr4 = zr.reshape(B, r, r, W)
    zi4 = zi.reshape(B, r, r, W)
    blk = pl.BlockSpec((1, r, r, W), lambda b: (b, 0, 0, 0))
    out = pl.pallas_call(
        functools.partial(_fft_kernel, t=FFT_ROWS),
        grid=(B,),
        in_specs=[blk, blk, pl.BlockSpec((2 * r, 2 * r), lambda b: (0, 0)),
                  pl.BlockSpec((r, r, 2 * r), lambda b: (0, 0, 0))],
        out_specs=blk,
        out_shape=jax.ShapeDtypeStruct((B, r, r, W), BF16),
        scratch_shapes=[pltpu.VMEM((2, r, r, W), BF16)],
        compiler_params=_params("arbitrary"),
        name="fft_4096",
    )(zr4, zi4, w1, w2)
    return out.reshape(B, S, W)


def _dft_small_kernel(zr_ref, zi_ref, w_ref, o_ref, *, scale):
    x = jnp.concatenate([zr_ref[0], zi_ref[0]], axis=0)
    o_ref[0] = (_dot(w_ref[...], x) * scale).astype(BF16)


def _fourier_small(zr, zi, w):
    B, L, W = zr.shape
    scale = float((L * FOURIER_HEAD_DIM) ** -0.5)
    blk = pl.BlockSpec((1, L, W), lambda b: (b, 0, 0))
    return pl.pallas_call(
        functools.partial(_dft_small_kernel, scale=scale),
        grid=(B,),
        in_specs=[blk, blk, pl.BlockSpec((L, 2 * L), lambda b: (0, 0))],
        out_specs=blk,
        out_shape=jax.ShapeDtypeStruct((B, L, W), BF16),
        compiler_params=_params("arbitrary"),
        name="dft_ctx",
    )(zr, zi, w)


def _att_kernel(sink_ref, q_ref, *refs, n_steps, qb, window, n_cast):
    n_kv = 8 if window else 2
    kv_refs, cast_in = refs[:n_kv], refs[n_kv:n_kv + n_cast]
    o_ref, cast_out = refs[n_kv + n_cast], refs[n_kv + n_cast + 1:]
    if window:
        kp_ref, kc_ref, kn_ref, vp_ref, vc_ref, vn_ref, kx_ref, vx_ref = kv_refs
    else:
        kx_ref, vx_ref = kv_refs
    for src, dst in zip(cast_in, cast_out):
        dst[...] = src[...].astype(BF16)
    i = pl.program_id(1)
    lane = lax.broadcasted_iota(jnp.int32, (BLOCK, LANES), 1)
    low = lane < HEAD_DIM
    if window:
        row = lax.broadcasted_iota(jnp.int32, (BLOCK, BLOCK), 0)
        col = lax.broadcasted_iota(jnp.int32, (BLOCK, BLOCK), 1)
        ninf = jnp.float32(-jnp.inf)
        before = col >= row
        after = col <= row

    def window_tiles(p_ref, c_ref, n_ref, kh, a):
        cur = lambda j: c_ref[0, kh, j * BLOCK:(j + 1) * BLOCK]
        first = p_ref[0, kh] if a == 0 else cur(a - 1)
        last = n_ref[0, kh] if a == qb - 1 else cur(a + 1)
        return [first, cur(a), last]

    def scores(a, kh):
        if window:
            k = jnp.concatenate(window_tiles(kp_ref, kc_ref, kn_ref, kh, a) + [kx_ref[0, kh]], axis=0)
        else:
            k = kx_ref[0, kh]
        qs = []
        for g in range(GQA_GROUP):
            h = kh * GQA_GROUP + g
            qp = q_ref[0, a * BLOCK:(a + 1) * BLOCK, (h // 2) * LANES:(h // 2 + 1) * LANES]
            keep = low if h % 2 == 0 else jnp.logical_not(low)
            qs.append(jnp.where(keep, qp, jnp.zeros_like(qp)))
        qst = jnp.concatenate(qs, axis=0)
        return lax.dot_general(qst, k, (((1,), (1,)), ((), ())), preferred_element_type=F32)

    def finish(a, kh, s):
        if window:
            v = jnp.concatenate(window_tiles(vp_ref, vc_ref, vn_ref, kh, a) + [vx_ref[0, kh]], axis=0)
            ok_first = before & (i > 0) if a == 0 else before
            ok_last = after & (i < n_steps - 1) if a == qb - 1 else after
            b_first = jnp.where(ok_first, 0.0, ninf)
            b_last = jnp.where(ok_last, 0.0, ninf)
        else:
            v = vx_ref[0, kh]
        ps, ls = [], []
        for g in range(GQA_GROUP):
            sk = sink_ref[kh * GQA_GROUP + g] * LOG2E
            sg = s[g * BLOCK:(g + 1) * BLOCK]
            if window:
                sg = jnp.concatenate(
                    [sg[:, :BLOCK] + b_first, sg[:, BLOCK:2 * BLOCK],
                     sg[:, 2 * BLOCK:3 * BLOCK] + b_last, sg[:, 3 * BLOCK:]], axis=1)
            m = jnp.maximum(jnp.max(sg, axis=-1, keepdims=True), sk)
            p = jnp.exp2(sg - m)
            ls.append(jnp.sum(p, axis=-1, keepdims=True) + jnp.exp2(sk - m))
            ps.append(p.astype(BF16))
        o = _dot(jnp.concatenate(ps, axis=0), v)
        og = [o[g * BLOCK:(g + 1) * BLOCK] / ls[g] for g in range(GQA_GROUP)]
        for pr in range(GQA_GROUP // 2):
            c0 = kh * GQA_GROUP * HEAD_DIM + pr * LANES
            o_ref[0, a * BLOCK:(a + 1) * BLOCK, c0:c0 + LANES] = jnp.where(
                low, og[2 * pr], og[2 * pr + 1]).astype(BF16)

    units = [(a, kh) for a in range(qb) for kh in range(N_KV_HEADS)]
    ahead = 2
    pending = [scores(*u) for u in units[:ahead]]
    for n, u in enumerate(units):
        s = pending.pop(0)
        if n + ahead < len(units):
            pending.append(scores(*units[n + ahead]))
        finish(*u, s)


def _cast_operands(cast, grid_steps, flat_step):
    in_specs, out_specs, out_shapes = [], [], []
    for w, first, n in cast:
        per, every = n // grid_steps, 1
        while per % BF16_ROWS:
            per, every = 2 * per, 2 * every
        assert n % per == 0 and first % per == 0
        blk = (per, w.shape[1])
        in_specs.append(pl.BlockSpec(blk, lambda b, i, e=every, o=first // per: (o + flat_step(b, i) // e, 0)))
        out_specs.append(pl.BlockSpec(blk, lambda b, i, e=every: (flat_step(b, i) // e, 0)))
        out_shapes.append(jax.ShapeDtypeStruct((n, w.shape[1]), BF16))
    return in_specs, out_specs, out_shapes


def _attention(q, k, v, kx, vx, sink, window, cast=()):
    B, S, _ = q.shape
    nb = S // BLOCK
    qb = min(ATT_QBLOCKS, nb)
    rows = qb * BLOCK
    n_steps = S // rows
    L = kx.shape[2] // (B // kx.shape[0])
    cast_in, cast_out, cast_shapes = _cast_operands(cast, B * n_steps, lambda b, i: b * n_steps + i)
    qspec = pl.BlockSpec((1, rows, ATTN_WIDTH), lambda b, i: (b, i, 0))
    if kx.shape[0] == B:
        xspec = pl.BlockSpec((1, N_KV_HEADS, L, LANES), lambda b, i: (b, 0, 0, 0))
    else:
        xspec = pl.BlockSpec((1, N_KV_HEADS, L, LANES), lambda b, i: (0, 0, b, 0))
    smem = pl.BlockSpec(memory_space=pltpu.SMEM)
    if window:
        blk = (1, N_KV_HEADS, BLOCK, LANES)
        prev = pl.BlockSpec(blk, lambda b, i: (b, 0, jnp.maximum(i * qb - 1, 0), 0))
        cur = pl.BlockSpec((1, N_KV_HEADS, rows, LANES), lambda b, i: (b, 0, i, 0))
        nxt = pl.BlockSpec(blk, lambda b, i: (b, 0, jnp.minimum((i + 1) * qb, nb - 1), 0))
        in_specs = [smem, qspec, prev, cur, nxt, prev, cur, nxt, xspec, xspec]
        args = (sink, q, k, k, k, v, v, v, kx, vx)
    else:
        in_specs = [smem, qspec, xspec, xspec]
        args = (sink, q, kx, vx)
    out, *casted = pl.pallas_call(
        functools.partial(_att_kernel, n_steps=n_steps, qb=qb, window=window, n_cast=len(cast)),
        grid=(B, n_steps),
        in_specs=in_specs + cast_in,
        out_specs=[qspec] + cast_out,
        out_shape=[jax.ShapeDtypeStruct((B, S, ATTN_WIDTH), BF16)] + cast_shapes,
        compiler_params=_params("arbitrary", "arbitrary"),
        name="window_attn" if window else "ctx_attn",
    )(*args, *(w for w, _, _ in cast))
    return (out, casted) if cast else out


def _pool_fill(xs_ref, h_ref, hp_ref, hx_ref, g, sh, sc, lo, hi, tm, first_tile, last_tile):
    hal = POOL_HALO
    if lo == 0:
        xs_ref[0:hal] = jnp.where(first_tile, 0.0, _rms_mod(hp_ref[0], g, sh, sc))
    xs_ref[hal + lo:hal + hi] = _rms_mod(h_ref[0, lo:hi], g, sh, sc)
    if hi == tm:
        xs_ref[hal + tm:2 * hal + tm] = jnp.where(last_tile, 0.0, _rms_mod(hx_ref[0], g, sh, sc))
        xs_ref[2 * hal + tm:] = jnp.zeros((POOL_PAD, D_MODEL), F32)


def _pool_part(xs_ref, p_ref, ic_ref, pw_ref, r0, pr):
    hal = POOL_HALO
    ys = []
    for gi, w in enumerate(POOL_WINDOWS):
        cs = slice(gi * POOL_GROUP, (gi + 1) * POOL_GROUP)
        x = lambda off, n: xs_ref[r0 + off:r0 + off + n, cs]
        if w == 2:
            acc = x(hal - 1, pr) + x(hal, pr)
        else:
            n2, n4, n8 = pr + 3 * hal, pr + 2 * hal, pr + hal
            p_ref[0:n2] = x(0, n2) + x(1, n2)
            if w == 4:
                acc = p_ref[hal - 2:hal - 2 + pr] + p_ref[hal:hal + pr]
            else:
                p_ref[0:n4] = p_ref[0:n4] + p_ref[2:n4 + 2]
                if w == 8:
                    acc = p_ref[hal - 4:hal - 4 + pr] + p_ref[hal:hal + pr]
                else:
                    p_ref[0:n8] = p_ref[0:n8] + p_ref[4:n8 + 4]
                    acc = p_ref[0:pr] + p_ref[hal:hal + pr]
        yg = acc * ic_ref[r0:r0 + pr, gi:gi + 1] - x(hal, pr)
        ys.append(_dot(yg.astype(BF16), pw_ref[gi]))
    return jnp.concatenate(ys, axis=1)


def _ffn_parts(tm, mode):
    return max(1, tm // (FFN_POOL_PART_ROWS if mode == "pool" else FFN_PART_ROWS))


def _ffn_kernel(*refs, tm, seq, mode, final):
    refs = list(refs)
    take = lambda n: [refs.pop(0) for _ in range(n)]
    (h_ref,) = take(1)
    if mode == "proj":
        four_ref, attn_ref, wo_ref, g1_ref = take(4)
    elif mode == "pool":
        hp_ref, hx_ref, gm_ref, sh1_ref, sc1_ref, g1_ref, pw_ref, ps_ref, ic_ref = take(9)
    gn_ref, sh_ref, sc_ref, g2_ref, w1_ref, w3_ref, w2_ref = take(7)
    if final:
        (fg_ref,) = take(1)
    (o_ref,) = take(1)
    if mode == "pool":
        xs_ref, p_ref = take(2)
        first_tile = pl.program_id(1) == 0
        last_tile = pl.program_id(1) == seq // tm - 1
    n_parts = _ffn_parts(tm, mode)
    pr = tm // n_parts
    rows = [slice(p * pr, (p + 1) * pr) for p in range(n_parts)]

    def mixed(p):
        h = h_ref[0, rows[p]]
        if mode == "proj":
            y = (_dot(four_ref[0, rows[p]], wo_ref[:FOURIER_WIDTH])
                 + _dot(attn_ref[0, rows[p]], wo_ref[FOURIER_WIDTH:]))
            return h + _v(g1_ref) * y
        if mode == "pool":
            lo = 0 if p == 0 else p * pr + POOL_PAD
            hi = tm if p == n_parts - 1 else (p + 1) * pr + POOL_PAD
            _pool_fill(xs_ref, h_ref, hp_ref, hx_ref, gm_ref[...], _v(sh1_ref), _v(sc1_ref), lo, hi, tm,
                       first_tile, last_tile)
            y = _pool_part(xs_ref, p_ref, ic_ref, pw_ref, p * pr, pr) * ps_ref[...]
            return h + _v(g1_ref) * y
        return h

    def normed(h):
        return _rms_mod(h, gn_ref[...], _v(sh_ref), _v(sc_ref)).astype(BF16)

    def up(hn, c0, cn):
        return _dot(hn, w1_ref[0, :, c0:c0 + cn]), _dot(hn, w3_ref[0, :, c0:c0 + cn])

    c0, cn = FFN_CHUNKS[0]
    if mode == "pool":
        hs, hns, ab = [], [], []
        for p in range(n_parts):
            hs.append(mixed(p))
            hns.append(normed(hs[p]))
            ab.append(up(hns[p], c0, cn))
    else:
        hs = [mixed(p) for p in range(n_parts)]
        hns = [normed(h) for h in hs]
        ab = [up(hn, c0, cn) for hn in hns]
    accs = [None] * n_parts
    for ci, (c0, cn) in enumerate(FFN_CHUNKS):
        if ci > 0:
            ab = [up(hn, c0, cn) for hn in hns]
        for p, (a, b) in enumerate(ab):
            t = (jax.nn.silu(a) * b).astype(BF16)
            d = _dot(t, w2_ref[0, c0:c0 + cn, :])
            accs[p] = d if accs[p] is None else accs[p] + d
    for p, r in enumerate(rows):
        out = hs[p] + _v(g2_ref) * accs[p]
        if final:
            out = _rms(out, fg_ref[...])
        o_ref[0, r] = out


def _ffn(h, mix, gn, mods, layer, ctx, ffn_w, final_g, tm):
    B, S, _ = h.shape
    mode = "none" if mix is None else mix[0]
    vecs = [_mod_operand(mods, layer, k, ctx) for k in range(6)]
    vspec = [v[0] for v in vecs]
    row = pl.BlockSpec((1, D_MODEL), lambda b, i: (0, 0))
    tok = lambda w: pl.BlockSpec((1, tm, w), lambda b, i: (b, i, 0))
    whole = pl.BlockSpec(memory_space=pltpu.VMEM)
    in_specs = [tok(D_MODEL)]
    args = [h]
    scratch = []
    if mode == "proj":
        _, four, attn, w_out = mix
        in_specs += [tok(FOURIER_WIDTH), tok(ATTN_WIDTH), whole, vspec[2]]
        args += [four, attn, w_out, mods]
    elif mode == "pool":
        _, g_mix, pool_w, pool_scale = mix
        hb = tm // POOL_HALO
        halo = (1, POOL_HALO, D_MODEL)
        t = np.arange(S)
        inv_count = jnp.asarray(np.stack(
            [1.0 / (np.minimum(t + w // 2, S) - np.maximum(t - w // 2, 0)) for w in POOL_WINDOWS], axis=1), F32)
        in_specs += [
            pl.BlockSpec(halo, lambda b, i: (b, jnp.maximum(i * hb - 1, 0), 0)),
            pl.BlockSpec(halo, lambda b, i: (b, jnp.minimum((i + 1) * hb, S // POOL_HALO - 1), 0)),
            row, vspec[0], vspec[1], vspec[2],
            pl.BlockSpec((len(POOL_WINDOWS), POOL_GROUP, POOL_GROUP), lambda b, i: (0, 0, 0)),
            row,
            pl.BlockSpec((tm, len(POOL_WINDOWS)), lambda b, i: (i, 0)),
        ]
        args += [h, h, g_mix, mods, mods, mods, pool_w, pool_scale, inv_count]
        pr = tm // _ffn_parts(tm, mode)
        scratch = [pltpu.VMEM((tm + 2 * POOL_HALO + POOL_PAD, D_MODEL), F32),
                   pltpu.VMEM((pr + 3 * POOL_HALO, POOL_GROUP), F32)]
    in_specs += [row, vspec[3], vspec[4], vspec[5]] + [
        pl.BlockSpec((1,) + w.shape[1:], lambda b, i, n=w.shape[0]: (layer % n, 0, 0),
                     pipeline_mode=pl.Buffered(1)) for w in ffn_w]
    args += [gn, mods, mods, mods, *ffn_w]
    if final_g is not None:
        in_specs.append(row)
        args.append(final_g)
    return pl.pallas_call(
        functools.partial(_ffn_kernel, tm=tm, seq=S, mode=mode, final=final_g is not None),
        grid=(B, S // tm),
        in_specs=in_specs,
        out_specs=tok(D_MODEL),
        out_shape=jax.ShapeDtypeStruct((B, S, D_MODEL), F32),
        scratch_shapes=scratch,
        compiler_params=_params("arbitrary", "arbitrary"),
        name="ffn_" + mode,
    )(*args)


def _dft_cos_sin(n):
    idx = np.arange(n, dtype=np.int64)
    ang = 2.0 * np.pi * ((idx[:, None] * idx[None, :]) % n) / n
    return np.cos(ang), np.sin(ang)


def _rope_tables(n_tokens):
    rows = n_tokens // GRID_W
    row = jnp.repeat(jnp.arange(rows, dtype=F32), GRID_W)
    col = jnp.tile(jnp.arange(GRID_W, dtype=F32), rows)
    n_freq = HEAD_DIM // 4
    inv = ROPE_THETA ** (-jnp.arange(n_freq, dtype=F32) / n_freq)
    ang = jnp.concatenate([row[:, None] * inv[None], col[:, None] * inv[None]], axis=-1)
    cos = jnp.repeat(jnp.cos(ang), 2, axis=-1)
    sin = jnp.repeat(jnp.sin(ang), 2, axis=-1)
    sign = jnp.tile(jnp.asarray([-1.0, 1.0], F32), HEAD_DIM // 2)
    return jnp.tile(cos, (1, 2)), jnp.tile(sin * sign, (1, 2))


def _fourier_tables():
    r = FFT_RADIX
    c, s = _dft_cos_sin(r)
    w1 = np.block([[c, s], [-s, c]])
    n_pos = r * r
    idx = np.arange(r, dtype=np.int64)
    k = idx[:, None, None] + r * idx[None, :, None]
    ang = 2.0 * np.pi * ((k * idx[None, None, :]) % n_pos) / n_pos
    scale = (n_pos * FOURIER_HEAD_DIM) ** -0.5
    w2 = np.concatenate([np.cos(ang), np.sin(ang)], axis=2) * scale
    cc, sc = _dft_cos_sin(FOURIER_HEAD_DIM)
    wc = np.concatenate([cc, -sc], axis=1)
    f32 = lambda a: jnp.asarray(np.ascontiguousarray(a), F32)
    return f32(w1).astype(BF16), f32(w2).astype(BF16), f32(wc).astype(BF16)


def kernel(x, c, ctx, c_ctx, ada_w, ada_b, norm_mix_g, norm_ffn_g, mix_in_w, mix_out_w, attn_sink,
           pool_w, pool_scale, ffn_w1, ffn_w3, ffn_w2, final_g):
    B, S, _ = x.shape
    L = ctx.shape[1]
    tm_ffn = 2 * FFN_PART_ROWS
    tm_ctx = L
    tm_flat = min(B * L, 2 * FFN_PART_ROWS)

    cond = jnp.zeros((ADA_ROWS, D_MODEL), F32).at[:B].set(c).at[B].set(c_ctx)
    mods = _ada(cond, ada_w, ada_b)

    cosf, sinf = _rope_tables(S)
    cos_id = jnp.ones((B * L, LANES), F32)
    sin_id = jnp.zeros((B * L, LANES), F32)
    w1_dft, w2_dft, wc = _fourier_tables()
    cl, sl_ = _dft_cos_sin(L)
    w_ctx_dft = jnp.asarray(np.concatenate([cl, sl_], axis=1), F32).astype(BF16)

    last_ctx_reader = max(range(0, DEPTH, 2))
    h, hc = x, ctx
    for layer in range(DEPTH):
        update_ctx = layer < last_ctx_reader
        g_mix = norm_mix_g[layer].reshape(1, D_MODEL)
        g_ffn = norm_ffn_g[layer].reshape(1, D_MODEL)
        fin = final_g.reshape(1, D_MODEL) if layer == DEPTH - 1 else None
        j = layer // 2
        if layer % 2 == 0:
            w_in = mix_in_w[j].astype(BF16)
            w_out = mix_out_w[j].astype(BF16)
            sink = attn_sink[j]
            flat = lambda a: a.reshape(1, B * L, a.shape[-1])
            unflat = lambda a: a.reshape(B, L, a.shape[-1])
            zr_c, zi_c, q_c, k_c, v_c = _in_proj(flat(hc), g_mix, mods, layer, True, w_in, cos_id, sin_id, wc,
                                                 tm_flat)
            zr, zi, q, k, v = _in_proj(h, g_mix, mods, layer, False, w_in, cosf, sinf, wc, 4 * IN_PART_ROWS)
            four = _fourier_4096(zr, zi, w1_dft, w2_dft)
            f32_w = (ffn_w1, ffn_w3, ffn_w2)
            cast = [(w.reshape(-1, w.shape[-1]), layer * w.shape[1], 2 * w.shape[1]) for w in f32_w]
            attn, casted = _attention(q, k, v, k_c, v_c, sink, True, cast=cast)
            ffn_w = tuple(cw.reshape(2, *w.shape[1:]) for cw, w in zip(casted, f32_w))
            h = _ffn(h, ("proj", four, attn, w_out), g_ffn, mods, layer, False, ffn_w, fin, tm_ffn)
            if update_ctx:
                four_c = _fourier_small(unflat(zr_c), unflat(zi_c), w_ctx_dft)
                attn_c = _attention(unflat(q_c), None, None, k_c, v_c, sink, False)
                hc = unflat(_ffn(flat(hc), ("proj", flat(four_c), flat(attn_c), w_out), g_ffn, mods, layer, True,
                                 ffn_w, None, tm_flat))
        else:
            pw = pool_w[j].astype(BF16)
            psc = pool_scale[j].reshape(1, D_MODEL)
            h = _ffn(h, ("pool", g_mix, pw, psc), g_ffn, mods, layer, False, ffn_w, fin, tm_ffn)
            if update_ctx:
                hc = _ffn(hc, ("pool", g_mix, pw, psc), g_ffn, mods, layer, True, ffn_w, None, tm_ctx)
    return h
```

```python
import functools

import numpy as np
import jax
import jax.numpy as jnp
from jax import lax
from jax.experimental import pallas as pl
from jax.experimental.pallas import tpu as pltpu

F32 = jnp.float32
BF16 = jnp.bfloat16

D_MODEL = 1024
DEPTH = 4
GRID_W = 64
EPS = 1e-6
FOURIER_HEADS = 4
FOURIER_HEAD_DIM = 128
FOURIER_WIDTH = 512
HEAD_DIM = 64
N_KV_HEADS = 2
GQA_GROUP = 4
ATTN_WIDTH = 512
KV_WIDTH = 128
Q_END = 1024
IN_WIDTH = 1280
BLOCK = 128
ATT_QBLOCKS = 16
ROPE_THETA = 10000.0
POOL_WINDOWS = (2, 4, 8, 16)
POOL_GROUP = 256
POOL_HALO = 8
POOL_PAD = 2 * POOL_HALO
FFN_CHUNKS = ((0, 1024), (1024, 1024), (2048, 768))
FFN_TILE_ROWS = 1024
FFN_PART_ROWS = 256
IN_PART_ROWS = 512
FFT_RADIX = 64
FFT_ROWS = 16
LANES = 128
BF16_ROWS = 16
LOG2E = 1.4426950408889634
ADA_ROWS = 16
ADA_CTX_ROW = 8
ADA_STEP_CHUNKS = 3
VMEM_LIMIT = 56 * 1024 * 1024


def _params(*sem):
    return pltpu.CompilerParams(dimension_semantics=sem, vmem_limit_bytes=VMEM_LIMIT)


def _rms(x, g):
    ms = jnp.mean(x * x, axis=-1, keepdims=True)
    return x * lax.rsqrt(ms + EPS) * g


def _rms_mod(x, g, shift, scale):
    return _rms(x, g) * (1.0 + scale) + shift


def _dot(a, b):
    return jnp.dot(a, b, preferred_element_type=F32)


def _ada_kernel(c_ref, w_ref, b_ref, o_ref):
    a = jax.nn.silu(c_ref[...])
    w = w_ref[0]
    a_hi = a.astype(BF16)
    a_lo = (a - a_hi.astype(F32)).astype(BF16)
    w_hi = w.astype(BF16)
    w_lo = (w - w_hi.astype(F32)).astype(BF16)
    m2 = _dot(jnp.concatenate([a_hi, a_lo], axis=0), w_hi)
    m = m2[:ADA_ROWS] + m2[ADA_ROWS:] + _dot(a_hi, w_lo) + b_ref[0]
    for c in range(ADA_STEP_CHUNKS):
        for r in range(ADA_ROWS):
            o_ref[0, c, r] = m[r:r + 1, c * D_MODEL:(c + 1) * D_MODEL]


def _ada(cond, ada_w, ada_b):
    n_chunks = ada_w.shape[-1] // D_MODEL
    tn = ADA_STEP_CHUNKS * D_MODEL
    return pl.pallas_call(
        _ada_kernel,
        grid=(DEPTH, n_chunks // ADA_STEP_CHUNKS),
        in_specs=[
            pl.BlockSpec((ADA_ROWS, D_MODEL), lambda l, n: (0, 0)),
            pl.BlockSpec((1, D_MODEL, tn), lambda l, n: (l, 0, n)),
            pl.BlockSpec((1, 1, tn), lambda l, n: (l, 0, n)),
        ],
        out_specs=pl.BlockSpec((1, ADA_STEP_CHUNKS, ADA_ROWS, 1, D_MODEL), lambda l, n: (l, n, 0, 0, 0)),
        out_shape=jax.ShapeDtypeStruct((DEPTH, n_chunks, ADA_ROWS, 1, D_MODEL), F32),
        compiler_params=_params("arbitrary", "arbitrary"),
        name="ada_mod",
    )(cond, ada_w, ada_b.reshape(DEPTH, 1, ada_w.shape[-1]))


def _mod_operand(mods, layer, chunk, ctx):
    spec = pl.BlockSpec((1, 1, 1, 1, D_MODEL),
                        lambda b, i: (layer, chunk, ADA_CTX_ROW if ctx else b, 0, 0))
    return spec, mods


def _v(ref):
    return ref[0, 0, 0]


def _in_kernel(h_ref, g_ref, sh_ref, sc_ref, w_ref, cos_ref, sin_ref, wc_ref,
               zr_ref, zi_ref, q_ref, k_ref, v_ref, *, tm):
    n_parts = max(1, tm // IN_PART_ROWS)
    pr = tm // n_parts
    rows = [slice(p * pr, (p + 1) * pr) for p in range(n_parts)]
    projs = [_dot(_rms_mod(h_ref[0, r], g_ref[...], _v(sh_ref), _v(sc_ref)).astype(BF16), w_ref[...])
             for r in rows]
    wc = wc_ref[...]
    zs = [[_dot(proj[:, hh * LANES:(hh + 1) * LANES].astype(BF16), wc) for hh in range(FOURIER_HEADS)]
          for proj in projs]
    lane = lax.broadcasted_iota(jnp.int32, (pr, LANES), 1)
    even = (lane & 1) == 0
    low = lane < HEAD_DIM
    for r, proj, z4 in zip(rows, projs, zs):
        cosf = cos_ref[r]
        sinf = sin_ref[r]

        def rope(t):
            partner = jnp.where(even, pltpu.roll(t, LANES - 1, 1), pltpu.roll(t, 1, 1))
            return t * cosf + partner * sinf

        for hh, z in enumerate(z4):
            sl = slice(hh * LANES, (hh + 1) * LANES)
            zr_ref[0, r, sl] = z[:, :LANES].astype(BF16)
            zi_ref[0, r, sl] = z[:, LANES:].astype(BF16)
        for p in range(ATTN_WIDTH // LANES):
            sl = slice(p * LANES, (p + 1) * LANES)
            qq = proj[:, FOURIER_WIDTH + p * LANES:FOURIER_WIDTH + (p + 1) * LANES]
            q_ref[0, r, sl] = (rope(qq) * (HEAD_DIM ** -0.5 * LOG2E)).astype(BF16)
        kk = rope(proj[:, Q_END:Q_END + KV_WIDTH])
        vv = proj[:, Q_END + KV_WIDTH:]
        k_sw = pltpu.roll(kk, HEAD_DIM, 1)
        v_sw = pltpu.roll(vv, HEAD_DIM, 1)
        k_ref[0, 0, r] = jnp.where(low, kk, k_sw).astype(BF16)
        k_ref[0, 1, r] = jnp.where(low, k_sw, kk).astype(BF16)
        v_ref[0, 0, r] = jnp.where(low, vv, v_sw).astype(BF16)
        v_ref[0, 1, r] = jnp.where(low, v_sw, vv).astype(BF16)


def _in_proj(h, g, mods, layer, ctx, w_in, cosf, sinf, wc, tm):
    B, S, _ = h.shape
    (shift_spec, shift), (scale_spec, scale) = (_mod_operand(mods, layer, k, ctx) for k in (0, 1))
    tok = lambda w: pl.BlockSpec((1, tm, w), lambda b, i: (b, i, 0))
    kv = pl.BlockSpec((1, N_KV_HEADS, tm, LANES), lambda b, i: (b, 0, i, 0))
    return pl.pallas_call(
        functools.partial(_in_kernel, tm=tm),
        grid=(B, S // tm),
        in_specs=[
            tok(D_MODEL),
            pl.BlockSpec((1, D_MODEL), lambda b, i: (0, 0)),
            shift_spec, scale_spec,
            pl.BlockSpec((D_MODEL, IN_WIDTH), lambda b, i: (0, 0)),
            pl.BlockSpec((tm, LANES), lambda b, i: (i, 0)),
            pl.BlockSpec((tm, LANES), lambda b, i: (i, 0)),
            pl.BlockSpec((FOURIER_HEAD_DIM, 2 * FOURIER_HEAD_DIM), lambda b, i: (0, 0)),
        ],
        out_specs=[tok(FOURIER_WIDTH), tok(FOURIER_WIDTH), tok(ATTN_WIDTH), kv, kv],
        out_shape=[
            jax.ShapeDtypeStruct((B, S, FOURIER_WIDTH), BF16),
            jax.ShapeDtypeStruct((B, S, FOURIER_WIDTH), BF16),
            jax.ShapeDtypeStruct((B, S, ATTN_WIDTH), BF16),
            jax.ShapeDtypeStruct((B, N_KV_HEADS, S, LANES), BF16),
            jax.ShapeDtypeStruct((B, N_KV_HEADS, S, LANES), BF16),
        ],
        compiler_params=_params("arbitrary", "arbitrary"),
        name="in_proj",
    )(h, g, shift, scale, w_in, cosf, sinf, wc)


def _fft_kernel(zr_ref, zi_ref, w1_ref, w2_ref, o_ref, y_ref, *, t):
    r = FFT_RADIX
    w1 = w1_ref[...]
    for c in range(r // t):
        cs = slice(c * t, (c + 1) * t)
        xr = jnp.swapaxes(zr_ref[0, :, cs, :], 0, 1)
        xi = jnp.swapaxes(zi_ref[0, :, cs, :], 0, 1)
        outs_r, outs_i = [], []
        for j in range(t):
            y = _dot(w1, jnp.concatenate([xr[j], xi[j]], axis=0))
            outs_r.append(y[:r].astype(BF16))
            outs_i.append(y[r:].astype(BF16))
        y_ref[0, :, cs, :] = jnp.swapaxes(jnp.stack(outs_r, axis=0), 0, 1)
        y_ref[1, :, cs, :] = jnp.swapaxes(jnp.stack(outs_i, axis=0), 0, 1)
    for c in range(r // t):
        outs = []
        for j in range(t):
            k1 = c * t + j
            x = jnp.concatenate([y_ref[0, k1], y_ref[1, k1]], axis=0)
            outs.append(_dot(w2_ref[k1], x).astype(BF16))
        o_ref[0, :, c * t:(c + 1) * t, :] = jnp.swapaxes(jnp.stack(outs, axis=0), 0, 1)


def _fourier_4096(zr, zi, w1, w2):
    B, S, W = zr.shape
    r = FFT_RADIX
    zr4 = zr.reshape(B, r, r, W)
    zi4 = zi.reshape(B, r, r, W)
    blk = pl.BlockSpec((1, r, r, W), lambda b: (b, 0, 0, 0))
    out = pl.pallas_call(
        functools.partial(_fft_kernel, t=FFT_ROWS),
        grid=(B,),
        in_specs=[blk, blk, pl.BlockSpec((2 * r, 2 * r), lambda b: (0, 0)),
                  pl.BlockSpec((r, r, 2 * r), lambda b: (0, 0, 0))],
        out_specs=blk,
        out_shape=jax.ShapeDtypeStruct((B, r, r, W), BF16),
        scratch_shapes=[pltpu.VMEM((2, r, r, W), BF16)],
        compiler_params=_params("arbitrary"),
        name="fft_4096",
    )(zr4, zi4, w1, w2)
    return out.reshape(B, S, W)


def _dft_small_kernel(zr_ref, zi_ref, w_ref, o_ref, *, scale):
    x = jnp.concatenate([zr_ref[0], zi_ref[0]], axis=0)
    o_ref[0] = (_dot(w_ref[...], x) * scale).astype(BF16)


def _fourier_small(zr, zi, w):
    B, L, W = zr.shape
    scale = float((L * FOURIER_HEAD_DIM) ** -0.5)
    blk = pl.BlockSpec((1, L, W), lambda b: (b, 0, 0))
    return pl.pallas_call(
        functools.partial(_dft_small_kernel, scale=scale),
        grid=(B,),
        in_specs=[blk, blk, pl.BlockSpec((L, 2 * L), lambda b: (0, 0))],
        out_specs=blk,
        out_shape=jax.ShapeDtypeStruct((B, L, W), BF16),
        compiler_params=_params("arbitrary"),
        name="dft_ctx",
    )(zr, zi, w)


def _att_kernel(sink_ref, q_ref, *refs, n_steps, qb, window, n_cast):
    n_kv = 8 if window else 2
    kv_refs, cast_in = refs[:n_kv], refs[n_kv:n_kv + n_cast]
    o_ref, cast_out = refs[n_kv + n_cast], refs[n_kv + n_cast + 1:]
    if window:
        kp_ref, kc_ref, kn_ref, vp_ref, vc_ref, vn_ref, kx_ref, vx_ref = kv_refs
    else:
        kx_ref, vx_ref = kv_refs
    for src, dst in zip(cast_in, cast_out):
        dst[...] = src[...].astype(BF16)
    i = pl.program_id(1)
    lane = lax.broadcasted_iota(jnp.int32, (BLOCK, LANES), 1)
    low = lane < HEAD_DIM
    if window:
        row = lax.broadcasted_iota(jnp.int32, (BLOCK, BLOCK), 0)
        col = lax.broadcasted_iota(jnp.int32, (BLOCK, BLOCK), 1)
        ninf = jnp.float32(-jnp.inf)
        before = col >= row
        after = col <= row

    def window_tiles(p_ref, c_ref, n_ref, kh, a):
        cur = lambda j: c_ref[0, kh, j * BLOCK:(j + 1) * BLOCK]
        first = p_ref[0, kh] if a == 0 else cur(a - 1)
        last = n_ref[0, kh] if a == qb - 1 else cur(a + 1)
        return [first, cur(a), last]

    def scores(a, kh):
        if window:
            k = jnp.concatenate(window_tiles(kp_ref, kc_ref, kn_ref, kh, a) + [kx_ref[0, kh]], axis=0)
        else:
            k = kx_ref[0, kh]
        qs = []
        for g in range(GQA_GROUP):
            h = kh * GQA_GROUP + g
            qp = q_ref[0, a * BLOCK:(a + 1) * BLOCK, (h // 2) * LANES:(h // 2 + 1) * LANES]
            keep = low if h % 2 == 0 else jnp.logical_not(low)
            qs.append(jnp.where(keep, qp, jnp.zeros_like(qp)))
        qst = jnp.concatenate(qs, axis=0)
        return lax.dot_general(qst, k, (((1,), (1,)), ((), ())), preferred_element_type=F32)

    def finish(a, kh, s):
        if window:
            v = jnp.concatenate(window_tiles(vp_ref, vc_ref, vn_ref, kh, a) + [vx_ref[0, kh]], axis=0)
            ok_first = before & (i > 0) if a == 0 else before
            ok_last = after & (i < n_steps - 1) if a == qb - 1 else after
            b_first = jnp.where(ok_first, 0.0, ninf)
            b_last = jnp.where(ok_last, 0.0, ninf)
        else:
            v = vx_ref[0, kh]
        ps, ls = [], []
        for g in range(GQA_GROUP):
            sk = sink_ref[kh * GQA_GROUP + g] * LOG2E
            sg = s[g * BLOCK:(g + 1) * BLOCK]
            if window:
                sg = jnp.concatenate(
                    [sg[:, :BLOCK] + b_first, sg[:, BLOCK:2 * BLOCK],
                     sg[:, 2 * BLOCK:3 * BLOCK] + b_last, sg[:, 3 * BLOCK:]], axis=1)
            m = jnp.maximum(jnp.max(sg, axis=-1, keepdims=True), sk)
            p = jnp.exp2(sg - m)
            ls.append(jnp.sum(p, axis=-1, keepdims=True) + jnp.exp2(sk - m))
            ps.append(p.astype(BF16))
        o = _dot(jnp.concatenate(ps, axis=0), v)
        og = [o[g * BLOCK:(g + 1) * BLOCK] / ls[g] for g in range(GQA_GROUP)]
        for pr in range(GQA_GROUP // 2):
            c0 = kh * GQA_GROUP * HEAD_DIM + pr * LANES
            o_ref[0, a * BLOCK:(a + 1) * BLOCK, c0:c0 + LANES] = jnp.where(
                low, og[2 * pr], og[2 * pr + 1]).astype(BF16)

    units = [(a, kh) for a in range(qb) for kh in range(N_KV_HEADS)]
    ahead = 2
    pending = [scores(*u) for u in units[:ahead]]
    for n, u in enumerate(units):
        s = pending.pop(0)
        if n + ahead < len(units):
            pending.append(scores(*units[n + ahead]))
        finish(*u, s)


def _cast_operands(cast, grid_steps, flat_step):
    in_specs, out_specs, out_shapes = [], [], []
    for w, first, n in cast:
        per, every = n // grid_steps, 1
        while per % BF16_ROWS:
            per, every = 2 * per, 2 * every
        assert n % per == 0 and first % per == 0
        blk = (per, w.shape[1])
        in_specs.append(pl.BlockSpec(blk, lambda b, i, e=every, o=first // per: (o + flat_step(b, i) // e, 0)))
        out_specs.append(pl.BlockSpec(blk, lambda b, i, e=every: (flat_step(b, i) // e, 0)))
        out_shapes.append(jax.ShapeDtypeStruct((n, w.shape[1]), BF16))
    return in_specs, out_specs, out_shapes


def _attention(q, k, v, kx, vx, sink, window, cast=()):
    B, S, _ = q.shape
    nb = S // BLOCK
    qb = min(ATT_QBLOCKS, nb)
    rows = qb * BLOCK
    n_steps = S // rows
    L = kx.shape[2] // (B // kx.shape[0])
    cast_in, cast_out, cast_shapes = _cast_operands(cast, B * n_steps, lambda b, i: b * n_steps + i)
    qspec = pl.BlockSpec((1, rows, ATTN_WIDTH), lambda b, i: (b, i, 0))
    if kx.shape[0] == B:
        xspec = pl.BlockSpec((1, N_KV_HEADS, L, LANES), lambda b, i: (b, 0, 0, 0))
    else:
        xspec = pl.BlockSpec((1, N_KV_HEADS, L, LANES), lambda b, i: (0, 0, b, 0))
    smem = pl.BlockSpec(memory_space=pltpu.SMEM)
    if window:
        blk = (1, N_KV_HEADS, BLOCK, LANES)
        prev = pl.BlockSpec(blk, lambda b, i: (b, 0, jnp.maximum(i * qb - 1, 0), 0))
        cur = pl.BlockSpec((1, N_KV_HEADS, rows, LANES), lambda b, i: (b, 0, i, 0))
        nxt = pl.BlockSpec(blk, lambda b, i: (b, 0, jnp.minimum((i + 1) * qb, nb - 1), 0))
        in_specs = [smem, qspec, prev, cur, nxt, prev, cur, nxt, xspec, xspec]
        args = (sink, q, k, k, k, v, v, v, kx, vx)
    else:
        in_specs = [smem, qspec, xspec, xspec]
        args = (sink, q, kx, vx)
    out, *casted = pl.pallas_call(
        functools.partial(_att_kernel, n_steps=n_steps, qb=qb, window=window, n_cast=len(cast)),
        grid=(B, n_steps),
        in_specs=in_specs + cast_in,
        out_specs=[qspec] + cast_out,
        out_shape=[jax.ShapeDtypeStruct((B, S, ATTN_WIDTH), BF16)] + cast_shapes,
        compiler_params=_params("arbitrary", "arbitrary"),
        name="window_attn" if window else "ctx_attn",
    )(*args, *(w for w, _, _ in cast))
    return (out, casted) if cast else out


def _pool_fill(xs_ref, h_ref, hp_ref, hx_ref, g, sh, sc, lo, hi, tm, first_tile, last_tile):
    hal = POOL_HALO
    if lo == 0:
        xs_ref[0:hal] = jnp.where(first_tile, 0.0, _rms_mod(hp_ref[0], g, sh, sc))
    xs_ref[hal + lo:hal + hi] = _rms_mod(h_ref[0, lo:hi], g, sh, sc)
    if hi == tm:
        xs_ref[hal + tm:2 * hal + tm] = jnp.where(last_tile, 0.0, _rms_mod(hx_ref[0], g, sh, sc))
        xs_ref[2 * hal + tm:] = jnp.zeros((POOL_PAD, D_MODEL), F32)


def _pool_part(xs_ref, p_ref, ic_ref, pw_ref, r0, pr):
    hal = POOL_HALO
    ys = []
    for gi, w in enumerate(POOL_WINDOWS):
        cs = slice(gi * POOL_GROUP, (gi + 1) * POOL_GROUP)
        x = lambda off, n: xs_ref[r0 + off:r0 + off + n, cs]
        if w == 2:
            acc = x(hal - 1, pr) + x(hal, pr)
        else:
            n2, n4, n8 = pr + 3 * hal, pr + 2 * hal, pr + hal
            p_ref[0:n2] = x(0, n2) + x(1, n2)
            if w == 4:
                acc = p_ref[hal - 2:hal - 2 + pr] + p_ref[hal:hal + pr]
            else:
                p_ref[0:n4] = p_ref[0:n4] + p_ref[2:n4 + 2]
                if w == 8:
                    acc = p_ref[hal - 4:hal - 4 + pr] + p_ref[hal:hal + pr]
                else:
                    p_ref[0:n8] = p_ref[0:n8] + p_ref[4:n8 + 4]
                    acc = p_ref[0:pr] + p_ref[hal:hal + pr]
        yg = acc * ic_ref[r0:r0 + pr, gi:gi + 1] - x(hal, pr)
        ys.append(_dot(yg.astype(BF16), pw_ref[gi]))
    return jnp.concatenate(ys, axis=1)


def _ffn_parts(tm, mode):
    return max(1, tm // FFN_PART_ROWS)


def _ffn_kernel(*refs, tm, seq, mode, final):
    refs = list(refs)
    take = lambda n: [refs.pop(0) for _ in range(n)]
    (h_ref,) = take(1)
    if mode == "proj":
        four_ref, attn_ref, wo_ref, g1_ref = take(4)
    elif mode == "pool":
        hp_ref, hx_ref, gm_ref, sh1_ref, sc1_ref, g1_ref, pw_ref, ps_ref, ic_ref = take(9)
    gn_ref, sh_ref, sc_ref, g2_ref, w1_ref, w3_ref, w2_ref = take(7)
    if final:
        (fg_ref,) = take(1)
    (o_ref,) = take(1)
    if mode == "pool":
        xs_ref, p_ref = take(2)
        first_tile = pl.program_id(1) == 0
        last_tile = pl.program_id(1) == seq // tm - 1
    n_parts = _ffn_parts(tm, mode)
    pr = tm // n_parts
    rows = [slice(p * pr, (p + 1) * pr) for p in range(n_parts)]

    def mixed(p):
        h = h_ref[0, rows[p]]
        if mode == "proj":
            y = (_dot(four_ref[0, rows[p]], wo_ref[:FOURIER_WIDTH])
                 + _dot(attn_ref[0, rows[p]], wo_ref[FOURIER_WIDTH:]))
            return h + _v(g1_ref) * y
        if mode == "pool":
            lo = 0 if p == 0 else p * pr + POOL_PAD
            hi = tm if p == n_parts - 1 else (p + 1) * pr + POOL_PAD
            _pool_fill(xs_ref, h_ref, hp_ref, hx_ref, gm_ref[...], _v(sh1_ref), _v(sc1_ref), lo, hi, tm,
                       first_tile, last_tile)
            y = _pool_part(xs_ref, p_ref, ic_ref, pw_ref, p * pr, pr) * ps_ref[...]
            return h + _v(g1_ref) * y
        return h

    def normed(h):
        return _rms_mod(h, gn_ref[...], _v(sh_ref), _v(sc_ref)).astype(BF16)

    def up(hn, c0, cn):
        return _dot(hn, w1_ref[0, :, c0:c0 + cn]), _dot(hn, w3_ref[0, :, c0:c0 + cn])

    c0, cn = FFN_CHUNKS[0]
    if mode == "pool":
        hs, hns, ab = [], [], []
        for p in range(n_parts):
            hs.append(mixed(p))
            hns.append(normed(hs[p]))
            ab.append(up(hns[p], c0, cn))
    else:
        hs = [mixed(p) for p in range(n_parts)]
        hns = [normed(h) for h in hs]
        ab = [up(hn, c0, cn) for hn in hns]
    accs = [None] * n_parts
    for ci, (c0, cn) in enumerate(FFN_CHUNKS):
        if ci > 0:
            ab = [up(hn, c0, cn) for hn in hns]
        for p, (a, b) in enumerate(ab):
            t = (jax.nn.silu(a) * b).astype(BF16)
            d = _dot(t, w2_ref[0, c0:c0 + cn, :])
            accs[p] = d if accs[p] is None else accs[p] + d
    for p, r in enumerate(rows):
        out = hs[p] + _v(g2_ref) * accs[p]
        if final:
            out = _rms(out, fg_ref[...])
        o_ref[0, r] = out


def _ffn(h, mix, gn, mods, layer, ctx, ffn_w, final_g, tm):
    B, S, _ = h.shape
    mode = "none" if mix is None else mix[0]
    vecs = [_mod_operand(mods, layer, k, ctx) for k in range(6)]
    vspec = [v[0] for v in vecs]
    row = pl.BlockSpec((1, D_MODEL), lambda b, i: (0, 0))
    tok = lambda w: pl.BlockSpec((1, tm, w), lambda b, i: (b, i, 0))
    whole = pl.BlockSpec(memory_space=pltpu.VMEM)
    in_specs = [tok(D_MODEL)]
    args = [h]
    scratch = []
    if mode == "proj":
        _, four, attn, w_out = mix
        in_specs += [tok(FOURIER_WIDTH), tok(ATTN_WIDTH), whole, vspec[2]]
        args += [four, attn, w_out, mods]
    elif mode == "pool":
        _, g_mix, pool_w, pool_scale = mix
        hb = tm // POOL_HALO
        halo = (1, POOL_HALO, D_MODEL)
        t = np.arange(S)
        inv_count = jnp.asarray(np.stack(
            [1.0 / (np.minimum(t + w // 2, S) - np.maximum(t - w // 2, 0)) for w in POOL_WINDOWS], axis=1), F32)
        in_specs += [
            pl.BlockSpec(halo, lambda b, i: (b, jnp.maximum(i * hb - 1, 0), 0)),
            pl.BlockSpec(halo, lambda b, i: (b, jnp.minimum((i + 1) * hb, S // POOL_HALO - 1), 0)),
            row, vspec[0], vspec[1], vspec[2],
            pl.BlockSpec((len(POOL_WINDOWS), POOL_GROUP, POOL_GROUP), lambda b, i: (0, 0, 0)),
            row,
            pl.BlockSpec((tm, len(POOL_WINDOWS)), lambda b, i: (i, 0)),
        ]
        args += [h, h, g_mix, mods, mods, mods, pool_w, pool_scale, inv_count]
        pr = tm // _ffn_parts(tm, mode)
        scratch = [pltpu.VMEM((tm + 2 * POOL_HALO + POOL_PAD, D_MODEL), F32),
                   pltpu.VMEM((pr + 3 * POOL_HALO, POOL_GROUP), F32)]
    in_specs += [row, vspec[3], vspec[4], vspec[5]] + [
        pl.BlockSpec((1,) + w.shape[1:], lambda b, i, n=w.shape[0]: (layer % n, 0, 0),
                     pipeline_mode=pl.Buffered(1)) for w in ffn_w]
    args += [gn, mods, mods, mods, *ffn_w]
    if final_g is not None:
        in_specs.append(row)
        args.append(final_g)
    return pl.pallas_call(
        functools.partial(_ffn_kernel, tm=tm, seq=S, mode=mode, final=final_g is not None),
        grid=(B, S // tm),
        in_specs=in_specs,
        out_specs=tok(D_MODEL),
        out_shape=jax.ShapeDtypeStruct((B, S, D_MODEL), F32),
        scratch_shapes=scratch,
        compiler_params=_params("arbitrary", "arbitrary"),
        name="ffn_" + mode,
    )(*args)


def _dft_cos_sin(n):
    idx = np.arange(n, dtype=np.int64)
    ang = 2.0 * np.pi * ((idx[:, None] * idx[None, :]) % n) / n
    return np.cos(ang), np.sin(ang)


def _rope_tables(n_tokens):
    rows = n_tokens // GRID_W
    row = jnp.repeat(jnp.arange(rows, dtype=F32), GRID_W)
    col = jnp.tile(jnp.arange(GRID_W, dtype=F32), rows)
    n_freq = HEAD_DIM // 4
    inv = ROPE_THETA ** (-jnp.arange(n_freq, dtype=F32) / n_freq)
    ang = jnp.concatenate([row[:, None] * inv[None], col[:, None] * inv[None]], axis=-1)
    cos = jnp.repeat(jnp.cos(ang), 2, axis=-1)
    sin = jnp.repeat(jnp.sin(ang), 2, axis=-1)
    sign = jnp.tile(jnp.asarray([-1.0, 1.0], F32), HEAD_DIM // 2)
    return jnp.tile(cos, (1, 2)), jnp.tile(sin * sign, (1, 2))


def _fourier_tables():
    r = FFT_RADIX
    c, s = _dft_cos_sin(r)
    w1 = np.block([[c, s], [-s, c]])
    n_pos = r * r
    idx = np.arange(r, dtype=np.int64)
    k = idx[:, None, None] + r * idx[None, :, None]
    ang = 2.0 * np.pi * ((k * idx[None, None, :]) % n_pos) / n_pos
    scale = (n_pos * FOURIER_HEAD_DIM) ** -0.5
    w2 = np.concatenate([np.cos(ang), np.sin(ang)], axis=2) * scale
    cc, sc = _dft_cos_sin(FOURIER_HEAD_DIM)
    wc = np.concatenate([cc, -sc], axis=1)
    f32 = lambda a: jnp.asarray(np.ascontiguousarray(a), F32)
    return f32(w1).astype(BF16), f32(w2).astype(BF16), f32(wc).astype(BF16)


def kernel(x, c, ctx, c_ctx, ada_w, ada_b, norm_mix_g, norm_ffn_g, mix_in_w, mix_out_w, attn_sink,
           pool_w, pool_scale, ffn_w1, ffn_w3, ffn_w2, final_g):
    B, S, _ = x.shape
    L = ctx.shape[1]
    tm_ffn = FFN_TILE_ROWS
    tm_ctx = L
    tm_flat = min(B * L, FFN_TILE_ROWS)

    cond = jnp.zeros((ADA_ROWS, D_MODEL), F32).at[:B].set(c).at[B].set(c_ctx)
    mods = _ada(cond, ada_w, ada_b)

    cosf, sinf = _rope_tables(S)
    cos_id = jnp.ones((B * L, LANES), F32)
    sin_id = jnp.zeros((B * L, LANES), F32)
    w1_dft, w2_dft, wc = _fourier_tables()
    cl, sl_ = _dft_cos_sin(L)
    w_ctx_dft = jnp.asarray(np.concatenate([cl, sl_], axis=1), F32).astype(BF16)

    last_ctx_reader = max(range(0, DEPTH, 2))
    h, hc = x, ctx
    for layer in range(DEPTH):
        update_ctx = layer < last_ctx_reader
        g_mix = norm_mix_g[layer].reshape(1, D_MODEL)
        g_ffn = norm_ffn_g[layer].reshape(1, D_MODEL)
        fin = final_g.reshape(1, D_MODEL) if layer == DEPTH - 1 else None
        j = layer // 2
        if layer % 2 == 0:
            w_in = mix_in_w[j].astype(BF16)
            w_out = mix_out_w[j].astype(BF16)
            sink = attn_sink[j]
            flat = lambda a: a.reshape(1, B * L, a.shape[-1])
            unflat = lambda a: a.reshape(B, L, a.shape[-1])
            zr_c, zi_c, q_c, k_c, v_c = _in_proj(flat(hc), g_mix, mods, layer, True, w_in, cos_id, sin_id, wc,
                                                 tm_flat)
            zr, zi, q, k, v = _in_proj(h, g_mix, mods, layer, False, w_in, cosf, sinf, wc, 4 * IN_PART_ROWS)
            four = _fourier_4096(zr, zi, w1_dft, w2_dft)
            f32_w = (ffn_w1, ffn_w3, ffn_w2)
            cast = [(w.reshape(-1, w.shape[-1]), layer * w.shape[1], 2 * w.shape[1]) for w in f32_w]
            attn, casted = _attention(q, k, v, k_c, v_c, sink, True, cast=cast)
            ffn_w = tuple(cw.reshape(2, *w.shape[1:]) for cw, w in zip(casted, f32_w))
            h = _ffn(h, ("proj", four, attn, w_out), g_ffn, mods, layer, False, ffn_w, fin, tm_ffn)
            if update_ctx:
                four_c = _fourier_small(unflat(zr_c), unflat(zi_c), w_ctx_dft)
                attn_c = _attention(unflat(q_c), None, None, k_c, v_c, sink, False)
                hc = unflat(_ffn(flat(hc), ("proj", flat(four_c), flat(attn_c), w_out), g_ffn, mods, layer, True,
                                 ffn_w, None, tm_flat))
        else:
            pw = pool_w[j].astype(BF16)
            psc = pool_scale[j].reshape(1, D_MODEL)
            h = _ffn(h, ("pool", g_mix, pw, psc), g_ffn, mods, layer, False, ffn_w, fin, tm_ffn)
            if update_ctx:
                hc = _ffn(hc, ("pool", g_mix, pw, psc), g_ffn, mods, layer, True, ffn_w, None, tm_ctx)
    return h
```
